```python
import math, functools
import jax, jax.numpy as jnp
from jax import lax
import numpy as np

D_MODEL = 2048
BATCH = 2
SEQ = 8192
DEPTH = 2

GRID_W = 64
CTX_LEN = 256
HEAD_DIM = 64
ROPE_THETA = 10000.0
NORM_EPS = 1e-6
NEG_INF = -1e30

CONV_CH = D_MODEL // 4
CONV_K = 31
DIFF_HEADS = (3 * D_MODEL // 8) // (2 * HEAD_DIM)
DIFF_V_DIM = 2 * HEAD_DIM
SWA_Q_HEADS = (3 * D_MODEL // 8) // HEAD_DIM
SWA_KV_HEADS = SWA_Q_HEADS // 3
SWA_GROUP = SWA_Q_HEADS // SWA_KV_HEADS
WINDOW = 128
BLOCK = 128

D_MIX = CONV_CH + DIFF_HEADS * DIFF_V_DIM + SWA_Q_HEADS * HEAD_DIM

IN_A = 2 * CONV_CH
IN_BQ = 2 * DIFF_HEADS * HEAD_DIM
IN_CQ = SWA_Q_HEADS * HEAD_DIM
IN_BK = 2 * DIFF_HEADS * HEAD_DIM
IN_BV = DIFF_HEADS * DIFF_V_DIM
IN_CK = SWA_KV_HEADS * HEAD_DIM
IN_CV = SWA_KV_HEADS * HEAD_DIM
OFF_Q = IN_A
OFF_KV = IN_A + IN_BQ + IN_CQ
D_IN = OFF_KV + IN_BK + IN_BV + IN_CK + IN_CV

D_FF = 5632
N_EXPERTS = 8
TOP_K = 2
N_DENSE = (DEPTH + 1) // 2
N_MOE = DEPTH // 2

kernel_name = "hybrid_dit_conv_diffattn_swa_moe"


def rmsnorm(x, g):
    xf = x.astype(jnp.float32)
    y = xf * lax.rsqrt(jnp.mean(xf * xf, axis=-1, keepdims=True) + NORM_EPS)
    return (y * g.astype(jnp.float32)).astype(x.dtype)


def layernorm(x, g, b):
    xf = x.astype(jnp.float32)
    mu = jnp.mean(xf, axis=-1, keepdims=True)
    var = jnp.mean(jnp.square(xf - mu), axis=-1, keepdims=True)
    y = (xf - mu) * lax.rsqrt(var + NORM_EPS)
    return (y * g.astype(jnp.float32) + b.astype(jnp.float32)).astype(x.dtype)


def modulate(h, shift, scale):
    return h * (1 + scale) + shift


def rope_tables(n_tokens):
    rows = n_tokens // GRID_W
    row = jnp.repeat(jnp.arange(rows), GRID_W).astype(jnp.float32)
    col = jnp.tile(jnp.arange(GRID_W), rows).astype(jnp.float32)
    axis_dim = HEAD_DIM // 2
    inv = ROPE_THETA ** (-jnp.arange(axis_dim // 2, dtype=jnp.float32) / (axis_dim // 2))
    ang_r = row[:, None] * inv[None, :]
    ang_c = col[:, None] * inv[None, :]
    ang = jnp.concatenate([ang_r, ang_r, ang_c, ang_c], axis=-1)
    return jnp.cos(ang), jnp.sin(ang)


def apply_rope(x, cos, sin):
    shp = (cos.shape[0],) + (1,) * (x.ndim - 3) + (HEAD_DIM,)
    cos, sin = cos.reshape(shp), sin.reshape(shp)
    xr = x.reshape(x.shape[:-1] + (2, 2, HEAD_DIM // 4))
    rot = jnp.stack([-xr[..., 1, :], xr[..., 0, :]], axis=-2).reshape(x.shape)
    return (x * cos + rot * sin).astype(x.dtype)


def split_q(u):
    B, L = u.shape[:2]
    bq = u[..., :IN_BQ].reshape(B, L, DIFF_HEADS, 2, HEAD_DIM)
    cq = u[..., IN_BQ:].reshape(B, L, SWA_KV_HEADS, SWA_GROUP, HEAD_DIM)
    return bq, cq


def split_kv(u):
    B, L = u.shape[:2]
    o1 = IN_BK
    o2 = o1 + IN_BV
    o3 = o2 + IN_CK
    bk = u[..., :o1].reshape(B, L, DIFF_HEADS, 2, HEAD_DIM)
    bv = u[..., o1:o2].reshape(B, L, DIFF_HEADS, DIFF_V_DIM)
    ck = u[..., o2:o3].reshape(B, L, SWA_KV_HEADS, HEAD_DIM)
    cv = u[..., o3:].reshape(B, L, SWA_KV_HEADS, HEAD_DIM)
    return bk, bv, ck, cv


def conv_module(u, w_dw, b_dw, ln_g, ln_b):
    a, g = jnp.split(u, 2, axis=-1)
    y = a * jax.nn.sigmoid(g)
    y = lax.conv_general_dilated(y, w_dw, window_strides=(1,), padding=((CONV_K // 2, CONV_K // 2),),
                                 dimension_numbers=("NWC", "WIO", "NWC"), feature_group_count=CONV_CH)
    y = layernorm(y + b_dw, ln_g, ln_b)
    return jax.nn.silu(y)


def diff_attend(q, k, v, lam):
    s = jnp.einsum("bqhcd,bkhcd->bhcqk", q, k).astype(jnp.float32) * (HEAD_DIM ** -0.5)
    p = jax.nn.softmax(s, axis=-1)
    a = p[:, :, 0] - lam * p[:, :, 1]
    return jnp.einsum("bhqk,bkhe->bqhe", a.astype(v.dtype), v)


def diff_attention_latent(q, k_all, v_all, lam):
    B, N = q.shape[:2]
    nb = N // BLOCK
    qb = jnp.moveaxis(q.reshape((B, nb, BLOCK) + q.shape[2:]), 1, 0)
    out = lax.map(lambda qblk: diff_attend(qblk, k_all, v_all, lam), qb)
    return jnp.moveaxis(out, 0, 1).reshape(B, N, DIFF_HEADS, DIFF_V_DIM)


def diff_head_out(o, g_subln, lam_init):
    B, L = o.shape[:2]
    return (rmsnorm(o, g_subln) * (1.0 - lam_init)).reshape(B, L, DIFF_HEADS * DIFF_V_DIM)


def swa_latent(q, k, v, kc, vc, sink):
    B, N = q.shape[:2]
    nb = N // BLOCK
    qb = q.reshape(B, nb, BLOCK, SWA_KV_HEADS, SWA_GROUP, HEAD_DIM)

    def band(t):
        tb = t.reshape(B, nb, BLOCK, SWA_KV_HEADS, HEAD_DIM)
        tp = jnp.pad(tb, ((0, 0), (1, 1), (0, 0), (0, 0), (0, 0)))
        return jnp.concatenate([tp[:, :-2], tp[:, 1:-1], tp[:, 2:]], axis=2)

    kb, vb = band(k), band(v)
    scale = HEAD_DIM ** -0.5
    s_loc = jnp.einsum("bnqhgd,bnjhd->bnhgqj", qb, kb).astype(jnp.float32) * scale
    blk = jnp.arange(nb)[:, None, None]
    qpos = blk * BLOCK + jnp.arange(BLOCK)[None, :, None]
    kpos = (blk - 1) * BLOCK + jnp.arange(3 * BLOCK)[None, None, :]
    valid = (jnp.abs(qpos - kpos) <= WINDOW) & (kpos >= 0) & (kpos < N)
    s_loc = jnp.where(valid[None, :, None, None], s_loc, NEG_INF)
    s_ctx = jnp.einsum("bnqhgd,bjhd->bnhgqj", qb, kc).astype(jnp.float32) * scale
    sink_col = jnp.broadcast_to(sink.astype(jnp.float32).reshape(SWA_KV_HEADS, SWA_GROUP, 1, 1),
                                s_loc.shape[:-1] + (1,))
    p = jax.nn.softmax(jnp.concatenate([s_ctx, s_loc, sink_col], axis=-1), axis=-1)
    lc = kc.shape[1]
    pc = p[..., :lc].astype(vc.dtype)
    pl = p[..., lc:lc + 3 * BLOCK].astype(vb.dtype)
    out = jnp.einsum("bnhgqj,bjhd->bnqhgd", pc, vc) + jnp.einsum("bnhgqj,bnjhd->bnqhgd", pl, vb)
    return out.reshape(B, N, SWA_Q_HEADS * HEAD_DIM)


def swa_context(q, kc, vc, sink):
    B, L = q.shape[:2]
    s = jnp.einsum("bqhgd,bjhd->bhgqj", q, kc).astype(jnp.float32) * (HEAD_DIM ** -0.5)
    sink_col = jnp.broadcast_to(sink.astype(jnp.float32).reshape(SWA_KV_HEADS, SWA_GROUP, 1, 1),
                                s.shape[:-1] + (1,))
    p = jax.nn.softmax(jnp.concatenate([s, sink_col], axis=-1), axis=-1)[..., :-1]
    out = jnp.einsum("bhgqj,bjhd->bqhgd", p.astype(vc.dtype), vc)
    return out.reshape(B, L, SWA_Q_HEADS * HEAD_DIM)


def swiglu(h, w_gate, w_up, w_down):
    return (jax.nn.silu(h @ w_gate) * (h @ w_up)) @ w_down


def moe_swiglu(h, w_router, w_gate, w_up, w_down):
    logits = (h @ w_router).astype(jnp.float32)
    top_v, top_i = lax.top_k(logits, TOP_K)
    top_w = jax.nn.softmax(top_v, axis=-1)
    gates = jnp.sum(jax.nn.one_hot(top_i, N_EXPERTS, dtype=jnp.float32) * top_w[..., None], axis=-2).astype(h.dtype)
    out = jnp.zeros_like(h)
    for e in range(N_EXPERTS):
        out = out + gates[..., e:e + 1] * swiglu(h, w_gate[e], w_up[e], w_down[e])
    return out


def trunk_layer(xl, xc, c, c_ctx, w_ada, b_ada, g_n1, g_n2, w_in, w_dw, b_dw, ln_g, ln_b,
                lam_q1, lam_k1, lam_q2, lam_k2, g_subln, sink, w_out, ffn, lam_init, update_ctx):
    sh1_l, sc1_l, gt1_l, sh2_l, sc2_l, gt2_l = jnp.split((jax.nn.silu(c) @ w_ada + b_ada)[:, None, :], 6, axis=-1)
    sh1_c, sc1_c, gt1_c, sh2_c, sc2_c, gt2_c = jnp.split((jax.nn.silu(c_ctx) @ w_ada + b_ada)[None, :], 6, axis=-1)
    f32 = jnp.float32
    lam = (jnp.exp(jnp.sum(lam_q1.astype(f32) * lam_k1.astype(f32)))
           - jnp.exp(jnp.sum(lam_q2.astype(f32) * lam_k2.astype(f32))) + lam_init)

    hc = modulate(rmsnorm(xc, g_n1), sh1_c, sc1_c)
    if update_ctx:
        uc = hc @ w_in
        ukv_c = uc[..., OFF_KV:]
    else:
        ukv_c = hc @ w_in[:, OFF_KV:]
    bk_c, bv_c, ck_c, cv_c = split_kv(ukv_c)

    hl = modulate(rmsnorm(xl, g_n1), sh1_l, sc1_l)
    ul = hl @ w_in
    cos, sin = rope_tables(xl.shape[1])
    a_l = conv_module(ul[..., :IN_A], w_dw, b_dw, ln_g, ln_b)
    bq_l, cq_l = split_q(ul[..., OFF_Q:OFF_KV])
    bk_l, bv_l, ck_l, cv_l = split_kv(ul[..., OFF_KV:])
    bq_l, bk_l = apply_rope(bq_l, cos, sin), apply_rope(bk_l, cos, sin)
    cq_l, ck_l = apply_rope(cq_l, cos, sin), apply_rope(ck_l, cos, sin)
    o_b = diff_attention_latent(bq_l, jnp.concatenate([bk_c, bk_l], axis=1),
                                jnp.concatenate([bv_c, bv_l], axis=1), lam)
    o_c = swa_latent(cq_l, ck_l, cv_l, ck_c, cv_c, sink)
    mix_l = jnp.concatenate([a_l, diff_head_out(o_b, g_subln, lam_init), o_c], axis=-1) @ w_out
    xl = xl + gt1_l * mix_l
    hl2 = modulate(rmsnorm(xl, g_n2), sh2_l, sc2_l)

    if update_ctx:
        a_c = conv_module(uc[..., :IN_A], w_dw, b_dw, ln_g, ln_b)
        bq_c, cq_c = split_q(uc[..., OFF_Q:OFF_KV])
        o_bc = diff_attend(bq_c, bk_c, bv_c, lam)
        o_cc = swa_context(cq_c, ck_c, cv_c, sink)
        mix_c = jnp.concatenate([a_c, diff_head_out(o_bc, g_subln, lam_init), o_cc], axis=-1) @ w_out
        xc = xc + gt1_c * mix_c
        hc2 = modulate(rmsnorm(xc, g_n2), sh2_c, sc2_c)
        lc = xc.shape[1]
        f = ffn(jnp.concatenate([hc2, hl2], axis=1))
        xc = xc + gt2_c * f[:, :lc]
        xl = xl + gt2_l * f[:, lc:]
        return xl, xc
    xl = xl + gt2_l * ffn(hl2)
    return xl, None


def setup_inputs(seed: int = 0) -> dict:
    key = jax.random.key(seed)
    ks = jax.random.split(key, 32)
    D = D_MODEL

    def nrm(k, shape, scale):
        return jax.random.normal(k, shape, jnp.float32) * scale

    return {
        "x": nrm(ks[0], (BATCH, SEQ, D), 1.0),
        "c": nrm(ks[1], (BATCH, D), 1.0),
        "ctx": nrm(ks[2], (BATCH, CTX_LEN, D), 1.0),
        "c_ctx": nrm(ks[3], (D,), 1.0),
        "w_ada": nrm(ks[4], (DEPTH, D, 6 * D), 0.5 * D ** -0.5),
        "b_ada": nrm(ks[5], (DEPTH, 6 * D), 0.02),
        "g_norm1": 1.0 + nrm(ks[6], (DEPTH, D), 0.1),
        "g_norm2": 1.0 + nrm(ks[7], (DEPTH, D), 0.1),
        "w_in": nrm(ks[8], (DEPTH, D, D_IN), D ** -0.5),
        "w_dw": nrm(ks[9], (DEPTH, CONV_K, 1, CONV_CH), CONV_K ** -0.5),
        "b_dw": nrm(ks[10], (DEPTH, CONV_CH), 0.02),
        "ln_g": 1.0 + nrm(ks[11], (DEPTH, CONV_CH), 0.1),
        "ln_b": nrm(ks[12], (DEPTH, CONV_CH), 0.02),
        "lam_q1": nrm(ks[13], (DEPTH, HEAD_DIM), 0.1),
        "lam_k1": nrm(ks[14], (DEPTH, HEAD_DIM), 0.1),
        "lam_q2": nrm(ks[15], (DEPTH, HEAD_DIM), 0.1),
        "lam_k2": nrm(ks[16], (DEPTH, HEAD_DIM), 0.1),
        "g_subln": 1.0 + nrm(ks[17], (DEPTH, DIFF_V_DIM), 0.1),
        "sink": nrm(ks[18], (DEPTH, SWA_Q_HEADS), 0.5),
        "w_out": nrm(ks[19], (DEPTH, D_MIX, D), D_MIX ** -0.5),
        "w_ff_gate": nrm(ks[20], (N_DENSE, D, D_FF), D ** -0.5),
        "w_ff_up": nrm(ks[21], (N_DENSE, D, D_FF), D ** -0.5),
        "w_ff_down": nrm(ks[22], (N_DENSE, D_FF, D), D_FF ** -0.5),
        "w_router": nrm(ks[23], (N_MOE, D, N_EXPERTS), D ** -0.5),
        "w_ex_gate": nrm(ks[24], (N_MOE, N_EXPERTS, D, D_FF), D ** -0.5),
        "w_ex_up": nrm(ks[25], (N_MOE, N_EXPERTS, D, D_FF), D ** -0.5),
        "w_ex_down": nrm(ks[26], (N_MOE, N_EXPERTS, D_FF, D), D_FF ** -0.5),
        "g_final": 1.0 + nrm(ks[27], (D,), 0.1),
    }


def reference(x, c, ctx, c_ctx, w_ada, b_ada, g_norm1, g_norm2, w_in, w_dw, b_dw, ln_g, ln_b,
              lam_q1, lam_k1, lam_q2, lam_k2, g_subln, sink, w_out, w_ff_gate, w_ff_up, w_ff_down,
              w_router, w_ex_gate, w_ex_up, w_ex_down, g_final):
    xl, xc = x, ctx
    for l in range(DEPTH):
        i = l // 2
        if l % 2 == 0:
            ffn = functools.partial(swiglu, w_gate=w_ff_gate[i], w_up=w_ff_up[i], w_down=w_ff_down[i])
        else:
            ffn = functools.partial(moe_swiglu, w_router=w_router[i], w_gate=w_ex_gate[i],
                                    w_up=w_ex_up[i], w_down=w_ex_down[i])
        lam_init = 0.8 - 0.6 * math.exp(-0.3 * l)
        xl, xc = trunk_layer(xl, xc, c, c_ctx, w_ada[l], b_ada[l], g_norm1[l], g_norm2[l], w_in[l],
                             w_dw[l], b_dw[l], ln_g[l], ln_b[l], lam_q1[l], lam_k1[l], lam_q2[l], lam_k2[l],
                             g_subln[l], sink[l], w_out[l], ffn, lam_init, l < DEPTH - 1)
    return rmsnorm(xl, g_final)
```

```python
import functools
import math

import jax
import jax.numpy as jnp
from jax import lax
from jax.experimental import pallas as pl
from jax.experimental.pallas import tpu as pltpu

F32 = jnp.float32
BF16 = jnp.bfloat16

HEAD_DIM = 64
GRID_W = 64
ROPE_THETA = 10000.0
NORM_EPS = 1e-6
MASK_VALUE = -1e30
CONV_K = 31
CONV_HALO = 16
WINDOW = 128
TOP_K = 2
LANES = 128
VMEM_LIMIT = 56 * 1024 * 1024


def _cparams(sem):
    return pltpu.CompilerParams(dimension_semantics=sem, vmem_limit_bytes=VMEM_LIMIT)


def _rms_mod(x, g, shift, scale):
    y = x * lax.rsqrt(jnp.mean(x * x, axis=-1, keepdims=True) + NORM_EPS)
    return (y * g) * (1.0 + scale) + shift


def _sigmoid(x):
    return 1.0 / (1.0 + jnp.exp(-x))


def _ada_kernel(c_ref, w_ref, b_ref, o_ref):
    c = c_ref[...]
    s = (c * _sigmoid(c)).astype(BF16)
    o_ref[0] = jnp.dot(s, w_ref[0].astype(BF16), preferred_element_type=F32) + b_ref[0]


def _ada(cvec, w_ada, b_ada):
    depth, d, n6 = w_ada.shape
    rows = cvec.shape[0]
    tn = 1024
    return pl.pallas_call(
        _ada_kernel,
        out_shape=jax.ShapeDtypeStruct((depth, rows, n6), F32),
        grid=(depth, n6 // tn),
        in_specs=[pl.BlockSpec((rows, d), lambda l, j: (0, 0)),
                  pl.BlockSpec((1, d, tn), lambda l, j: (l, 0, j)),
                  pl.BlockSpec((1, 1, tn), lambda l, j: (l, 0, j))],
        out_specs=pl.BlockSpec((1, rows, tn), lambda l, j: (l, 0, j)),
        compiler_params=_cparams(("arbitrary", "arbitrary")),
        name="ada",
    )(cvec, w_ada, b_ada.reshape(depth, 1, n6))


def _in_kernel(x_ref, g_ref, mod_ref, w_ref, cos_ref, sa_ref, sb_ref,
               y_ref, qb_ref, qc_ref, kb_ref, vb_ref, kc_ref, vc_ref, *, sizes):
    conv_ch, n_bq, n_cq, n_bk, n_bv, n_ck, n_cv = sizes
    x = x_ref[0]
    hb = _rms_mod(x, g_ref[...], mod_ref[0, 0:1, :], mod_ref[0, 1:2, :]).astype(BF16)

    def mm(lo, width):
        return jnp.dot(hb, w_ref[:, lo:lo + width], preferred_element_type=F32)

    cos, sa, sb = cos_ref[...], sa_ref[...], sb_ref[...]

    def rope_store(u, out_ref, scale):
        for gi in range(u.shape[1] // LANES):
            ug = u[:, gi * LANES:(gi + 1) * LANES]
            r = ug * cos + pltpu.roll(ug, LANES - 16, 1) * sa + pltpu.roll(ug, 16, 1) * sb
            out_ref[0, :, gi * LANES:(gi + 1) * LANES] = (r * scale).astype(out_ref.dtype)

    off = 0
    u = mm(off, 2 * conv_ch)
    y_ref[0] = u[:, :conv_ch] * _sigmoid(u[:, conv_ch:])
    off += 2 * conv_ch
    qscale = HEAD_DIM ** -0.5
    rope_store(mm(off, n_bq), qb_ref, qscale)
    off += n_bq
    rope_store(mm(off, n_cq), qc_ref, qscale)
    off += n_cq
    rope_store(mm(off, n_bk), kb_ref, 1.0)
    off += n_bk
    vb_ref[0] = mm(off, n_bv).astype(BF16)
    off += n_bv
    rope_store(mm(off, n_ck), kc_ref, 1.0)
    off += n_ck
    vc_ref[0] = mm(off, n_cv).astype(BF16)


def _in_proj(x, g, mod, w_bf16, tabs, sizes, lc, tm):
    b, s, d = x.shape
    conv_ch, n_bq, n_cq, n_bk, n_bv, n_ck, n_cv = sizes
    d_in = w_bf16.shape[1]
    nctx = lc // tm
    n_mod = mod.shape[0]

    def row(bi, j):
        return (bi, j, 0)

    def modrow(bi, j):
        return (jnp.where(j < nctx, n_mod - 1, bi), 0, 0)

    widths = (conv_ch, n_bq, n_cq, n_bk, n_bv, n_ck, n_cv)
    dts = (F32, BF16, BF16, BF16, BF16, BF16, BF16)
    out_shape = tuple(jax.ShapeDtypeStruct((b, s, w), dt) for w, dt in zip(widths, dts))
    out_specs = tuple(pl.BlockSpec((1, tm, w), row) for w in widths)
    tab_spec = pl.BlockSpec((tm, LANES), lambda bi, j: (j, 0))
    return pl.pallas_call(
        functools.partial(_in_kernel, sizes=sizes),
        out_shape=out_shape,
        grid=(b, s // tm),
        in_specs=[pl.BlockSpec((1, tm, d), row),
                  pl.BlockSpec((1, d), lambda bi, j: (0, 0)),
                  pl.BlockSpec((1, 6, d), modrow),
                  pl.BlockSpec((d, d_in), lambda bi, j: (0, 0)),
                  tab_spec, tab_spec, tab_spec],
        out_specs=out_specs,
        compiler_params=_cparams(("arbitrary", "arbitrary")),
        name="in_proj",
    )(x, g.reshape(1, d), mod, w_bf16, *tabs)


def _conv_kernel(yp_ref, yc_ref, yn_ref, w_ref, b_ref, g_ref, bb_ref, o_ref, win_ref, *, tm, seg_tile, sub):
    j = pl.program_id(1)
    nj = pl.num_programs(1)
    has_prev = jnp.logical_and(j > 0, j != seg_tile)
    has_next = jnp.logical_and(j + 1 < nj, j + 1 != seg_tile)
    h = CONV_HALO
    win_ref[0:h, :] = jnp.where(has_prev, yp_ref[0], 0.0)
    win_ref[h:h + tm, :] = yc_ref[0]
    win_ref[h + tm:h + tm + h, :] = jnp.where(has_next, yn_ref[0], 0.0)
    base = h - CONV_K // 2
    for r0 in range(0, tm, sub):
        acc = jnp.zeros((sub, win_ref.shape[1]), F32)
        for t in range(CONV_K):
            acc = acc + w_ref[t:t + 1, :] * win_ref[pl.ds(base + r0 + t, sub), :]
        y = acc + b_ref[...]
        mu = jnp.mean(y, axis=-1, keepdims=True)
        yc = y - mu
        var = jnp.mean(yc * yc, axis=-1, keepdims=True)
        z = yc * lax.rsqrt(var + NORM_EPS) * g_ref[...] + bb_ref[...]
        o_ref[0, r0:r0 + sub, :] = (z * _sigmoid(z)).astype(o_ref.dtype)


def _conv_module(y, w_dw, b_dw, ln_g, ln_b, lc, tm):
    b, s, ch = y.shape
    h = CONV_HALO
    per = tm // h
    nblk = s // h

    def prev(bi, j):
        return (bi, jnp.maximum(j * per - 1, 0), 0)

    def nxt(bi, j):
        return (bi, jnp.minimum((j + 1) * per, nblk - 1), 0)

    vec = pl.BlockSpec((1, ch), lambda bi, j: (0, 0))
    return pl.pallas_call(
        functools.partial(_conv_kernel, tm=tm, seg_tile=lc // tm, sub=32),
        out_shape=jax.ShapeDtypeStruct((b, s, ch), BF16),
        grid=(b, s // tm),
        in_specs=[pl.BlockSpec((1, h, ch), prev),
                  pl.BlockSpec((1, tm, ch), lambda bi, j: (bi, j, 0)),
                  pl.BlockSpec((1, h, ch), nxt),
                  pl.BlockSpec((CONV_K, ch), lambda bi, j: (0, 0)),
                  vec, vec, vec],
        out_specs=pl.BlockSpec((1, tm, ch), lambda bi, j: (bi, j, 0)),
        scratch_shapes=[pltpu.VMEM((tm + 2 * h, ch), F32)],
        compiler_params=_cparams(("arbitrary", "arbitrary")),
        name="conv_module",
    )(y, y, y, w_dw.reshape(CONV_K, ch), b_dw.reshape(1, ch), ln_g.reshape(1, ch), ln_b.reshape(1, ch))


def _diff_kernel(lamp_ref, q_ref, k_ref, v_ref, g_ref, o_ref, m_scr, l_scr, acc_scr,
                 *, tq, tk, q_off, n_ctx_qblk, ctx_chunks, all_chunks, lam_init):
    i = pl.program_id(2) + q_off
    q = q_ref[0]
    lane = lax.broadcasted_iota(jnp.int32, q.shape, 1)
    zero = jnp.zeros_like(q)
    qq = jnp.concatenate([jnp.where(lane < HEAD_DIM, q, zero), jnp.where(lane >= HEAD_DIM, q, zero)], axis=0)
    m_scr[...] = jnp.full(m_scr.shape, MASK_VALUE, F32)
    l_scr[...] = jnp.zeros(l_scr.shape, F32)
    acc_scr[...] = jnp.zeros(acc_scr.shape, F32)

    def body(c, carry):
        start = pl.multiple_of(c * tk, tk)
        k = k_ref[0, pl.ds(start, tk), :]
        v = v_ref[0, pl.ds(start, tk), :]
        s = lax.dot_general(qq, k, (((1,), (1,)), ((), ())), preferred_element_type=F32)
        m_prev = m_scr[...]
        m_new = jnp.maximum(m_prev, jnp.max(s, axis=1, keepdims=True))
        alpha = jnp.exp(m_prev - m_new)
        p = jnp.exp(s - m_new)
        l_scr[...] = alpha * l_scr[...] + jnp.sum(p, axis=1, keepdims=True)
        acc_scr[...] = alpha * acc_scr[...] + jnp.dot(p.astype(BF16), v, preferred_element_type=F32)
        m_scr[...] = m_new
        return carry

    n_chunks = jnp.where(i < n_ctx_qblk, ctx_chunks, all_chunks)
    lax.fori_loop(0, n_chunks, body, 0)

    o = acc_scr[...] / l_scr[...]
    lp = lamp_ref[...]
    lam = (jnp.exp(jnp.sum(lp[0:1] * lp[1:2], axis=1, keepdims=True))
           - jnp.exp(jnp.sum(lp[2:3] * lp[3:4], axis=1, keepdims=True)) + lam_init)
    od = o[:tq] - lam * o[tq:]
    y = od * lax.rsqrt(jnp.mean(od * od, axis=-1, keepdims=True) + NORM_EPS)
    o_ref[0] = (y * g_ref[...] * (1.0 - lam_init)).astype(o_ref.dtype)


def _diff_attention(qb, kb, vb, lam_params, g_subln, lam_init, lc, with_ctx, tq, tk):
    b, s, w = qb.shape
    heads = w // (2 * HEAD_DIM)
    q_off = 0 if with_ctx else lc // tq
    nq = s // tq - q_off
    kern = functools.partial(_diff_kernel, tq=tq, tk=tk, q_off=q_off, n_ctx_qblk=lc // tq,
                             ctx_chunks=lc // tk, all_chunks=s // tk, lam_init=lam_init)
    return pl.pallas_call(
        kern,
        out_shape=jax.ShapeDtypeStruct((b, nq * tq, w), BF16),
        grid=(b, heads, nq),
        in_specs=[pl.BlockSpec((4, HEAD_DIM), lambda bi, h, i: (0, 0)),
                  pl.BlockSpec((1, tq, 2 * HEAD_DIM), lambda bi, h, i: (bi, i + q_off, h)),
                  pl.BlockSpec((1, s, 2 * HEAD_DIM), lambda bi, h, i: (bi, 0, h)),
                  pl.BlockSpec((1, s, 2 * HEAD_DIM), lambda bi, h, i: (bi, 0, h)),
                  pl.BlockSpec((1, 2 * HEAD_DIM), lambda bi, h, i: (0, 0))],
        out_specs=pl.BlockSpec((1, tq, 2 * HEAD_DIM), lambda bi, h, i: (bi, i, h)),
        scratch_shapes=[pltpu.VMEM((2 * tq, 1), F32), pltpu.VMEM((2 * tq, 1), F32),
                        pltpu.VMEM((2 * tq, 2 * HEAD_DIM), F32)],
        compiler_params=_cparams(("arbitrary", "arbitrary", "arbitrary")),
        name="diff_attention",
    )(lam_params, qb, kb, vb, g_subln.reshape(1, 2 * HEAD_DIM))


def _swa_kernel(sink_ref, q_ref, kctx_ref, vctx_ref, kp_ref, kc_ref, kn_ref, vp_ref, vc_ref, vn_ref, o_ref,
                *, tq, q_off, n_ctx_qblk, n_lat, kv_heads, group):
    i = pl.program_id(1) + q_off
    is_lat = i >= n_ctx_qblk
    q0 = (i - n_ctx_qblk) * tq
    qpos = q0 + lax.broadcasted_iota(jnp.int32, (tq, 3 * tq), 0)
    kpos = q0 - tq + lax.broadcasted_iota(jnp.int32, (tq, 3 * tq), 1)
    valid = jnp.logical_and(jnp.abs(qpos - kpos) <= WINDOW, jnp.logical_and(kpos >= 0, kpos < n_lat))
    valid = jnp.logical_and(valid, is_lat)
    valid = jnp.concatenate([valid] * group, axis=0)
    q = q_ref[0]
    kloc = jnp.concatenate([kp_ref[0], kc_ref[0], kn_ref[0]], axis=0)
    vloc = jnp.concatenate([vp_ref[0], vc_ref[0], vn_ref[0]], axis=0)
    kctx = kctx_ref[0]
    vctx = vctx_ref[0]
    nt = (((1,), (1,)), ((), ()))
    for h in range(kv_heads):
        hs = slice(h * HEAD_DIM, (h + 1) * HEAD_DIM)
        qg = jnp.concatenate(
            [q[:, (h * group + gi) * HEAD_DIM:(h * group + gi + 1) * HEAD_DIM] for gi in range(group)], axis=0)
        sink = jnp.concatenate(
            [jnp.full((tq, 1), sink_ref[h * group + gi], F32) for gi in range(group)], axis=0)
        s_ctx = lax.dot_general(qg, kctx[:, hs], nt, preferred_element_type=F32)
        s_loc = lax.dot_general(qg, kloc[:, hs], nt, preferred_element_type=F32)
        s_loc = jnp.where(valid, s_loc, MASK_VALUE)
        m = jnp.maximum(jnp.maximum(jnp.max(s_ctx, axis=1, keepdims=True), jnp.max(s_loc, axis=1, keepdims=True)), sink)
        e_ctx = jnp.exp(s_ctx - m)
        e_loc = jnp.exp(s_loc - m)
        denom = jnp.sum(e_ctx, axis=1, keepdims=True) + jnp.sum(e_loc, axis=1, keepdims=True) + jnp.exp(sink - m)
        o = (jnp.dot(e_ctx.astype(BF16), vctx[:, hs], preferred_element_type=F32)
             + jnp.dot(e_loc.astype(BF16), vloc[:, hs], preferred_element_type=F32)) / denom
        for gi in range(group):
            c0 = (h * group + gi) * HEAD_DIM
            o_ref[0, :, c0:c0 + HEAD_DIM] = o[gi * tq:(gi + 1) * tq].astype(o_ref.dtype)


def _swa(qc, kc, vc, sink, lc, with_ctx, tq):
    b, s, w = qc.shape
    kvw = kc.shape[2]
    kv_heads = kvw // HEAD_DIM
    group = w // kvw
    q_off = 0 if with_ctx else lc // tq
    nq = s // tq - q_off
    nblk = s // tq

    def cur(bi, i):
        return (bi, i + q_off, 0)

    def prev(bi, i):
        return (bi, jnp.maximum(i + q_off - 1, 0), 0)

    def nxt(bi, i):
        return (bi, jnp.minimum(i + q_off + 1, nblk - 1), 0)

    def ctx(bi, i):
        return (bi, 0, 0)

    kern = functools.partial(_swa_kernel, tq=tq, q_off=q_off, n_ctx_qblk=lc // tq, n_lat=s - lc,
                             kv_heads=kv_heads, group=group)
    band = lambda f: pl.BlockSpec((1, tq, kvw), f)
    return pl.pallas_call(
        kern,
        out_shape=jax.ShapeDtypeStruct((b, nq * tq, w), BF16),
        grid=(b, nq),
        in_specs=[pl.BlockSpec(memory_space=pltpu.SMEM),
                  pl.BlockSpec((1, tq, w), cur),
                  pl.BlockSpec((1, lc, kvw), ctx), pl.BlockSpec((1, lc, kvw), ctx),
                  band(prev), band(cur), band(nxt), band(prev), band(cur), band(nxt)],
        out_specs=pl.BlockSpec((1, tq, w), lambda bi, i: (bi, i, 0)),
        compiler_params=_cparams(("arbitrary", "arbitrary")),
        name="swa",
    )(sink, qc, kc, vc, kc, kc, kc, vc, vc, vc)


def _out_kernel(a_ref, ob_ref, oc_ref, x_ref, w_ref, mod_ref, g_ref, xo_ref, h_ref):
    na, nb = a_ref.shape[2], ob_ref.shape[2]
    mix = jnp.dot(a_ref[0], w_ref[0:na, :], preferred_element_type=F32)
    mix = mix + jnp.dot(ob_ref[0], w_ref[na:na + nb, :], preferred_element_type=F32)
    mix = mix + jnp.dot(oc_ref[0], w_ref[na + nb:, :], preferred_element_type=F32)
    xn = x_ref[0] + mod_ref[0, 2:3, :] * mix
    xo_ref[0] = xn
    h_ref[0] = _rms_mod(xn, g_ref[...], mod_ref[0, 3:4, :], mod_ref[0, 4:5, :]).astype(h_ref.dtype)


def _out_proj(a, ob, oc, x, w_bf16, mod, g2, lc, with_ctx, tm, h_dtype):
    b, s, d = x.shape
    nctx = lc // tm
    q_off = 0 if with_ctx else nctx
    nt = s // tm - q_off
    n_mod = mod.shape[0]
    out_rows = s - q_off * tm

    def row(bi, j):
        return (bi, j + q_off, 0)

    def modrow(bi, j):
        return (jnp.where(j + q_off < nctx, n_mod - 1, bi), 0, 0)

    def own(bi, j):
        return (bi, j, 0)

    assert ob.shape[1] == out_rows and oc.shape[1] == out_rows
    return pl.pallas_call(
        _out_kernel,
        out_shape=(jax.ShapeDtypeStruct((b, out_rows, d), F32), jax.ShapeDtypeStruct((b, out_rows, d), h_dtype)),
        grid=(b, nt),
        in_specs=[pl.BlockSpec((1, tm, a.shape[2]), row), pl.BlockSpec((1, tm, ob.shape[2]), own),
                  pl.BlockSpec((1, tm, oc.shape[2]), own), pl.BlockSpec((1, tm, d), row),
                  pl.BlockSpec(w_bf16.shape, lambda bi, j: (0, 0)),
                  pl.BlockSpec((1, 6, d), modrow),
                  pl.BlockSpec((1, d), lambda bi, j: (0, 0))],
        out_specs=(pl.BlockSpec((1, tm, d), own), pl.BlockSpec((1, tm, d), own)),
        compiler_params=_cparams(("arbitrary", "arbitrary")),
        name="out_proj",
    )(a, ob, oc, x, w_bf16, mod, g2.reshape(1, d))


def _ffn_kernel(h_ref, x_ref, wg_ref, wu_ref, wd_ref, modb_ref, modc_ref, o_ref, acc_ref, *, tm, lc):
    f = pl.program_id(2)

    @pl.when(f == 0)
    def _():
        acc_ref[...] = jnp.zeros(acc_ref.shape, F32)

    h = h_ref[0]
    g = jnp.dot(h, wg_ref[...], preferred_element_type=F32)
    u = jnp.dot(h, wu_ref[...], preferred_element_type=F32)
    act = (g * _sigmoid(g) * u).astype(BF16)
    acc_ref[...] += jnp.dot(act, wd_ref[...], preferred_element_type=F32)

    @pl.when(f == pl.num_programs(2) - 1)
    def _():
        rows = pl.program_id(1) * tm + lax.broadcasted_iota(jnp.int32, (tm, 1), 0)
        gate = jnp.where(rows < lc, modc_ref[0, 5:6, :], modb_ref[0, 5:6, :])
        o_ref[0] = x_ref[0] + gate * acc_ref[...]


def _ffn_dense(h, x, wg, wu, wd, mod, lc, tm, tf):
    b, s, d = x.shape
    dff = wg.shape[1]
    n_mod = mod.shape[0]
    row = lambda bi, j, f: (bi, j, 0)
    return pl.pallas_call(
        functools.partial(_ffn_kernel, tm=tm, lc=lc),
        out_shape=jax.ShapeDtypeStruct((b, s, d), F32),
        grid=(b, s // tm, dff // tf),
        in_specs=[pl.BlockSpec((1, tm, d), row), pl.BlockSpec((1, tm, d), row),
                  pl.BlockSpec((d, tf), lambda bi, j, f: (0, f)),
                  pl.BlockSpec((d, tf), lambda bi, j, f: (0, f)),
                  pl.BlockSpec((tf, d), lambda bi, j, f: (f, 0)),
                  pl.BlockSpec((1, 6, d), lambda bi, j, f: (bi, 0, 0)),
                  pl.BlockSpec((1, 6, d), lambda bi, j, f: (n_mod - 1, 0, 0))],
        out_specs=pl.BlockSpec((1, tm, d), row),
        scratch_shapes=[pltpu.VMEM((tm, d), F32)],
        compiler_params=_cparams(("arbitrary", "arbitrary", "arbitrary")),
        name="ffn_dense",
    )(h, x, wg, wu, wd, mod, mod)


def _router_kernel(h_ref, w_ref, idx_ref, wt_ref):
    logits = jnp.dot(h_ref[...], w_ref[...], preferred_element_type=F32, precision=lax.Precision.HIGHEST)
    n_e = logits.shape[1]
    lane = lax.broadcasted_iota(jnp.int32, logits.shape, 1)
    m1 = jnp.max(logits, axis=1, keepdims=True)
    i1 = jnp.min(jnp.where(logits == m1, lane, n_e), axis=1, keepdims=True)
    rest = jnp.where(lane == i1, -jnp.inf, logits)
    m2 = jnp.max(rest, axis=1, keepdims=True)
    i2 = jnp.min(jnp.where(rest == m2, lane, n_e), axis=1, keepdims=True)
    e2 = jnp.exp(m2 - m1)
    den = 1.0 + e2
    out_lane = lax.broadcasted_iota(jnp.int32, idx_ref.shape, 1)
    idx_ref[...] = jnp.where(out_lane == 0, i1, jnp.where(out_lane == 1, i2, 0))
    wt_ref[...] = jnp.where(out_lane == 0, 1.0 / den, jnp.where(out_lane == 1, e2 / den, 0.0))


def _router(h_flat, w_router, tm):
    t, d = h_flat.shape
    n_e = w_router.shape[1]
    return pl.pallas_call(
        _router_kernel,
        out_shape=(jax.ShapeDtypeStruct((t, LANES), jnp.int32), jax.ShapeDtypeStruct((t, LANES), F32)),
        grid=(t // tm,),
        in_specs=[pl.BlockSpec((tm, d), lambda i: (i, 0)), pl.BlockSpec((d, n_e), lambda i: (0, 0))],
        out_specs=(pl.BlockSpec((tm, LANES), lambda i: (i, 0)), pl.BlockSpec((tm, LANES), lambda i: (i, 0))),
        compiler_params=_cparams(("arbitrary",)),
        name="router",
    )(h_flat, w_router)


def _row_copy(src_hbm, row, dst, r, sem):
    return pltpu.make_async_copy(src_hbm.at[pl.ds(row, 1)], dst.at[pl.ds(r, 1)], sem)


def _gather_kernel(nact_ref, tok_ref, h_hbm, o_ref, buf, sem, *, tm):
    i = pl.program_id(0)

    @pl.when(i < nact_ref[0])
    def _():
        def issue(r, carry):
            _row_copy(h_hbm, tok_ref[0, 0, r], buf, r, sem).start()
            return carry

        lax.fori_loop(0, tm, issue, 0)

        def wait(r, carry):
            _row_copy(h_hbm, 0, buf, r, sem).wait()
            return carry

        lax.fori_loop(0, tm, wait, 0)
        o_ref[...] = buf[...].astype(o_ref.dtype)

    @pl.when(i >= nact_ref[0])
    def _():
        o_ref[...] = jnp.zeros(o_ref.shape, o_ref.dtype)


def _gather_rows(h_flat, tok, n_active, tm):
    n_tiles = tok.shape[0]
    d = h_flat.shape[1]
    grid_spec = pltpu.PrefetchScalarGridSpec(
        num_scalar_prefetch=1,
        grid=(n_tiles,),
        in_specs=[pl.BlockSpec((1, 1, tm), lambda i, na: (i, 0, 0), memory_space=pltpu.SMEM),
                  pl.BlockSpec(memory_space=pl.ANY)],
        out_specs=pl.BlockSpec((tm, d), lambda i, na: (i, 0)),
        scratch_shapes=[pltpu.VMEM((tm, d), h_flat.dtype), pltpu.SemaphoreType.DMA(())],
    )
    return pl.pallas_call(
        functools.partial(_gather_kernel, tm=tm),
        out_shape=jax.ShapeDtypeStruct((n_tiles * tm, d), BF16),
        grid_spec=grid_spec,
        compiler_params=_cparams(("arbitrary",)),
        name="moe_gather",
    )(n_active, tok, h_flat)


def _moe_ffn_kernel(te_ref, nact_ref, x_ref, wg_ref, wu_ref, wd_ref, o_ref, acc_ref):
    i = pl.program_id(0)
    f = pl.program_id(1)

    @pl.when(i < nact_ref[0])
    def _():
        @pl.when(f == 0)
        def _():
            acc_ref[...] = jnp.zeros(acc_ref.shape, F32)

        x = x_ref[...]
        g = jnp.dot(x, wg_ref[0], preferred_element_type=F32)
        u = jnp.dot(x, wu_ref[0], preferred_element_type=F32)
        act = (g * _sigmoid(g) * u).astype(BF16)
        acc_ref[...] += jnp.dot(act, wd_ref[0], preferred_element_type=F32)

        @pl.when(f == pl.num_programs(1) - 1)
        def _():
            o_ref[...] = acc_ref[...]

    @pl.when(jnp.logical_and(i >= nact_ref[0], f == 0))
    def _():
        o_ref[...] = jnp.zeros(o_ref.shape, o_ref.dtype)


def _moe_ffn(xs, wg, wu, wd, tile_expert, n_active, tm, tf):
    p, d = xs.shape
    dff = wg.shape[2]
    nf = dff // tf
    n_tiles = p // tm

    def tile(i, f, te, na):
        return (jnp.minimum(i, na[0] - 1), 0)

    def fcol(i, f, te, na):
        return jnp.where(i < na[0], f, nf - 1)

    grid_spec = pltpu.PrefetchScalarGridSpec(
        num_scalar_prefetch=2,
        grid=(n_tiles, nf),
        in_specs=[pl.BlockSpec((tm, d), tile),
                  pl.BlockSpec((1, d, tf), lambda i, f, te, na: (te[i], 0, fcol(i, f, te, na))),
                  pl.BlockSpec((1, d, tf), lambda i, f, te, na: (te[i], 0, fcol(i, f, te, na))),
                  pl.BlockSpec((1, tf, d), lambda i, f, te, na: (te[i], fcol(i, f, te, na), 0))],
        out_specs=pl.BlockSpec((tm, d), lambda i, f, te, na: (i, 0)),
        scratch_shapes=[pltpu.VMEM((tm, d), F32)],
    )
    return pl.pallas_call(
        _moe_ffn_kernel,
        out_shape=jax.ShapeDtypeStruct((p, d), F32),
        grid_spec=grid_spec,
        compiler_params=_cparams(("arbitrary", "arbitrary")),
        name="moe_ffn",
    )(tile_expert, n_active, xs, wg, wu, wd)


def _combine_kernel(p0_ref, p1_ref, y_hbm, x_ref, wt_ref, mod_ref, g_ref, o_ref, buf0, buf1, sem, *, tm):
    def issue(r, carry):
        _row_copy(y_hbm, p0_ref[0, 0, r], buf0, r, sem.at[0]).start()
        _row_copy(y_hbm, p1_ref[0, 0, r], buf1, r, sem.at[1]).start()
        return carry

    lax.fori_loop(0, tm, issue, 0)

    def wait(r, carry):
        _row_copy(y_hbm, 0, buf0, r, sem.at[0]).wait()
        _row_copy(y_hbm, 0, buf1, r, sem.at[1]).wait()
        return carry

    lax.fori_loop(0, tm, wait, 0)
    wt = wt_ref[...]
    f = wt[:, 0:1] * buf0[...] + wt[:, 1:2] * buf1[...]
    xn = x_ref[0] + mod_ref[0, 5:6, :] * f
    y = xn * lax.rsqrt(jnp.mean(xn * xn, axis=-1, keepdims=True) + NORM_EPS)
    o_ref[0] = y * g_ref[...]


def _combine(y, pos0, pos1, x, wts, mod, g_final, tm):
    b, n, d = x.shape
    nt = n // tm
    off = 0
    return pl.pallas_call(
        functools.partial(_combine_kernel, tm=tm),
        out_shape=jax.ShapeDtypeStruct((b, n, d), F32),
        grid=(b, nt),
        in_specs=[pl.BlockSpec((1, 1, tm), lambda bi, j: (bi * nt + j, 0, 0), memory_space=pltpu.SMEM),
                  pl.BlockSpec((1, 1, tm), lambda bi, j: (bi * nt + j, 0, 0), memory_space=pltpu.SMEM),
                  pl.BlockSpec(memory_space=pl.ANY),
                  pl.BlockSpec((1, tm, d), lambda bi, j: (bi, j + off, 0)),
                  pl.BlockSpec((tm, LANES), lambda bi, j: (bi * nt + j, 0)),
                  pl.BlockSpec((1, 6, d), lambda bi, j: (bi, 0, 0)),
                  pl.BlockSpec((1, d), lambda bi, j: (0, 0))],
        out_specs=pl.BlockSpec((1, tm, d), lambda bi, j: (bi, j, 0)),
        scratch_shapes=[pltpu.VMEM((tm, d), F32), pltpu.VMEM((tm, d), F32), pltpu.SemaphoreType.DMA((2,))],
        compiler_params=_cparams(("arbitrary", "arbitrary")),
        name="moe_combine",
    )(pos0, pos1, y, x, wts, mod, g_final.reshape(1, d))


def _dispatch_plan(idx, n_experts, tm):
    t = idx.shape[0]
    e_flat = idx.reshape(-1)
    onehot = (e_flat[:, None] == jnp.arange(n_experts, dtype=jnp.int32)[None, :]).astype(jnp.int32)
    rank = jnp.sum((jnp.cumsum(onehot, axis=0) - 1) * onehot, axis=1)
    counts = jnp.sum(onehot, axis=0)
    padded = ((counts + tm - 1) // tm) * tm
    ends = jnp.cumsum(padded)
    starts = ends - padded
    dest = starts[e_flat] + rank
    n_tiles = (TOP_K * t) // tm + n_experts
    tok = jnp.zeros((n_tiles * tm,), jnp.int32).at[dest].set(jnp.arange(TOP_K * t, dtype=jnp.int32) // TOP_K)
    tile_expert = jnp.searchsorted(ends, jnp.arange(n_tiles, dtype=jnp.int32) * tm, side="right").astype(jnp.int32)
    tile_expert = jnp.minimum(tile_expert, n_experts - 1)
    n_active = (ends[-1] // tm).astype(jnp.int32).reshape(1)
    last = jnp.maximum(n_active[0] - 1, 0)
    tile_expert = jnp.where(jnp.arange(n_tiles) < n_active[0], tile_expert, tile_expert[last])
    pos = dest.reshape(t, TOP_K)
    return tok.reshape(n_tiles, 1, tm), tile_expert, n_active, pos[:, 0], pos[:, 1]


def _rope_tables(lc, n):
    rows = n // GRID_W
    row = jnp.repeat(jnp.arange(rows), GRID_W).astype(F32)
    col = jnp.tile(jnp.arange(GRID_W), rows).astype(F32)
    axis_dim = HEAD_DIM // 2
    inv = ROPE_THETA ** (-jnp.arange(axis_dim // 2, dtype=F32) / (axis_dim // 2))
    ang_r = row[:, None] * inv[None, :]
    ang_c = col[:, None] * inv[None, :]
    ang = jnp.concatenate([ang_r, ang_r, ang_c, ang_c], axis=-1)
    cos, sin = jnp.cos(ang), jnp.sin(ang)
    first_half = (jnp.arange(HEAD_DIM) % (HEAD_DIM // 2)) < (HEAD_DIM // 4)
    sa = jnp.where(first_half[None, :], -sin, 0.0)
    sb = jnp.where(first_half[None, :], 0.0, sin)
    ident = lambda v: jnp.full((lc, HEAD_DIM), v, F32)
    full = lambda ctx_v, t: jnp.tile(jnp.concatenate([ident(ctx_v), t], axis=0), (1, LANES // HEAD_DIM))
    return full(1.0, cos), full(0.0, sa), full(0.0, sb)


def _pick_tile(n, candidates):
    for c in candidates:
        if n % c == 0:
            return c
    raise ValueError(f"no tile size among {candidates} divides {n}")


def kernel(x, c, ctx, c_ctx, w_ada, b_ada, g_norm1, g_norm2, w_in, w_dw, b_dw, ln_g, ln_b, lam_q1, lam_k1, lam_q2, lam_k2, g_subln, sink, w_out, w_ff_gate, w_ff_up, w_ff_down, w_router, w_ex_gate, w_ex_up, w_ex_down, g_final):
    b, n, d = x.shape
    lc = ctx.shape[1]
    s = lc + n
    depth = w_ada.shape[0]
    conv_ch = w_dw.shape[-1]
    n_swa_q = sink.shape[1] * HEAD_DIM
    n_diff_v = d - conv_ch - n_swa_q
    n_kv = n_swa_q // 3
    sizes = (conv_ch, n_diff_v, n_swa_q, n_diff_v, n_diff_v, n_kv, n_kv)
    assert sum(sizes) + conv_ch == w_in.shape[2]
    n_experts = w_router.shape[2]

    tm = 256
    assert lc % tm == 0 and n % tm == 0
    tm_ffn = _pick_tile(s, (768, 512, 256))
    tf = _pick_tile(w_ff_gate.shape[2], (512, 256, 128))
    tm_moe = 512

    xs = jnp.concatenate([ctx, x], axis=1)
    cvec = jnp.concatenate([c, c_ctx[None, :]], axis=0)
    mods = _ada(cvec, w_ada, b_ada).reshape(depth, b + 1, 6, d)
    tabs = _rope_tables(lc, n)

    out = None
    for l in range(depth):
        last = l == depth - 1
        lam_init = 0.8 - 0.6 * math.exp(-0.3 * l)
        mod = mods[l]
        lam_params = jnp.stack([lam_q1[l], lam_k1[l], lam_q2[l], lam_k2[l]], axis=0)
        y, qb, qc, kb, vb, kc, vc = _in_proj(xs, g_norm1[l], mod, w_in[l].astype(BF16), tabs, sizes, lc, tm)
        a = _conv_module(y, w_dw[l], b_dw[l], ln_g[l], ln_b[l], lc, tm)
        ob = _diff_attention(qb, kb, vb, lam_params, g_subln[l], lam_init, lc, not last, tq=256, tk=256)
        oc = _swa(qc, kc, vc, sink[l], lc, not last, tq=WINDOW)
        xs, h2 = _out_proj(a, ob, oc, xs, w_out[l].astype(BF16), mod, g_norm2[l], lc, not last, tm,
                           F32 if l % 2 == 1 else BF16)
        i = l // 2
        if l % 2 == 0:
            assert not last, "a final dense layer would need its own final-norm epilogue"
            xs = _ffn_dense(h2, xs, w_ff_gate[i].astype(BF16), w_ff_up[i].astype(BF16),
                            w_ff_down[i].astype(BF16), mod, lc, tm_ffn, tf)
        else:
            assert last, "the MoE layer carries the final norm and drops the context rows"
            h_flat = h2.reshape(b * n, d)
            idx, wts = _router(h_flat, w_router[i], tm)
            tok, tile_expert, n_active, pos0, pos1 = _dispatch_plan(idx[:, :TOP_K], n_experts, tm_moe)
            xg = _gather_rows(h_flat, tok, n_active, tm_moe)
            yexp = _moe_ffn(xg, w_ex_gate[i].astype(BF16), w_ex_up[i].astype(BF16), w_ex_down[i].astype(BF16),
                            tile_expert, n_active, tm_moe, tf)
            nt = (b * n) // tm
            out = _combine(yexp, pos0.reshape(nt, 1, tm), pos1.reshape(nt, 1, tm), xs, wts, mod, g_final, tm)
    return out
```

```python
import functools
import math

import jax
import jax.numpy as jnp
from jax import lax
from jax.experimental import pallas as pl
from jax.experimental.pallas import tpu as pltpu

F32 = jnp.float32
BF16 = jnp.bfloat16

HEAD_DIM = 64
GRID_W = 64
ROPE_THETA = 10000.0
NORM_EPS = 1e-6
MASK_VALUE = -1e30
LOG2E = math.log2(math.e)
CONV_K = 31
CONV_HALO = 16
WINDOW = 128
TOP_K = 2
LANES = 128
VMEM_LIMIT = 56 * 1024 * 1024


def _cparams(sem):
    return pltpu.CompilerParams(dimension_semantics=sem, vmem_limit_bytes=VMEM_LIMIT)


def _rms_mod(x, g, shift, scale):
    y = x * lax.rsqrt(jnp.mean(x * x, axis=-1, keepdims=True) + NORM_EPS)
    return (y * g) * (1.0 + scale) + shift


def _sigmoid(x):
    return 1.0 / (1.0 + jnp.exp(-x))


def _ada_kernel(c_ref, w_ref, b_ref, o_ref):
    c = c_ref[...]
    s = (c * _sigmoid(c)).astype(BF16)
    o_ref[0] = jnp.dot(s, w_ref[0].astype(BF16), preferred_element_type=F32) + b_ref[0]


def _ada(cvec, w_ada, b_ada):
    depth, d, n6 = w_ada.shape
    rows = cvec.shape[0]
    tn = 1024
    return pl.pallas_call(
        _ada_kernel,
        out_shape=jax.ShapeDtypeStruct((depth, rows, n6), F32),
        grid=(depth, n6 // tn),
        in_specs=[pl.BlockSpec((rows, d), lambda l, j: (0, 0)),
                  pl.BlockSpec((1, d, tn), lambda l, j: (l, 0, j)),
                  pl.BlockSpec((1, 1, tn), lambda l, j: (l, 0, j))],
        out_specs=pl.BlockSpec((1, rows, tn), lambda l, j: (l, 0, j)),
        compiler_params=_cparams(("arbitrary", "arbitrary")),
        name="ada",
    )(cvec, w_ada, b_ada.reshape(depth, 1, n6))


def _in_kernel(x_ref, g_ref, mod_ref, w_ref, cos_ref, sa_ref, sb_ref,
               y_ref, qb_ref, qc_ref, kb_ref, vb_ref, kc_ref, vc_ref, *, sizes):
    conv_ch, n_bq, n_cq, n_bk, n_bv, n_ck, n_cv = sizes
    x = x_ref[0]
    hb = _rms_mod(x, g_ref[...], mod_ref[0, 0:1, :], mod_ref[0, 1:2, :]).astype(BF16)

    def mm(lo, width):
        return jnp.dot(hb, w_ref[:, lo:lo + width], preferred_element_type=F32)

    cos, sa, sb = cos_ref[...], sa_ref[...], sb_ref[...]

    def rope_store(u, out_ref, scale, transposed=False):
        for gi in range(u.shape[1] // LANES):
            ug = u[:, gi * LANES:(gi + 1) * LANES]
            r = (ug * cos + pltpu.roll(ug, LANES - 16, 1) * sa + pltpu.roll(ug, 16, 1) * sb) * scale
            if transposed:
                out_ref[0, gi * LANES:(gi + 1) * LANES, :] = r.T.astype(out_ref.dtype)
            else:
                out_ref[0, :, gi * LANES:(gi + 1) * LANES] = r.astype(out_ref.dtype)

    off = 0
    u = mm(off, 2 * conv_ch)
    y_ref[0] = u[:, :conv_ch] * _sigmoid(u[:, conv_ch:])
    off += 2 * conv_ch
    qscale = HEAD_DIM ** -0.5
    rope_store(mm(off, n_bq), qb_ref, qscale * LOG2E, transposed=True)
    off += n_bq
    rope_store(mm(off, n_cq), qc_ref, qscale * LOG2E, transposed=True)
    off += n_cq
    rope_store(mm(off, n_bk), kb_ref, 1.0)
    off += n_bk
    v = mm(off, n_bv)
    for gi in range(n_bv // LANES):
        vb_ref[0, gi, 0] = v[:, gi * LANES:(gi + 1) * LANES].T.astype(BF16)
    off += n_bv
    rope_store(mm(off, n_ck), kc_ref, 1.0)
    off += n_ck
    v = mm(off, n_cv)
    for gi in range(n_cv // LANES):
        vc_ref[0, gi * LANES:(gi + 1) * LANES, :] = v[:, gi * LANES:(gi + 1) * LANES].T.astype(BF16)


def _in_proj(x, g, mod, w_bf16, tabs, sizes, lc, tm):
    b, s, d = x.shape
    conv_ch, n_bq, n_cq, n_bk, n_bv, n_ck, n_cv = sizes
    d_in = w_bf16.shape[1]
    nctx = lc // tm
    n_mod = mod.shape[0]

    def row(bi, j):
        return (bi, j, 0)

    def modrow(bi, j):
        return (jnp.where(j < nctx, n_mod - 1, bi), 0, 0)

    widths = (conv_ch, n_bq, n_cq, n_bk, n_bv, n_ck, n_cv)
    dts = (F32, BF16, BF16, BF16, BF16, BF16, BF16)
    out_shape = [jax.ShapeDtypeStruct((b, s, w), dt) for w, dt in zip(widths, dts)]
    out_specs = [pl.BlockSpec((1, tm, w), row) for w in widths]
    n_vh = n_bv // LANES
    out_shape[1] = jax.ShapeDtypeStruct((b, n_bq, s), BF16)
    out_specs[1] = pl.BlockSpec((1, n_bq, tm), lambda bi, j: (bi, 0, j))
    out_shape[4] = jax.ShapeDtypeStruct((b, n_vh, s // tm, LANES, tm), BF16)
    out_specs[4] = pl.BlockSpec((1, n_vh, 1, LANES, tm), lambda bi, j: (bi, 0, j, 0, 0))
    for idx, wdt in ((2, n_cq), (6, n_cv)):
        out_shape[idx] = jax.ShapeDtypeStruct((b, wdt, s), BF16)
        out_specs[idx] = pl.BlockSpec((1, wdt, tm), lambda bi, j: (bi, 0, j))
    tab_spec = pl.BlockSpec((tm, LANES), lambda bi, j: (j, 0))
    return pl.pallas_call(
        functools.partial(_in_kernel, sizes=sizes),
        out_shape=out_shape,
        grid=(b, s // tm),
        in_specs=[pl.BlockSpec((1, tm, d), row),
                  pl.BlockSpec((1, d), lambda bi, j: (0, 0)),
                  pl.BlockSpec((1, 6, d), modrow),
                  pl.BlockSpec((d, d_in), lambda bi, j: (0, 0)),
                  tab_spec, tab_spec, tab_spec],
        out_specs=out_specs,
        compiler_params=_cparams(("arbitrary", "arbitrary")),
        name="in_proj",
    )(x, g.reshape(1, d), mod, w_bf16, *tabs)


def _conv_kernel(yp_ref, yc_ref, yn_ref, w_ref, b_ref, g_ref, bb_ref, o_ref, win_ref, *, tm, seg_tile, sub):
    j = pl.program_id(1)
    nj = pl.num_programs(1)
    has_prev = jnp.logical_and(j > 0, j != seg_tile)
    has_next = jnp.logical_and(j + 1 < nj, j + 1 != seg_tile)
    h = CONV_HALO
    win_ref[0:h, :] = jnp.where(has_prev, yp_ref[0], 0.0)
    win_ref[h:h + tm, :] = yc_ref[0]
    win_ref[h + tm:h + tm + h, :] = jnp.where(has_next, yn_ref[0], 0.0)
    base = h - CONV_K // 2
    for r0 in range(0, tm, sub):
        acc = jnp.zeros((sub, win_ref.shape[1]), F32)
        for t in range(CONV_K):
            acc = acc + w_ref[t:t + 1, :] * win_ref[pl.ds(base + r0 + t, sub), :]
        y = acc + b_ref[...]
        mu = jnp.mean(y, axis=-1, keepdims=True)
        yc = y - mu
        var = jnp.mean(yc * yc, axis=-1, keepdims=True)
        z = yc * lax.rsqrt(var + NORM_EPS) * g_ref[...] + bb_ref[...]
        o_ref[0, r0:r0 + sub, :] = (z * _sigmoid(z)).astype(o_ref.dtype)


def _conv_module(y, w_dw, b_dw, ln_g, ln_b, lc, tm):
    b, s, ch = y.shape
    h = CONV_HALO
    per = tm // h
    nblk = s // h

    def prev(bi, j):
        return (bi, jnp.maximum(j * per - 1, 0), 0)

    def nxt(bi, j):
        return (bi, jnp.minimum((j + 1) * per, nblk - 1), 0)

    vec = pl.BlockSpec((1, ch), lambda bi, j: (0, 0))
    return pl.pallas_call(
        functools.partial(_conv_kernel, tm=tm, seg_tile=lc // tm, sub=32),
        out_shape=jax.ShapeDtypeStruct((b, s, ch), BF16),
        grid=(b, s // tm),
        in_specs=[pl.BlockSpec((1, h, ch), prev),
                  pl.BlockSpec((1, tm, ch), lambda bi, j: (bi, j, 0)),
                  pl.BlockSpec((1, h, ch), nxt),
                  pl.BlockSpec((CONV_K, ch), lambda bi, j: (0, 0)),
                  vec, vec, vec],
        out_specs=pl.BlockSpec((1, tm, ch), lambda bi, j: (bi, j, 0)),
        scratch_shapes=[pltpu.VMEM((tm + 2 * h, ch), F32)],
        compiler_params=_cparams(("arbitrary", "arbitrary")),
        name="conv_module",
    )(y, y, y, w_dw.reshape(CONV_K, ch), b_dw.reshape(1, ch), ln_g.reshape(1, ch), ln_b.reshape(1, ch))


def _diff_kernel(lamp_ref, qt_ref, k_ref, vt_ref, g_ref, o_ref, m_scr, l_scr, acc_scr, sa_scr, sb_scr,
                 *, tq, tk, unroll, q_off, n_ctx_qblk, ctx_chunks, all_chunks, lam_init):
    i = pl.program_id(2) + q_off
    qt = qt_ref[0]
    sub = lax.broadcasted_iota(jnp.int32, qt.shape, 0)
    zero = jnp.zeros_like(qt)
    rhs = jnp.concatenate([jnp.where(sub < HEAD_DIM, qt, zero), jnp.where(sub >= HEAD_DIM, qt, zero)], axis=1)
    m_scr[...] = jnp.full(m_scr.shape, MASK_VALUE, F32)
    l_scr[...] = jnp.zeros(l_scr.shape, F32)
    acc_scr[...] = jnp.zeros(acc_scr.shape, F32)

    def scores(gidx, count, s_ref):
        mx = None
        for u in range(count):
            start = pl.multiple_of((gidx * count + u) * tk, tk)
            s = jnp.dot(k_ref[0, pl.ds(start, tk), :], rhs, preferred_element_type=F32)
            s_ref[u * tk:(u + 1) * tk, :] = s
            cm = jnp.max(s, axis=0, keepdims=True)
            mx = cm if mx is None else jnp.maximum(mx, cm)
        return mx

    def accumulate(gidx, count, s_ref, mx):
        m_prev = m_scr[...]
        m_new = jnp.maximum(m_prev, mx)
        alpha = jnp.exp2(m_prev - m_new)
        l_new = alpha * l_scr[...]
        pv = None
        for u in range(count):
            p = jnp.exp2(s_ref[u * tk:(u + 1) * tk, :] - m_new)
            l_new = l_new + jnp.sum(p, axis=0, keepdims=True)
            d = jnp.dot(vt_ref[0, 0, gidx * count + u], p.astype(BF16), preferred_element_type=F32)
            pv = d if pv is None else pv + d
        l_scr[...] = l_new
        acc_scr[...] = alpha * acc_scr[...] + pv
        m_scr[...] = m_new

    def context_keys_only():
        for c in range(ctx_chunks):
            accumulate(c, 1, sa_scr, scores(c, 1, sa_scr))

    def all_keys():
        n_groups = all_chunks // unroll
        pairs = (n_groups - 1) // 2
        mx_a = scores(0, unroll, sa_scr)

        def pair_body(j, mx_a):
            mx_b = scores(2 * j + 1, unroll, sb_scr)
            accumulate(2 * j, unroll, sa_scr, mx_a)
            mx_a = scores(2 * j + 2, unroll, sa_scr)
            accumulate(2 * j + 1, unroll, sb_scr, mx_b)
            return mx_a

        mx_a = lax.fori_loop(0, pairs, pair_body, mx_a)
        if (n_groups - 1) % 2 == 0:
            accumulate(n_groups - 1, unroll, sa_scr, mx_a)
        else:
            mx_b = scores(n_groups - 1, unroll, sb_scr)
            accumulate(n_groups - 2, unroll, sa_scr, mx_a)
            accumulate(n_groups - 1, unroll, sb_scr, mx_b)

    if q_off >= n_ctx_qblk:
        all_keys()
    else:
        pl.when(i < n_ctx_qblk)(context_keys_only)
        pl.when(i >= n_ctx_qblk)(all_keys)

    o = acc_scr[...] / l_scr[...]
    lp = lamp_ref[...]
    lam = (jnp.exp(jnp.sum(lp[0:1] * lp[1:2], axis=1, keepdims=True))
           - jnp.exp(jnp.sum(lp[2:3] * lp[3:4], axis=1, keepdims=True)) + lam_init)
    od = o[:, :tq] - lam * o[:, tq:]
    y = od * lax.rsqrt(jnp.mean(od * od, axis=0, keepdims=True) + NORM_EPS)
    o_ref[0] = (y * g_ref[...] * (1.0 - lam_init)).T.astype(o_ref.dtype)


def _diff_attention(qbt, kb, vbt, lam_params, g_subln, lam_init, lc, with_ctx, tq, unroll):
    b, s, w = kb.shape
    heads = w // (2 * HEAD_DIM)
    tk = vbt.shape[4]
    q_off = 0 if with_ctx else lc // tq
    nq = s // tq - q_off
    assert (s // tk) % unroll == 0 and lc % tk == 0
    kern = functools.partial(_diff_kernel, tq=tq, tk=tk, unroll=unroll, q_off=q_off, n_ctx_qblk=lc // tq,
                             ctx_chunks=lc // tk, all_chunks=s // tk, lam_init=lam_init)
    return pl.pallas_call(
        kern,
        out_shape=jax.ShapeDtypeStruct((b, nq * tq, w), BF16),
        grid=(b, heads, nq),
        in_specs=[pl.BlockSpec((4, HEAD_DIM), lambda bi, h, i: (0, 0)),
                  pl.BlockSpec((1, 2 * HEAD_DIM, tq), lambda bi, h, i: (bi, h, i + q_off)),
                  pl.BlockSpec((1, s, 2 * HEAD_DIM), lambda bi, h, i: (bi, 0, h)),
                  pl.BlockSpec((1, 1, s // tk, 2 * HEAD_DIM, tk), lambda bi, h, i: (bi, h, 0, 0, 0)),
                  pl.BlockSpec((2 * HEAD_DIM, 1), lambda bi, h, i: (0, 0))],
        out_specs=pl.BlockSpec((1, tq, 2 * HEAD_DIM), lambda bi, h, i: (bi, i, h)),
        scratch_shapes=[pltpu.VMEM((1, 2 * tq), F32), pltpu.VMEM((1, 2 * tq), F32),
                        pltpu.VMEM((2 * HEAD_DIM, 2 * tq), F32),
                        pltpu.VMEM((unroll * tk, 2 * tq), F32), pltpu.VMEM((unroll * tk, 2 * tq), F32)],
        compiler_params=_cparams(("arbitrary", "arbitrary", "arbitrary")),
        name="diff_attention",
    )(lam_params, qbt, kb, vbt, g_subln.reshape(2 * HEAD_DIM, 1))


def _swa_kernel(sink_ref, qt_ref, kctx_ref, vtctx_ref, kp_ref, kc_ref, kn_ref, vtp_ref, vtc_ref, vtn_ref, o_ref,
                *, tq, q_off, n_ctx_qblk, n_lat, kv_heads, group):
    i = pl.program_id(1) + q_off
    is_lat = i >= n_ctx_qblk
    q0 = (i - n_ctx_qblk) * tq
    wq = group * tq
    kpos = q0 - tq + lax.broadcasted_iota(jnp.int32, (3 * tq, wq), 0)
    qpos = q0 + jnp.bitwise_and(lax.broadcasted_iota(jnp.int32, (3 * tq, wq), 1), tq - 1)
    valid = jnp.logical_and(jnp.abs(qpos - kpos) <= WINDOW, jnp.logical_and(kpos >= 0, kpos < n_lat))
    valid = jnp.logical_and(valid, is_lat)
    qt = qt_ref[0]
    kloc = jnp.concatenate([kp_ref[0], kc_ref[0], kn_ref[0]], axis=0)
    vtloc = jnp.concatenate([vtp_ref[0], vtc_ref[0], vtn_ref[0]], axis=1)
    kctx = kctx_ref[0]
    vtctx = vtctx_ref[0]
    zeros = jnp.zeros((HEAD_DIM, tq), qt.dtype)
    heads_out = []
    for h in range(kv_heads):
        gs = slice((h // 2) * LANES, (h // 2 + 1) * LANES)
        upper = h % 2 == 1
        cols = []
        for gi in range(group):
            hq = h * group + gi
            qh = qt[hq * HEAD_DIM:(hq + 1) * HEAD_DIM, :]
            cols.append(jnp.concatenate([zeros, qh] if upper else [qh, zeros], axis=0))
        rhs = jnp.concatenate(cols, axis=1)
        sink = jnp.concatenate(
            [jnp.full((1, tq), sink_ref[h * group + gi] * LOG2E, F32) for gi in range(group)], axis=1)
        s_ctx = jnp.dot(kctx[:, gs], rhs, preferred_element_type=F32)
        s_loc = jnp.dot(kloc[:, gs], rhs, preferred_element_type=F32)
        s_loc = jnp.where(valid, s_loc, MASK_VALUE)
        m = jnp.maximum(jnp.maximum(jnp.max(s_ctx, axis=0, keepdims=True), jnp.max(s_loc, axis=0, keepdims=True)), sink)
        e_ctx = jnp.exp2(s_ctx - m)
        e_loc = jnp.exp2(s_loc - m)
        denom = jnp.sum(e_ctx, axis=0, keepdims=True) + jnp.sum(e_loc, axis=0, keepdims=True) + jnp.exp2(sink - m)
        ot = (jnp.dot(vtctx[gs, :], e_ctx.astype(BF16), preferred_element_type=F32)
              + jnp.dot(vtloc[gs, :], e_loc.astype(BF16), preferred_element_type=F32))
        ot = (ot[HEAD_DIM:, :] if upper else ot[:HEAD_DIM, :]) / denom
        for gi in range(group):
            heads_out.append(ot[:, gi * tq:(gi + 1) * tq])
    for pi in range(len(heads_out) // 2):
        pair = jnp.concatenate([heads_out[2 * pi], heads_out[2 * pi + 1]], axis=0)
        o_ref[0, :, pi * LANES:(pi + 1) * LANES] = pair.T.astype(o_ref.dtype)


def _swa(qct, kc, vct, sink, lc, with_ctx, tq):
    b, s, kvw = kc.shape
    w = qct.shape[1]
    kv_heads = kvw // HEAD_DIM
    group = w // kvw
    assert tq & (tq - 1) == 0 and kv_heads % 2 == 0
    q_off = 0 if with_ctx else lc // tq
    nq = s // tq - q_off
    nblk = s // tq

    def rows(f):
        return pl.BlockSpec((1, tq, kvw), lambda bi, i: (bi, f(i + q_off), 0))

    def cols(f):
        return pl.BlockSpec((1, kvw, tq), lambda bi, i: (bi, 0, f(i + q_off)))

    cur = lambda i: i
    prev = lambda i: jnp.maximum(i - 1, 0)
    nxt = lambda i: jnp.minimum(i + 1, nblk - 1)
    kern = functools.partial(_swa_kernel, tq=tq, q_off=q_off, n_ctx_qblk=lc // tq, n_lat=s - lc,
                             kv_heads=kv_heads, group=group)
    return pl.pallas_call(
        kern,
        out_shape=jax.ShapeDtypeStruct((b, nq * tq, w), BF16),
        grid=(b, nq),
        in_specs=[pl.BlockSpec(memory_space=pltpu.SMEM),
                  pl.BlockSpec((1, w, tq), lambda bi, i: (bi, 0, i + q_off)),
                  pl.BlockSpec((1, lc, kvw), lambda bi, i: (bi, 0, 0)),
                  pl.BlockSpec((1, kvw, lc), lambda bi, i: (bi, 0, 0)),
                  rows(prev), rows(cur), rows(nxt), cols(prev), cols(cur), cols(nxt)],
        out_specs=pl.BlockSpec((1, tq, w), lambda bi, i: (bi, i, 0)),
        compiler_params=_cparams(("arbitrary", "arbitrary")),
        name="swa",
    )(sink, qct, kc, vct, kc, kc, kc, vct, vct, vct)


def _out_kernel(a_ref, ob_ref, oc_ref, x_ref, w_ref, mod_ref, g_ref, xo_ref, h_ref):
    na, nb = a_ref.shape[2], ob_ref.shape[2]
    mix = jnp.dot(a_ref[0], w_ref[0:na, :], preferred_element_type=F32)
    mix = mix + jnp.dot(ob_ref[0], w_ref[na:na + nb, :], preferred_element_type=F32)
    mix = mix + jnp.dot(oc_ref[0], w_ref[na + nb:, :], preferred_element_type=F32)
    xn = x_ref[0] + mod_ref[0, 2:3, :] * mix
    xo_ref[0] = xn
    h_ref[0] = _rms_mod(xn, g_ref[...], mod_ref[0, 3:4, :], mod_ref[0, 4:5, :]).astype(h_ref.dtype)


def _out_proj(a, ob, oc, x, w_bf16, mod, g2, lc, with_ctx, tm, h_dtype):
    b, s, d = x.shape
    nctx = lc // tm
    q_off = 0 if with_ctx else nctx
    nt = s // tm - q_off
    n_mod = mod.shape[0]
    out_rows = s - q_off * tm

    def row(bi, j):
        return (bi, j + q_off, 0)

    def modrow(bi, j):
        return (jnp.where(j + q_off < nctx, n_mod - 1, bi), 0, 0)

    def own(bi, j):
        return (bi, j, 0)

    assert ob.shape[1] == out_rows and oc.shape[1] == out_rows
    return pl.pallas_call(
        _out_kernel,
        out_shape=(jax.ShapeDtypeStruct((b, out_rows, d), F32), jax.ShapeDtypeStruct((b, out_rows, d), h_dtype)),
        grid=(b, nt),
        in_specs=[pl.BlockSpec((1, tm, a.shape[2]), row), pl.BlockSpec((1, tm, ob.shape[2]), own),
                  pl.BlockSpec((1, tm, oc.shape[2]), own), pl.BlockSpec((1, tm, d), row),
                  pl.BlockSpec(w_bf16.shape, lambda bi, j: (0, 0)),
                  pl.BlockSpec((1, 6, d), modrow),
                  pl.BlockSpec((1, d), lambda bi, j: (0, 0))],
        out_specs=(pl.BlockSpec((1, tm, d), own), pl.BlockSpec((1, tm, d), own)),
        compiler_params=_cparams(("arbitrary", "arbitrary")),
        name="out_proj",
    )(a, ob, oc, x, w_bf16, mod, g2.reshape(1, d))


def _ffn_kernel(h_ref, x_ref, wg_ref, wu_ref, wd_ref, modb_ref, modc_ref, o_ref, acc_ref, *, tm, lc):
    f = pl.program_id(2)

    @pl.when(f == 0)
    def _():
        acc_ref[...] = jnp.zeros(acc_ref.shape, F32)

    h = h_ref[0]
    g = jnp.dot(h, wg_ref[...], preferred_element_type=F32)
    u = jnp.dot(h, wu_ref[...], preferred_element_type=F32)
    act = (g * _sigmoid(g) * u).astype(BF16)
    acc_ref[...] += jnp.dot(act, wd_ref[...], preferred_element_type=F32)

    @pl.when(f == pl.num_programs(2) - 1)
    def _():
        rows = pl.program_id(1) * tm + lax.broadcasted_iota(jnp.int32, (tm, 1), 0)
        gate = jnp.where(rows < lc, modc_ref[0, 5:6, :], modb_ref[0, 5:6, :])
        o_ref[0] = x_ref[0] + gate * acc_ref[...]


def _ffn_dense(h, x, wg, wu, wd, mod, lc, tm, tf):
    b, s, d = x.shape
    dff = wg.shape[1]
    n_mod = mod.shape[0]
    row = lambda bi, j, f: (bi, j, 0)
    return pl.pallas_call(
        functools.partial(_ffn_kernel, tm=tm, lc=lc),
        out_shape=jax.ShapeDtypeStruct((b, s, d), F32),
        grid=(b, s // tm, dff // tf),
        in_specs=[pl.BlockSpec((1, tm, d), row), pl.BlockSpec((1, tm, d), row),
                  pl.BlockSpec((d, tf), lambda bi, j, f: (0, f)),
                  pl.BlockSpec((d, tf), lambda bi, j, f: (0, f)),
                  pl.BlockSpec((tf, d), lambda bi, j, f: (f, 0)),
                  pl.BlockSpec((1, 6, d), lambda bi, j, f: (bi, 0, 0)),
                  pl.BlockSpec((1, 6, d), lambda bi, j, f: (n_mod - 1, 0, 0))],
        out_specs=pl.BlockSpec((1, tm, d), row),
        scratch_shapes=[pltpu.VMEM((tm, d), F32)],
        compiler_params=_cparams(("arbitrary", "arbitrary", "arbitrary")),
        name="ffn_dense",
    )(h, x, wg, wu, wd, mod, mod)


def _router_kernel(h_ref, w_ref, idx_ref, wt_ref):
    logits = jnp.dot(h_ref[...], w_ref[...], preferred_element_type=F32, precision=lax.Precision.HIGHEST)
    n_e = logits.shape[1]
    lane = lax.broadcasted_iota(jnp.int32, logits.shape, 1)
    m1 = jnp.max(logits, axis=1, keepdims=True)
    i1 = jnp.min(jnp.where(logits == m1, lane, n_e), axis=1, keepdims=True)
    rest = jnp.where(lane == i1, -jnp.inf, logits)
    m2 = jnp.max(rest, axis=1, keepdims=True)
    i2 = jnp.min(jnp.where(rest == m2, lane, n_e), axis=1, keepdims=True)
    e2 = jnp.exp(m2 - m1)
    den = 1.0 + e2
    out_lane = lax.broadcasted_iota(jnp.int32, idx_ref.shape, 1)
    idx_ref[...] = jnp.where(out_lane == 0, i1, jnp.where(out_lane == 1, i2, 0))
    wt_ref[...] = jnp.where(out_lane == 0, 1.0 / den, jnp.where(out_lane == 1, e2 / den, 0.0))


def _router(h_flat, w_router, tm):
    t, d = h_flat.shape
    n_e = w_router.shape[1]
    return pl.pallas_call(
        _router_kernel,
        out_shape=(jax.ShapeDtypeStruct((t, LANES), jnp.int32), jax.ShapeDtypeStruct((t, LANES), F32)),
        grid=(t // tm,),
        in_specs=[pl.BlockSpec((tm, d), lambda i: (i, 0)), pl.BlockSpec((d, n_e), lambda i: (0, 0))],
        out_specs=(pl.BlockSpec((tm, LANES), lambda i: (i, 0)), pl.BlockSpec((tm, LANES), lambda i: (i, 0))),
        compiler_params=_cparams(("arbitrary",)),
        name="router",
    )(h_flat, w_router)


def _row_copy(src_hbm, row, dst, r, sem):
    return pltpu.make_async_copy(src_hbm.at[pl.ds(row, 1)], dst.at[pl.ds(r, 1)], sem)


def _start_rows(src_hbm, idx_ref, dst, sem, n):
    def body(r, carry):
        _row_copy(src_hbm, idx_ref[0, 0, r], dst, r, sem).start()
        return carry

    lax.fori_loop(0, n, body, 0, unroll=8)


def _wait_rows(src_hbm, dst, sem, n):
    def body(r, carry):
        _row_copy(src_hbm, 0, dst, r, sem).wait()
        return carry

    lax.fori_loop(0, n, body, 0, unroll=8)


def _gather_kernel(nact_ref, tok_ref, tok_next_ref, h_hbm, o_ref, buf, sem, *, tm):
    i = pl.program_id(0)
    nact = nact_ref[0]
    slot = lax.rem(i, 2)

    @pl.when(jnp.logical_and(i == 0, nact > 0))
    def _():
        _start_rows(h_hbm, tok_ref, buf.at[0], sem.at[0], tm)

    @pl.when(i + 1 < nact)
    def _():
        _start_rows(h_hbm, tok_next_ref, buf.at[1 - slot], sem.at[1 - slot], tm)

    @pl.when(i < nact)
    def _():
        _wait_rows(h_hbm, buf.at[slot], sem.at[slot], tm)
        o_ref[...] = buf[slot].astype(o_ref.dtype)

    @pl.when(i >= nact)
    def _():
        o_ref[...] = jnp.zeros(o_ref.shape, o_ref.dtype)


def _gather_rows(h_flat, tok, n_active, tm):
    n_tiles = tok.shape[0]
    d = h_flat.shape[1]
    grid_spec = pltpu.PrefetchScalarGridSpec(
        num_scalar_prefetch=1,
        grid=(n_tiles,),
        in_specs=[pl.BlockSpec((1, 1, tm), lambda i, na: (i, 0, 0), memory_space=pltpu.SMEM),
                  pl.BlockSpec((1, 1, tm), lambda i, na: (jnp.minimum(i + 1, n_tiles - 1), 0, 0),
                               memory_space=pltpu.SMEM),
                  pl.BlockSpec(memory_space=pl.ANY)],
        out_specs=pl.BlockSpec((tm, d), lambda i, na: (i, 0)),
        scratch_shapes=[pltpu.VMEM((2, tm, d), h_flat.dtype), pltpu.SemaphoreType.DMA((2,))],
    )
    return pl.pallas_call(
        functools.partial(_gather_kernel, tm=tm),
        out_shape=jax.ShapeDtypeStruct((n_tiles * tm, d), BF16),
        grid_spec=grid_spec,
        compiler_params=_cparams(("arbitrary",)),
        name="moe_gather",
    )(n_active, tok, tok, h_flat)


def _moe_ffn_kernel(te_ref, nact_ref, x_ref, wg_ref, wu_ref, wd_ref, o_ref, acc_ref):
    i = pl.program_id(0)
    f = pl.program_id(1)

    @pl.when(i < nact_ref[0])
    def _():
        @pl.when(f == 0)
        def _():
            acc_ref[...] = jnp.zeros(acc_ref.shape, F32)

        x = x_ref[...]
        g = jnp.dot(x, wg_ref[0], preferred_element_type=F32)
        u = jnp.dot(x, wu_ref[0], preferred_element_type=F32)
        act = (g * _sigmoid(g) * u).astype(BF16)
        acc_ref[...] += jnp.dot(act, wd_ref[0], preferred_element_type=F32)

        @pl.when(f == pl.num_programs(1) - 1)
        def _():
            o_ref[...] = acc_ref[...]

    @pl.when(jnp.logical_and(i >= nact_ref[0], f == 0))
    def _():
        o_ref[...] = jnp.zeros(o_ref.shape, o_ref.dtype)


def _moe_ffn(xs, wg, wu, wd, tile_expert, n_active, tm, tf):
    p, d = xs.shape
    dff = wg.shape[2]
    nf = dff // tf
    n_tiles = p // tm

    def tile(i, f, te, na):
        return (jnp.minimum(i, na[0] - 1), 0)

    def fcol(i, f, te, na):
        return jnp.where(i < na[0], f, nf - 1)

    grid_spec = pltpu.PrefetchScalarGridSpec(
        num_scalar_prefetch=2,
        grid=(n_tiles, nf),
        in_specs=[pl.BlockSpec((tm, d), tile),
                  pl.BlockSpec((1, d, tf), lambda i, f, te, na: (te[i], 0, fcol(i, f, te, na))),
                  pl.BlockSpec((1, d, tf), lambda i, f, te, na: (te[i], 0, fcol(i, f, te, na))),
                  pl.BlockSpec((1, tf, d), lambda i, f, te, na: (te[i], fcol(i, f, te, na), 0))],
        out_specs=pl.BlockSpec((tm, d), lambda i, f, te, na: (i, 0)),
        scratch_shapes=[pltpu.VMEM((tm, d), F32)],
    )
    return pl.pallas_call(
        _moe_ffn_kernel,
        out_shape=jax.ShapeDtypeStruct((p, d), F32),
        grid_spec=grid_spec,
        compiler_params=_cparams(("arbitrary", "arbitrary")),
        name="moe_ffn",
    )(tile_expert, n_active, xs, wg, wu, wd)


def _combine_kernel(p0_ref, p1_ref, p0n_ref, p1n_ref, y_hbm, x_ref, wt_ref, mod_ref, g_ref, o_ref, buf, sem, *, tm):
    t = pl.program_id(0) * pl.num_programs(1) + pl.program_id(1)
    n_tiles = pl.num_programs(0) * pl.num_programs(1)
    slot = lax.rem(t, 2)

    def start(r0, r1, s):
        _start_rows(y_hbm, r0, buf.at[s, 0], sem.at[s, 0], tm)
        _start_rows(y_hbm, r1, buf.at[s, 1], sem.at[s, 1], tm)

    @pl.when(t == 0)
    def _():
        start(p0_ref, p1_ref, 0)

    @pl.when(t + 1 < n_tiles)
    def _():
        start(p0n_ref, p1n_ref, 1 - slot)

    _wait_rows(y_hbm, buf.at[slot, 0], sem.at[slot, 0], tm)
    _wait_rows(y_hbm, buf.at[slot, 1], sem.at[slot, 1], tm)
    wt = wt_ref[...]
    f = wt[:, 0:1] * buf[slot, 0] + wt[:, 1:2] * buf[slot, 1]
    xn = x_ref[0] + mod_ref[0, 5:6, :] * f
    y = xn * lax.rsqrt(jnp.mean(xn * xn, axis=-1, keepdims=True) + NORM_EPS)
    o_ref[0] = y * g_ref[...]


def _combine(y, pos0, pos1, x, wts, mod, g_final, tm):
    b, n, d = x.shape
    nt = n // tm
    last = b * nt - 1
    idx = lambda f: pl.BlockSpec((1, 1, tm), lambda bi, j: (f(bi * nt + j), 0, 0), memory_space=pltpu.SMEM)
    cur = lambda t: t
    nxt = lambda t: jnp.minimum(t + 1, last)
    return pl.pallas_call(
        functools.partial(_combine_kernel, tm=tm),
        out_shape=jax.ShapeDtypeStruct((b, n, d), F32),
        grid=(b, nt),
        in_specs=[idx(cur), idx(cur), idx(nxt), idx(nxt),
                  pl.BlockSpec(memory_space=pl.ANY),
                  pl.BlockSpec((1, tm, d), lambda bi, j: (bi, j, 0)),
                  pl.BlockSpec((tm, LANES), lambda bi, j: (bi * nt + j, 0)),
                  pl.BlockSpec((1, 6, d), lambda bi, j: (bi, 0, 0)),
                  pl.BlockSpec((1, d), lambda bi, j: (0, 0))],
        out_specs=pl.BlockSpec((1, tm, d), lambda bi, j: (bi, j, 0)),
        scratch_shapes=[pltpu.VMEM((2, 2, tm, d), F32), pltpu.SemaphoreType.DMA((2, 2))],
        compiler_params=_cparams(("arbitrary", "arbitrary")),
        name="moe_combine",
    )(pos0, pos1, pos0, pos1, y, x, wts, mod, g_final.reshape(1, d))


def _dispatch_plan(idx, n_experts, tm):
    t = idx.shape[0]
    e_flat = idx.reshape(-1)
    onehot = (e_flat[:, None] == jnp.arange(n_experts, dtype=jnp.int32)[None, :]).astype(jnp.int32)
    rank = jnp.sum((jnp.cumsum(onehot, axis=0) - 1) * onehot, axis=1)
    counts = jnp.sum(onehot, axis=0)
    padded = ((counts + tm - 1) // tm) * tm
    ends = jnp.cumsum(padded)
    starts = ends - padded
    dest = starts[e_flat] + rank
    n_tiles = (TOP_K * t) // tm + n_experts
    tok = jnp.zeros((n_tiles * tm,), jnp.int32).at[dest].set(jnp.arange(TOP_K * t, dtype=jnp.int32) // TOP_K)
    tile_row0 = jnp.arange(n_tiles, dtype=jnp.int32) * tm
    tile_expert = jnp.sum((ends[None, :] <= tile_row0[:, None]).astype(jnp.int32), axis=1)
    tile_expert = jnp.minimum(tile_expert, n_experts - 1)
    n_active = (ends[-1] // tm).astype(jnp.int32).reshape(1)
    last = jnp.maximum(n_active[0] - 1, 0)
    tile_expert = jnp.where(jnp.arange(n_tiles) < n_active[0], tile_expert, tile_expert[last])
    pos = dest.reshape(t, TOP_K)
    return tok.reshape(n_tiles, 1, tm), tile_expert, n_active, pos[:, 0], pos[:, 1]


def _rope_tables(lc, n):
    rows = n // GRID_W
    row = jnp.repeat(jnp.arange(rows), GRID_W).astype(F32)
    col = jnp.tile(jnp.arange(GRID_W), rows).astype(F32)
    axis_dim = HEAD_DIM // 2
    inv = ROPE_THETA ** (-jnp.arange(axis_dim // 2, dtype=F32) / (axis_dim // 2))
    ang_r = row[:, None] * inv[None, :]
    ang_c = col[:, None] * inv[None, :]
    ang = jnp.concatenate([ang_r, ang_r, ang_c, ang_c], axis=-1)
    cos, sin = jnp.cos(ang), jnp.sin(ang)
    first_half = (jnp.arange(HEAD_DIM) % (HEAD_DIM // 2)) < (HEAD_DIM // 4)
    sa = jnp.where(first_half[None, :], -sin, 0.0)
    sb = jnp.where(first_half[None, :], 0.0, sin)
    ident = lambda v: jnp.full((lc, HEAD_DIM), v, F32)
    full = lambda ctx_v, t: jnp.tile(jnp.concatenate([ident(ctx_v), t], axis=0), (1, LANES // HEAD_DIM))
    return full(1.0, cos), full(0.0, sa), full(0.0, sb)


def _pick_tile(n, candidates):
    for c in candidates:
        if n % c == 0:
            return c
    raise ValueError(f"no tile size among {candidates} divides {n}")


def kernel(x, c, ctx, c_ctx, w_ada, b_ada, g_norm1, g_norm2, w_in, w_dw, b_dw, ln_g, ln_b, lam_q1, lam_k1, lam_q2, lam_k2, g_subln, sink, w_out, w_ff_gate, w_ff_up, w_ff_down, w_router, w_ex_gate, w_ex_up, w_ex_down, g_final):
    b, n, d = x.shape
    lc = ctx.shape[1]
    s = lc + n
    depth = w_ada.shape[0]
    conv_ch = w_dw.shape[-1]
    n_swa_q = sink.shape[1] * HEAD_DIM
    n_diff_v = d - conv_ch - n_swa_q
    n_kv = n_swa_q // 3
    sizes = (conv_ch, n_diff_v, n_swa_q, n_diff_v, n_diff_v, n_kv, n_kv)
    assert sum(sizes) + conv_ch == w_in.shape[2]
    n_experts = w_router.shape[2]

    tm = 256
    assert lc % tm == 0 and n % tm == 0
    tm_ffn = _pick_tile(s, (768, 512, 256))
    tf = _pick_tile(w_ff_gate.shape[2], (512, 256, 128))
    tm_moe = 512

    xs = jnp.concatenate([ctx, x], axis=1)
    cvec = jnp.concatenate([c, c_ctx[None, :]], axis=0)
    mods = _ada(cvec, w_ada, b_ada).reshape(depth, b + 1, 6, d)
    tabs = _rope_tables(lc, n)

    out = None
    for l in range(depth):
        last = l == depth - 1
        lam_init = 0.8 - 0.6 * math.exp(-0.3 * l)
        mod = mods[l]
        lam_params = jnp.stack([lam_q1[l], lam_k1[l], lam_q2[l], lam_k2[l]], axis=0)
        y, qb, qc, kb, vb, kc, vc = _in_proj(xs, g_norm1[l], mod, w_in[l].astype(BF16), tabs, sizes, lc, tm)
        a = _conv_module(y, w_dw[l], b_dw[l], ln_g[l], ln_b[l], lc, tm)
        ob = _diff_attention(qb, kb, vb, lam_params, g_subln[l], lam_init, lc, not last, tq=256, unroll=3)
        oc = _swa(qc, kc, vc, sink[l], lc, not last, tq=WINDOW)
        xs, h2 = _out_proj(a, ob, oc, xs, w_out[l].astype(BF16), mod, g_norm2[l], lc, not last, tm,
                           F32 if l % 2 == 1 else BF16)
        i = l // 2
        if l % 2 == 0:
            assert not last, "a final dense layer would need its own final-norm epilogue"
            xs = _ffn_dense(h2, xs, w_ff_gate[i].astype(BF16), w_ff_up[i].astype(BF16),
                            w_ff_down[i].astype(BF16), mod, lc, tm_ffn, tf)
        else:
            assert last, "the MoE layer carries the final norm and drops the context rows"
            h_flat = h2.reshape(b * n, d)
            idx, wts = _router(h_flat, w_router[i], tm)
            tok, tile_expert, n_active, pos0, pos1 = _dispatch_plan(idx[:, :TOP_K], n_experts, tm_moe)
            xg = _gather_rows(h_flat, tok, n_active, tm_moe)
            yexp = _moe_ffn(xg, w_ex_gate[i].astype(BF16), w_ex_up[i].astype(BF16), w_ex_down[i].astype(BF16),
                            tile_expert, n_active, tm_moe, tf)
            nt = (b * n) // tm
            out = _combine(yexp, pos0.reshape(nt, 1, tm), pos1.reshape(nt, 1, tm), xs, wts, mod, g_final, tm)
    return out
```

```python
import functools
import math

import jax
import jax.numpy as jnp
from jax import lax
from jax.experimental import pallas as pl
from jax.experimental.pallas import tpu as pltpu

F32 = jnp.float32
BF16 = jnp.bfloat16

HEAD_DIM = 64
GRID_W = 64
ROPE_THETA = 10000.0
NORM_EPS = 1e-6
MASK_VALUE = -1e30
LOG2E = math.log2(math.e)
CONV_K = 31
CONV_HALO = 16
WINDOW = 128
TOP_K = 2
LANES = 128
SUBLANES = 8
ONES_ROWS = 16
VMEM_LIMIT = 56 * 1024 * 1024


def _cparams(sem):
    return pltpu.CompilerParams(dimension_semantics=sem, vmem_limit_bytes=VMEM_LIMIT)


def _rms_mod(x, g, shift, scale):
    y = x * lax.rsqrt(jnp.mean(x * x, axis=-1, keepdims=True) + NORM_EPS)
    return (y * g) * (1.0 + scale) + shift


def _sigmoid(x):
    return 1.0 / (1.0 + jnp.exp(-x))


def _ada_kernel(c_ref, w_ref, b_ref, o_ref):
    c = c_ref[...]
    s = (c * _sigmoid(c)).astype(BF16)
    o_ref[0] = jnp.dot(s, w_ref[0].astype(BF16), preferred_element_type=F32) + b_ref[0]


def _ada(cvec, w_ada, b_ada):
    depth, d, n6 = w_ada.shape
    rows = cvec.shape[0]
    tn = 1024
    return pl.pallas_call(
        _ada_kernel,
        out_shape=jax.ShapeDtypeStruct((depth, rows, n6), F32),
        grid=(depth, n6 // tn),
        in_specs=[pl.BlockSpec((rows, d), lambda l, j: (0, 0)),
                  pl.BlockSpec((1, d, tn), lambda l, j: (l, 0, j)),
                  pl.BlockSpec((1, 1, tn), lambda l, j: (l, 0, j))],
        out_specs=pl.BlockSpec((1, rows, tn), lambda l, j: (l, 0, j)),
        compiler_params=_cparams(("arbitrary", "arbitrary")),
        name="ada",
    )(cvec, w_ada, b_ada.reshape(depth, 1, n6))


def _in_kernel(x_ref, g_ref, mod_ref, w_ref, cos_ref, sa_ref, sb_ref,
               y_ref, qb_ref, qc_ref, kb_ref, vb_ref, kc_ref, vc_ref, *, sizes):
    conv_ch, n_bq, n_cq, n_bk, n_bv, n_ck, n_cv = sizes
    x = x_ref[0]
    hb = _rms_mod(x, g_ref[...], mod_ref[0, 0:1, :], mod_ref[0, 1:2, :]).astype(BF16)

    def mm(lo, width):
        return jnp.dot(hb, w_ref[:, lo:lo + width], preferred_element_type=F32)

    cos, sa, sb = cos_ref[...], sa_ref[...], sb_ref[...]

    def rope_store(u, out_ref, scale, transposed=False):
        for gi in range(u.shape[1] // LANES):
            ug = u[:, gi * LANES:(gi + 1) * LANES]
            r = (ug * cos + pltpu.roll(ug, LANES - 16, 1) * sa + pltpu.roll(ug, 16, 1) * sb) * scale
            if transposed:
                out_ref[0, gi * LANES:(gi + 1) * LANES, :] = r.T.astype(out_ref.dtype)
            else:
                out_ref[0, :, gi * LANES:(gi + 1) * LANES] = r.astype(out_ref.dtype)

    off = 0
    u = mm(off, 2 * conv_ch)
    y_ref[0] = u[:, :conv_ch] * _sigmoid(u[:, conv_ch:])
    off += 2 * conv_ch
    qscale = HEAD_DIM ** -0.5
    rope_store(mm(off, n_bq), qb_ref, qscale * LOG2E, transposed=True)
    off += n_bq
    rope_store(mm(off, n_cq), qc_ref, qscale * LOG2E, transposed=True)
    off += n_cq
    rope_store(mm(off, n_bk), kb_ref, 1.0)
    off += n_bk
    v = mm(off, n_bv)
    for gi in range(n_bv // LANES):
        vb_ref[0, gi, 0, 0:LANES, :] = v[:, gi * LANES:(gi + 1) * LANES].T.astype(BF16)
        vb_ref[0, gi, 0, LANES:LANES + ONES_ROWS, :] = jnp.ones((ONES_ROWS, v.shape[0]), BF16)
    off += n_bv
    rope_store(mm(off, n_ck), kc_ref, 1.0)
    off += n_ck
    v = mm(off, n_cv)
    grp = LANES + ONES_ROWS
    for gi in range(n_cv // LANES):
        vc_ref[0, gi * grp:gi * grp + LANES, :] = v[:, gi * LANES:(gi + 1) * LANES].T.astype(BF16)
        vc_ref[0, gi * grp + LANES:(gi + 1) * grp, :] = jnp.ones((ONES_ROWS, v.shape[0]), BF16)


def _in_proj(x, g, mod, w_bf16, tabs, sizes, lc, tm):
    b, s, d = x.shape
    conv_ch, n_bq, n_cq, n_bk, n_bv, n_ck, n_cv = sizes
    d_in = w_bf16.shape[1]
    nctx = lc // tm
    n_mod = mod.shape[0]

    def row(bi, j):
        return (bi, j, 0)

    def modrow(bi, j):
        return (jnp.where(j < nctx, n_mod - 1, bi), 0, 0)

    widths = (conv_ch, n_bq, n_cq, n_bk, n_bv, n_ck, n_cv)
    dts = (F32, BF16, BF16, BF16, BF16, BF16, BF16)
    out_shape = [jax.ShapeDtypeStruct((b, s, w), dt) for w, dt in zip(widths, dts)]
    out_specs = [pl.BlockSpec((1, tm, w), row) for w in widths]
    n_vh = n_bv // LANES
    out_shape[1] = jax.ShapeDtypeStruct((b, n_bq, s), BF16)
    out_specs[1] = pl.BlockSpec((1, n_bq, tm), lambda bi, j: (bi, 0, j))
    out_shape[4] = jax.ShapeDtypeStruct((b, n_vh, s // tm, LANES + ONES_ROWS, tm), BF16)
    out_specs[4] = pl.BlockSpec((1, n_vh, 1, LANES + ONES_ROWS, tm), lambda bi, j: (bi, 0, j, 0, 0))
    for idx, wdt in ((2, n_cq), (6, (n_cv // LANES) * (LANES + ONES_ROWS))):
        out_shape[idx] = jax.ShapeDtypeStruct((b, wdt, s), BF16)
        out_specs[idx] = pl.BlockSpec((1, wdt, tm), lambda bi, j: (bi, 0, j))
    tab_spec = pl.BlockSpec((tm, LANES), lambda bi, j: (j, 0))
    return pl.pallas_call(
        functools.partial(_in_kernel, sizes=sizes),
        out_shape=out_shape,
        grid=(b, s // tm),
        in_specs=[pl.BlockSpec((1, tm, d), row),
                  pl.BlockSpec((1, d), lambda bi, j: (0, 0)),
                  pl.BlockSpec((1, 6, d), modrow),
                  pl.BlockSpec((d, d_in), lambda bi, j: (0, 0)),
                  tab_spec, tab_spec, tab_spec],
        out_specs=out_specs,
        compiler_params=_cparams(("arbitrary", "arbitrary")),
        name="in_proj",
    )(x, g.reshape(1, d), mod, w_bf16, *tabs)


def _conv_kernel(yp_ref, yc_ref, yn_ref, w_ref, b_ref, g_ref, bb_ref, o_ref, win_ref, *, tm, seg_tile, sub):
    j = pl.program_id(1)
    nj = pl.num_programs(1)
    has_prev = jnp.logical_and(j > 0, j != seg_tile)
    has_next = jnp.logical_and(j + 1 < nj, j + 1 != seg_tile)
    h = CONV_HALO
    win_ref[0, 0:h, :] = jnp.where(has_prev, yp_ref[0], 0.0)
    win_ref[0, h:h + tm, :] = yc_ref[0]
    win_ref[0, h + tm:h + tm + h, :] = jnp.where(has_next, yn_ref[0], 0.0)
    span = tm + 2 * h - SUBLANES
    for r in range(1, SUBLANES):
        win_ref[r, 0:span, :] = win_ref[0, pl.ds(r, span), :]
    base = h - CONV_K // 2
    for r0 in range(0, tm, sub):
        acc = jnp.zeros((sub, win_ref.shape[2]), F32)
        for t in range(CONV_K):
            o = base + r0 + t
            acc = acc + w_ref[t:t + 1, :] * win_ref[o % SUBLANES, o - o % SUBLANES:o - o % SUBLANES + sub, :]
        y = acc + b_ref[...]
        mu = jnp.mean(y, axis=-1, keepdims=True)
        yc = y - mu
        var = jnp.mean(yc * yc, axis=-1, keepdims=True)
        z = yc * lax.rsqrt(var + NORM_EPS) * g_ref[...] + bb_ref[...]
        o_ref[0, r0:r0 + sub, :] = (z * _sigmoid(z)).astype(o_ref.dtype)


def _conv_module(y, w_dw, b_dw, ln_g, ln_b, lc, tm):
    b, s, ch = y.shape
    h = CONV_HALO
    per = tm // h
    nblk = s // h

    def prev(bi, j):
        return (bi, jnp.maximum(j * per - 1, 0), 0)

    def nxt(bi, j):
        return (bi, jnp.minimum((j + 1) * per, nblk - 1), 0)

    vec = pl.BlockSpec((1, ch), lambda bi, j: (0, 0))
    return pl.pallas_call(
        functools.partial(_conv_kernel, tm=tm, seg_tile=lc // tm, sub=32),
        out_shape=jax.ShapeDtypeStruct((b, s, ch), BF16),
        grid=(b, s // tm),
        in_specs=[pl.BlockSpec((1, h, ch), prev),
                  pl.BlockSpec((1, tm, ch), lambda bi, j: (bi, j, 0)),
                  pl.BlockSpec((1, h, ch), nxt),
                  pl.BlockSpec((CONV_K, ch), lambda bi, j: (0, 0)),
                  vec, vec, vec],
        out_specs=pl.BlockSpec((1, tm, ch), lambda bi, j: (bi, j, 0)),
        scratch_shapes=[pltpu.VMEM((SUBLANES, tm + 2 * h, ch), F32)],
        compiler_params=_cparams(("arbitrary", "arbitrary")),
        name="conv_module",
    )(y, y, y, w_dw.reshape(CONV_K, ch), b_dw.reshape(1, ch), ln_g.reshape(1, ch), ln_b.reshape(1, ch))


def _diff_kernel(lamp_ref, qt_ref, k_ref, vt_ref, g_ref, o_ref, m_scr, acc_scr, sa_scr, sb_scr,
                 *, tq, tk, unroll, q_off, n_ctx_qblk, ctx_chunks, all_chunks, lam_init):
    i = pl.program_id(2) + q_off
    qt = qt_ref[0]
    sub = lax.broadcasted_iota(jnp.int32, qt.shape, 0)
    zero = jnp.zeros_like(qt)
    rhs = jnp.concatenate([jnp.where(sub < HEAD_DIM, qt, zero), jnp.where(sub >= HEAD_DIM, qt, zero)], axis=1)
    m_scr[...] = jnp.full(m_scr.shape, MASK_VALUE, F32)
    acc_scr[...] = jnp.zeros(acc_scr.shape, F32)

    def scores(gidx, count, s_ref):
        mx = None
        for u in range(count):
            start = pl.multiple_of((gidx * count + u) * tk, tk)
            s = jnp.dot(k_ref[0, pl.ds(start, tk), :], rhs, preferred_element_type=F32)
            s_ref[u * tk:(u + 1) * tk, :] = s
            cm = jnp.max(s, axis=0, keepdims=True)
            mx = cm if mx is None else jnp.maximum(mx, cm)
        return mx

    def accumulate(gidx, count, s_ref, mx):
        m_prev = m_scr[...]
        m_new = jnp.maximum(m_prev, mx)
        alpha = jnp.exp2(m_prev - m_new)
        pv = None
        for u in range(count):
            p = jnp.exp2((s_ref[u * tk:(u + 1) * tk, :] - m_new).astype(BF16))
            d = jnp.dot(vt_ref[0, 0, gidx * count + u], p, preferred_element_type=F32)
            pv = d if pv is None else pv + d
        acc_scr[...] = alpha * acc_scr[...] + pv
        m_scr[...] = m_new

    def context_keys_only():
        for c in range(ctx_chunks):
            accumulate(c, 1, sa_scr, scores(c, 1, sa_scr))

    def all_keys():
        n_groups = all_chunks // unroll
        pairs = (n_groups - 1) // 2
        mx_a = scores(0, unroll, sa_scr)

        def pair_body(j, mx_a):
            mx_b = scores(2 * j + 1, unroll, sb_scr)
            accumulate(2 * j, unroll, sa_scr, mx_a)
            mx_a = scores(2 * j + 2, unroll, sa_scr)
            accumulate(2 * j + 1, unroll, sb_scr, mx_b)
            return mx_a

        mx_a = lax.fori_loop(0, pairs, pair_body, mx_a)
        if (n_groups - 1) % 2 == 0:
            accumulate(n_groups - 1, unroll, sa_scr, mx_a)
        else:
            mx_b = scores(n_groups - 1, unroll, sb_scr)
            accumulate(n_groups - 2, unroll, sa_scr, mx_a)
            accumulate(n_groups - 1, unroll, sb_scr, mx_b)

    if q_off >= n_ctx_qblk:
        all_keys()
    else:
        pl.when(i < n_ctx_qblk)(context_keys_only)
        pl.when(i >= n_ctx_qblk)(all_keys)

    vdim = 2 * HEAD_DIM
    o = acc_scr[0:vdim, :] / acc_scr[vdim:vdim + 1, :]
    lp = lamp_ref[...]
    lam = (jnp.exp(jnp.sum(lp[0:1] * lp[1:2], axis=1, keepdims=True))
           - jnp.exp(jnp.sum(lp[2:3] * lp[3:4], axis=1, keepdims=True)) + lam_init)
    od = o[:, :tq] - lam * o[:, tq:]
    y = od * lax.rsqrt(jnp.mean(od * od, axis=0, keepdims=True) + NORM_EPS)
    o_ref[0] = (y * g_ref[...] * (1.0 - lam_init)).T.astype(o_ref.dtype)


def _diff_attention(qbt, kb, vbt, lam_params, g_subln, lam_init, lc, with_ctx, tq, unroll):
    b, s, w = kb.shape
    heads = w // (2 * HEAD_DIM)
    vrows, tk = vbt.shape[3], vbt.shape[4]
    assert vrows == 2 * HEAD_DIM + ONES_ROWS
    q_off = 0 if with_ctx else lc // tq
    nq = s // tq - q_off
    assert (s // tk) % unroll == 0 and lc % tk == 0
    kern = functools.partial(_diff_kernel, tq=tq, tk=tk, unroll=unroll, q_off=q_off, n_ctx_qblk=lc // tq,
                             ctx_chunks=lc // tk, all_chunks=s // tk, lam_init=lam_init)
    return pl.pallas_call(
        kern,
        out_shape=jax.ShapeDtypeStruct((b, nq * tq, w), BF16),
        grid=(b, heads, nq),
        in_specs=[pl.BlockSpec((4, HEAD_DIM), lambda bi, h, i: (0, 0)),
                  pl.BlockSpec((1, 2 * HEAD_DIM, tq), lambda bi, h, i: (bi, h, i + q_off)),
                  pl.BlockSpec((1, s, 2 * HEAD_DIM), lambda bi, h, i: (bi, 0, h)),
                  pl.BlockSpec((1, 1, s // tk, vrows, tk), lambda bi, h, i: (bi, h, 0, 0, 0)),
                  pl.BlockSpec((2 * HEAD_DIM, 1), lambda bi, h, i: (0, 0))],
        out_specs=pl.BlockSpec((1, tq, 2 * HEAD_DIM), lambda bi, h, i: (bi, i, h)),
        scratch_shapes=[pltpu.VMEM((1, 2 * tq), F32),
                        pltpu.VMEM((vrows, 2 * tq), F32),
                        pltpu.VMEM((unroll * tk, 2 * tq), F32), pltpu.VMEM((unroll * tk, 2 * tq), F32)],
        compiler_params=_cparams(("arbitrary", "arbitrary", "arbitrary")),
        name="diff_attention",
    )(lam_params, qbt, kb, vbt, g_subln.reshape(2 * HEAD_DIM, 1))


def _swa_kernel(sink_ref, qt_ref, kctx_ref, vtctx_ref, kp_ref, kc_ref, kn_ref, vtp_ref, vtc_ref, vtn_ref, o_ref,
                *, tq, q_off, n_ctx_qblk, n_lat, kv_heads, group):
    i = pl.program_id(1) + q_off
    is_lat = i >= n_ctx_qblk
    q0 = (i - n_ctx_qblk) * tq
    wq = group * tq
    kpos = q0 - tq + lax.broadcasted_iota(jnp.int32, (3 * tq, wq), 0)
    qpos = q0 + jnp.bitwise_and(lax.broadcasted_iota(jnp.int32, (3 * tq, wq), 1), tq - 1)
    valid = jnp.logical_and(jnp.abs(qpos - kpos) <= WINDOW, jnp.logical_and(kpos >= 0, kpos < n_lat))
    valid = jnp.logical_and(valid, is_lat)
    qt = qt_ref[0]
    kloc = jnp.concatenate([kp_ref[0], kc_ref[0], kn_ref[0]], axis=0)
    vtloc = jnp.concatenate([vtp_ref[0], vtc_ref[0], vtn_ref[0]], axis=1)
    kctx = kctx_ref[0]
    vtctx = vtctx_ref[0]
    zeros = jnp.zeros((HEAD_DIM, tq), qt.dtype)
    heads_out = []
    for h in range(kv_heads):
        gs = slice((h // 2) * LANES, (h // 2 + 1) * LANES)
        vs = slice((h // 2) * (LANES + ONES_ROWS), (h // 2 + 1) * (LANES + ONES_ROWS))
        upper = h % 2 == 1
        cols = []
        for gi in range(group):
            hq = h * group + gi
            qh = qt[hq * HEAD_DIM:(hq + 1) * HEAD_DIM, :]
            cols.append(jnp.concatenate([zeros, qh] if upper else [qh, zeros], axis=0))
        rhs = jnp.concatenate(cols, axis=1)
        sink = jnp.concatenate(
            [jnp.full((1, tq), sink_ref[h * group + gi] * LOG2E, F32) for gi in range(group)], axis=1)
        s_ctx = jnp.dot(kctx[:, gs], rhs, preferred_element_type=F32)
        s_loc = jnp.dot(kloc[:, gs], rhs, preferred_element_type=F32)
        s_loc = jnp.where(valid, s_loc, MASK_VALUE)
        m = jnp.maximum(jnp.maximum(jnp.max(s_ctx, axis=0, keepdims=True), jnp.max(s_loc, axis=0, keepdims=True)), sink)
        e_ctx = jnp.exp2((s_ctx - m).astype(BF16))
        e_loc = jnp.exp2((s_loc - m).astype(BF16))
        ot = (jnp.dot(vtctx[vs, :], e_ctx, preferred_element_type=F32)
              + jnp.dot(vtloc[vs, :], e_loc, preferred_element_type=F32))
        denom = ot[LANES:LANES + 1, :] + jnp.exp2(sink - m)
        ot = (ot[HEAD_DIM:LANES, :] if upper else ot[:HEAD_DIM, :]) / denom
        for gi in range(group):
            heads_out.append(ot[:, gi * tq:(gi + 1) * tq])
    for pi in range(len(heads_out) // 2):
        pair = jnp.concatenate([heads_out[2 * pi], heads_out[2 * pi + 1]], axis=0)
        o_ref[0, :, pi * LANES:(pi + 1) * LANES] = pair.T.astype(o_ref.dtype)


def _swa(qct, kc, vct, sink, lc, with_ctx, tq):
    b, s, kvw = kc.shape
    w = qct.shape[1]
    kv_heads = kvw // HEAD_DIM
    group = w // kvw
    assert tq & (tq - 1) == 0 and kv_heads % 2 == 0
    q_off = 0 if with_ctx else lc // tq
    nq = s // tq - q_off
    nblk = s // tq

    def rows(f):
        return pl.BlockSpec((1, tq, kvw), lambda bi, i: (bi, f(i + q_off), 0))

    vrows = vct.shape[1]
    assert vrows == (kvw // LANES) * (LANES + ONES_ROWS)

    def cols(f):
        return pl.BlockSpec((1, vrows, tq), lambda bi, i: (bi, 0, f(i + q_off)))

    cur = lambda i: i
    prev = lambda i: jnp.maximum(i - 1, 0)
    nxt = lambda i: jnp.minimum(i + 1, nblk - 1)
    kern = functools.partial(_swa_kernel, tq=tq, q_off=q_off, n_ctx_qblk=lc // tq, n_lat=s - lc,
                             kv_heads=kv_heads, group=group)
    return pl.pallas_call(
        kern,
        out_shape=jax.ShapeDtypeStruct((b, nq * tq, w), BF16),
        grid=(b, nq),
        in_specs=[pl.BlockSpec(memory_space=pltpu.SMEM),
                  pl.BlockSpec((1, w, tq), lambda bi, i: (bi, 0, i + q_off)),
                  pl.BlockSpec((1, lc, kvw), lambda bi, i: (bi, 0, 0)),
                  pl.BlockSpec((1, vrows, lc), lambda bi, i: (bi, 0, 0)),
                  rows(prev), rows(cur), rows(nxt), cols(prev), cols(cur), cols(nxt)],
        out_specs=pl.BlockSpec((1, tq, w), lambda bi, i: (bi, i, 0)),
        compiler_params=_cparams(("arbitrary", "arbitrary")),
        name="swa",
    )(sink, qct, kc, vct, kc, kc, kc, vct, vct, vct)


def _out_kernel(a_ref, ob_ref, oc_ref, x_ref, w_ref, mod_ref, g_ref, xo_ref, h_ref):
    na, nb = a_ref.shape[2], ob_ref.shape[2]
    mix = jnp.dot(a_ref[0], w_ref[0:na, :], preferred_element_type=F32)
    mix = mix + jnp.dot(ob_ref[0], w_ref[na:na + nb, :], preferred_element_type=F32)
    mix = mix + jnp.dot(oc_ref[0], w_ref[na + nb:, :], preferred_element_type=F32)
    xn = x_ref[0] + mod_ref[0, 2:3, :] * mix
    xo_ref[0] = xn
    h_ref[0] = _rms_mod(xn, g_ref[...], mod_ref[0, 3:4, :], mod_ref[0, 4:5, :]).astype(h_ref.dtype)


def _out_proj(a, ob, oc, x, w_bf16, mod, g2, lc, with_ctx, tm, h_dtype):
    b, s, d = x.shape
    nctx = lc // tm
    q_off = 0 if with_ctx else nctx
    nt = s // tm - q_off
    n_mod = mod.shape[0]
    out_rows = s - q_off * tm

    def row(bi, j):
        return (bi, j + q_off, 0)

    def modrow(bi, j):
        return (jnp.where(j + q_off < nctx, n_mod - 1, bi), 0, 0)

    def own(bi, j):
        return (bi, j, 0)

    assert ob.shape[1] == out_rows and oc.shape[1] == out_rows
    return pl.pallas_call(
        _out_kernel,
        out_shape=(jax.ShapeDtypeStruct((b, out_rows, d), F32), jax.ShapeDtypeStruct((b, out_rows, d), h_dtype)),
        grid=(b, nt),
        in_specs=[pl.BlockSpec((1, tm, a.shape[2]), row), pl.BlockSpec((1, tm, ob.shape[2]), own),
                  pl.BlockSpec((1, tm, oc.shape[2]), own), pl.BlockSpec((1, tm, d), row),
                  pl.BlockSpec(w_bf16.shape, lambda bi, j: (0, 0)),
                  pl.BlockSpec((1, 6, d), modrow),
                  pl.BlockSpec((1, d), lambda bi, j: (0, 0))],
        out_specs=(pl.BlockSpec((1, tm, d), own), pl.BlockSpec((1, tm, d), own)),
        compiler_params=_cparams(("arbitrary", "arbitrary")),
        name="out_proj",
    )(a, ob, oc, x, w_bf16, mod, g2.reshape(1, d))


def _ffn_kernel(h_ref, x_ref, wg_ref, wu_ref, wd_ref, modb_ref, modc_ref, o_ref, *, tm, lc):
    f = pl.program_id(2)

    @pl.when(f == 0)
    def _():
        o_ref[...] = jnp.zeros(o_ref.shape, F32)

    h = h_ref[0]
    g = jnp.dot(h, wg_ref[...], preferred_element_type=F32)
    u = jnp.dot(h, wu_ref[...], preferred_element_type=F32)
    act = (g * _sigmoid(g) * u).astype(BF16)
    o_ref[0] += jnp.dot(act, wd_ref[...], preferred_element_type=F32)

    @pl.when(f == pl.num_programs(2) - 1)
    def _():
        rows = pl.program_id(1) * tm + lax.broadcasted_iota(jnp.int32, (tm, 1), 0)
        gate = jnp.where(rows < lc, modc_ref[0, 5:6, :], modb_ref[0, 5:6, :])
        o_ref[0] = x_ref[0] + gate * o_ref[0]


def _ffn_dense(h, x, wg, wu, wd, mod, lc, tm, tf):
    b, s, d = x.shape
    dff = wg.shape[1]
    n_mod = mod.shape[0]
    row = lambda bi, j, f: (bi, j, 0)
    return pl.pallas_call(
        functools.partial(_ffn_kernel, tm=tm, lc=lc),
        out_shape=jax.ShapeDtypeStruct((b, s, d), F32),
        grid=(b, s // tm, dff // tf),
        in_specs=[pl.BlockSpec((1, tm, d), row), pl.BlockSpec((1, tm, d), row),
                  pl.BlockSpec((d, tf), lambda bi, j, f: (0, f)),
                  pl.BlockSpec((d, tf), lambda bi, j, f: (0, f)),
                  pl.BlockSpec((tf, d), lambda bi, j, f: (f, 0)),
                  pl.BlockSpec((1, 6, d), lambda bi, j, f: (bi, 0, 0)),
                  pl.BlockSpec((1, 6, d), lambda bi, j, f: (n_mod - 1, 0, 0))],
        out_specs=pl.BlockSpec((1, tm, d), row),
        compiler_params=_cparams(("arbitrary", "arbitrary", "arbitrary")),
        name="ffn_dense",
    )(h, x, wg, wu, wd, mod, mod)


def _router_kernel(h_ref, w_ref, idx_ref, wt_ref):
    logits = jnp.dot(h_ref[...], w_ref[...], preferred_element_type=F32, precision=lax.Precision.HIGHEST)
    n_e = logits.shape[1]
    lane = lax.broadcasted_iota(jnp.int32, logits.shape, 1)
    m1 = jnp.max(logits, axis=1, keepdims=True)
    i1 = jnp.min(jnp.where(logits == m1, lane, n_e), axis=1, keepdims=True)
    rest = jnp.where(lane == i1, -jnp.inf, logits)
    m2 = jnp.max(rest, axis=1, keepdims=True)
    i2 = jnp.min(jnp.where(rest == m2, lane, n_e), axis=1, keepdims=True)
    e2 = jnp.exp(m2 - m1)
    den = 1.0 + e2
    out_lane = lax.broadcasted_iota(jnp.int32, idx_ref.shape, 1)
    idx_ref[...] = jnp.where(out_lane == 0, i1, jnp.where(out_lane == 1, i2, 0))
    wt_ref[...] = jnp.where(out_lane == 0, 1.0 / den, jnp.where(out_lane == 1, e2 / den, 0.0))


def _router(h_flat, w_router, tm):
    t, d = h_flat.shape
    n_e = w_router.shape[1]
    return pl.pallas_call(
        _router_kernel,
        out_shape=(jax.ShapeDtypeStruct((t, LANES), jnp.int32), jax.ShapeDtypeStruct((t, LANES), F32)),
        grid=(t // tm,),
        in_specs=[pl.BlockSpec((tm, d), lambda i: (i, 0)), pl.BlockSpec((d, n_e), lambda i: (0, 0))],
        out_specs=(pl.BlockSpec((tm, LANES), lambda i: (i, 0)), pl.BlockSpec((tm, LANES), lambda i: (i, 0))),
        compiler_params=_cparams(("arbitrary",)),
        name="router",
    )(h_flat, w_router)


def _row_copy(src_hbm, row, dst, r, sem):
    return pltpu.make_async_copy(src_hbm.at[pl.ds(row, 1)], dst.at[pl.ds(r, 1)], sem)


def _start_rows(src_hbm, idx_ref, dst, sem, n):
    def body(r, carry):
        _row_copy(src_hbm, idx_ref[0, 0, r], dst, r, sem).start()
        return carry

    lax.fori_loop(0, n, body, 0, unroll=8)


def _wait_rows(src_hbm, dst, sem, n):
    def body(r, carry):
        _row_copy(src_hbm, 0, dst, r, sem).wait()
        return carry

    lax.fori_loop(0, n, body, 0, unroll=8)


def _gather_kernel(nact_ref, tok_ref, tok_next_ref, h_hbm, o_ref, buf, sem, *, tm):
    i = pl.program_id(0)
    nact = nact_ref[0]
    slot = lax.rem(i, 2)

    @pl.when(jnp.logical_and(i == 0, nact > 0))
    def _():
        _start_rows(h_hbm, tok_ref, buf.at[0], sem.at[0], tm)

    @pl.when(i + 1 < nact)
    def _():
        _start_rows(h_hbm, tok_next_ref, buf.at[1 - slot], sem.at[1 - slot], tm)

    @pl.when(i < nact)
    def _():
        _wait_rows(h_hbm, buf.at[slot], sem.at[slot], tm)
        o_ref[...] = buf[slot].astype(o_ref.dtype)

    @pl.when(i >= nact)
    def _():
        o_ref[...] = jnp.zeros(o_ref.shape, o_ref.dtype)


def _gather_rows(h_flat, tok, n_active, tm):
    n_tiles = tok.shape[0]
    d = h_flat.shape[1]
    grid_spec = pltpu.PrefetchScalarGridSpec(
        num_scalar_prefetch=1,
        grid=(n_tiles,),
        in_specs=[pl.BlockSpec((1, 1, tm), lambda i, na: (i, 0, 0), memory_space=pltpu.SMEM),
                  pl.BlockSpec((1, 1, tm), lambda i, na: (jnp.minimum(i + 1, n_tiles - 1), 0, 0),
                               memory_space=pltpu.SMEM),
                  pl.BlockSpec(memory_space=pl.ANY)],
        out_specs=pl.BlockSpec((tm, d), lambda i, na: (i, 0)),
        scratch_shapes=[pltpu.VMEM((2, tm, d), h_flat.dtype), pltpu.SemaphoreType.DMA((2,))],
    )
    return pl.pallas_call(
        functools.partial(_gather_kernel, tm=tm),
        out_shape=jax.ShapeDtypeStruct((n_tiles * tm, d), BF16),
        grid_spec=grid_spec,
        compiler_params=_cparams(("arbitrary",)),
        name="moe_gather",
    )(n_active, tok, tok, h_flat)


def _moe_ffn_kernel(te_ref, nact_ref, x_ref, wg_ref, wu_ref, wd_ref, o_ref):
    i = pl.program_id(0)
    f = pl.program_id(1)

    @pl.when(f == 0)
    def _():
        o_ref[...] = jnp.zeros(o_ref.shape, o_ref.dtype)

    @pl.when(i < nact_ref[0])
    def _():
        x = x_ref[...]
        g = jnp.dot(x, wg_ref[0].astype(BF16), preferred_element_type=F32)
        u = jnp.dot(x, wu_ref[0].astype(BF16), preferred_element_type=F32)
        act = (g * _sigmoid(g) * u).astype(BF16)
        o_ref[...] += jnp.dot(act, wd_ref[0].astype(BF16), preferred_element_type=F32)


def _moe_ffn(xs, wg, wu, wd, tile_expert, n_active, tm, tf):
    p, d = xs.shape
    dff = wg.shape[2]
    nf = dff // tf
    n_tiles = p // tm

    def tile(i, f, te, na):
        return (jnp.minimum(i, na[0] - 1), 0)

    def fcol(i, f, te, na):
        return jnp.where(i < na[0], f, nf - 1)

    grid_spec = pltpu.PrefetchScalarGridSpec(
        num_scalar_prefetch=2,
        grid=(n_tiles, nf),
        in_specs=[pl.BlockSpec((tm, d), tile),
                  pl.BlockSpec((1, d, tf), lambda i, f, te, na: (te[i], 0, fcol(i, f, te, na))),
                  pl.BlockSpec((1, d, tf), lambda i, f, te, na: (te[i], 0, fcol(i, f, te, na))),
                  pl.BlockSpec((1, tf, d), lambda i, f, te, na: (te[i], fcol(i, f, te, na), 0))],
        out_specs=pl.BlockSpec((tm, d), lambda i, f, te, na: (i, 0)),
    )
    return pl.pallas_call(
        _moe_ffn_kernel,
        out_shape=jax.ShapeDtypeStruct((p, d), F32),
        grid_spec=grid_spec,
        compiler_params=_cparams(("arbitrary", "arbitrary")),
        name="moe_ffn",
    )(tile_expert, n_active, xs, wg, wu, wd)


def _combine_kernel(p0_ref, p1_ref, p0n_ref, p1n_ref, y_hbm, x_ref, wt_ref, mod_ref, g_ref, o_ref, buf, sem, *, tm):
    t = pl.program_id(0) * pl.num_programs(1) + pl.program_id(1)
    n_tiles = pl.num_programs(0) * pl.num_programs(1)
    slot = lax.rem(t, 2)

    def start(r0, r1, s):
        _start_rows(y_hbm, r0, buf.at[s, 0], sem.at[s, 0], tm)
        _start_rows(y_hbm, r1, buf.at[s, 1], sem.at[s, 1], tm)

    @pl.when(t == 0)
    def _():
        start(p0_ref, p1_ref, 0)

    @pl.when(t + 1 < n_tiles)
    def _():
        start(p0n_ref, p1n_ref, 1 - slot)

    _wait_rows(y_hbm, buf.at[slot, 0], sem.at[slot, 0], tm)
    _wait_rows(y_hbm, buf.at[slot, 1], sem.at[slot, 1], tm)
    wt = wt_ref[...]
    f = wt[:, 0:1] * buf[slot, 0] + wt[:, 1:2] * buf[slot, 1]
    xn = x_ref[0] + mod_ref[0, 5:6, :] * f
    y = xn * lax.rsqrt(jnp.mean(xn * xn, axis=-1, keepdims=True) + NORM_EPS)
    o_ref[0] = y * g_ref[...]


def _combine(y, pos0, pos1, x, wts, mod, g_final, tm):
    b, n, d = x.shape
    nt = n // tm
    last = b * nt - 1
    idx = lambda f: pl.BlockSpec((1, 1, tm), lambda bi, j: (f(bi * nt + j), 0, 0), memory_space=pltpu.SMEM)
    cur = lambda t: t
    nxt = lambda t: jnp.minimum(t + 1, last)
    return pl.pallas_call(
        functools.partial(_combine_kernel, tm=tm),
        out_shape=jax.ShapeDtypeStruct((b, n, d), F32),
        grid=(b, nt),
        in_specs=[idx(cur), idx(cur), idx(nxt), idx(nxt),
                  pl.BlockSpec(memory_space=pl.ANY),
                  pl.BlockSpec((1, tm, d), lambda bi, j: (bi, j, 0)),
                  pl.BlockSpec((tm, LANES), lambda bi, j: (bi * nt + j, 0)),
                  pl.BlockSpec((1, 6, d), lambda bi, j: (bi, 0, 0)),
                  pl.BlockSpec((1, d), lambda bi, j: (0, 0))],
        out_specs=pl.BlockSpec((1, tm, d), lambda bi, j: (bi, j, 0)),
        scratch_shapes=[pltpu.VMEM((2, 2, tm, d), F32), pltpu.SemaphoreType.DMA((2, 2))],
        compiler_params=_cparams(("arbitrary", "arbitrary")),
        name="moe_combine",
    )(pos0, pos1, pos0, pos1, y, x, wts, mod, g_final.reshape(1, d))


def _dispatch_plan(idx, n_experts, tm):
    t = idx.shape[0]
    e_flat = idx.reshape(-1)
    onehot = (e_flat[:, None] == jnp.arange(n_experts, dtype=jnp.int32)[None, :]).astype(jnp.int32)
    rank = jnp.sum((jnp.cumsum(onehot, axis=0) - 1) * onehot, axis=1)
    counts = jnp.sum(onehot, axis=0)
    padded = ((counts + tm - 1) // tm) * tm
    ends = jnp.cumsum(padded)
    starts = ends - padded
    dest = starts[e_flat] + rank
    n_tiles = (TOP_K * t) // tm + n_experts
    tok = jnp.zeros((n_tiles * tm,), jnp.int32).at[dest].set(jnp.arange(TOP_K * t, dtype=jnp.int32) // TOP_K)
    tile_row0 = jnp.arange(n_tiles, dtype=jnp.int32) * tm
    tile_expert = jnp.sum((ends[None, :] <= tile_row0[:, None]).astype(jnp.int32), axis=1)
    tile_expert = jnp.minimum(tile_expert, n_experts - 1)
    n_active = (ends[-1] // tm).astype(jnp.int32).reshape(1)
    last = jnp.maximum(n_active[0] - 1, 0)
    tile_expert = jnp.where(jnp.arange(n_tiles) < n_active[0], tile_expert, tile_expert[last])
    pos = dest.reshape(t, TOP_K)
    return tok.reshape(n_tiles, 1, tm), tile_expert, n_active, pos[:, 0], pos[:, 1]


def _rope_tables(lc, n):
    rows = n // GRID_W
    row = jnp.repeat(jnp.arange(rows), GRID_W).astype(F32)
    col = jnp.tile(jnp.arange(GRID_W), rows).astype(F32)
    axis_dim = HEAD_DIM // 2
    inv = ROPE_THETA ** (-jnp.arange(axis_dim // 2, dtype=F32) / (axis_dim // 2))
    ang_r = row[:, None] * inv[None, :]
    ang_c = col[:, None] * inv[None, :]
    ang = jnp.concatenate([ang_r, ang_r, ang_c, ang_c], axis=-1)
    cos, sin = jnp.cos(ang), jnp.sin(ang)
    first_half = (jnp.arange(HEAD_DIM) % (HEAD_DIM // 2)) < (HEAD_DIM // 4)
    sa = jnp.where(first_half[None, :], -sin, 0.0)
    sb = jnp.where(first_half[None, :], 0.0, sin)
    ident = lambda v: jnp.full((lc, HEAD_DIM), v, F32)
    full = lambda ctx_v, t: jnp.tile(jnp.concatenate([ident(ctx_v), t], axis=0), (1, LANES // HEAD_DIM))
    return full(1.0, cos), full(0.0, sa), full(0.0, sb)


def _pick_tile(n, candidates):
    for c in candidates:
        if n % c == 0:
            return c
    raise ValueError(f"no tile size among {candidates} divides {n}")


def kernel(x, c, ctx, c_ctx, w_ada, b_ada, g_norm1, g_norm2, w_in, w_dw, b_dw, ln_g, ln_b, lam_q1, lam_k1, lam_q2, lam_k2, g_subln, sink, w_out, w_ff_gate, w_ff_up, w_ff_down, w_router, w_ex_gate, w_ex_up, w_ex_down, g_final):
    b, n, d = x.shape
    lc = ctx.shape[1]
    s = lc + n
    depth = w_ada.shape[0]
    conv_ch = w_dw.shape[-1]
    n_swa_q = sink.shape[1] * HEAD_DIM
    n_diff_v = d - conv_ch - n_swa_q
    n_kv = n_swa_q // 3
    sizes = (conv_ch, n_diff_v, n_swa_q, n_diff_v, n_diff_v, n_kv, n_kv)
    assert sum(sizes) + conv_ch == w_in.shape[2]
    n_experts = w_router.shape[2]

    tm = 256
    assert lc % tm == 0 and n % tm == 0
    tm_ffn = _pick_tile(s, (768, 512, 256))
    tf_dense = _pick_tile(w_ff_gate.shape[2], (512, 256, 128))
    tf_moe = _pick_tile(w_ex_gate.shape[3], (256, 128))
    tm_moe = 1024

    xs = jnp.concatenate([ctx, x], axis=1)
    cvec = jnp.concatenate([c, c_ctx[None, :]], axis=0)
    mods = _ada(cvec, w_ada, b_ada).reshape(depth, b + 1, 6, d)
    tabs = _rope_tables(lc, n)

    out = None
    for l in range(depth):
        last = l == depth - 1
        lam_init = 0.8 - 0.6 * math.exp(-0.3 * l)
        mod = mods[l]
        lam_params = jnp.stack([lam_q1[l], lam_k1[l], lam_q2[l], lam_k2[l]], axis=0)
        y, qb, qc, kb, vb, kc, vc = _in_proj(xs, g_norm1[l], mod, w_in[l].astype(BF16), tabs, sizes, lc, tm)
        a = _conv_module(y, w_dw[l], b_dw[l], ln_g[l], ln_b[l], lc, tm)
        ob = _diff_attention(qb, kb, vb, lam_params, g_subln[l], lam_init, lc, not last, tq=256, unroll=3)
        oc = _swa(qc, kc, vc, sink[l], lc, not last, tq=WINDOW)
        xs, h2 = _out_proj(a, ob, oc, xs, w_out[l].astype(BF16), mod, g_norm2[l], lc, not last, tm,
                           F32 if l % 2 == 1 else BF16)
        i = l // 2
        if l % 2 == 0:
            assert not last, "a final dense layer would need its own final-norm epilogue"
            xs = _ffn_dense(h2, xs, w_ff_gate[i].astype(BF16), w_ff_up[i].astype(BF16),
                            w_ff_down[i].astype(BF16), mod, lc, tm_ffn, tf_dense)
        else:
            assert last, "the MoE layer carries the final norm and drops the context rows"
            h_flat = h2.reshape(b * n, d)
            idx, wts = _router(h_flat, w_router[i], tm)
            tok, tile_expert, n_active, pos0, pos1 = _dispatch_plan(idx[:, :TOP_K], n_experts, tm_moe)
            xg = _gather_rows(h_flat, tok, n_active, tm_moe)
            yexp = _moe_ffn(xg, w_ex_gate[i], w_ex_up[i], w_ex_down[i], tile_expert, n_active, tm_moe, tf_moe)
            nt = (b * n) // tm
            out = _combine(yexp, pos0.reshape(nt, 1, tm), pos1.reshape(nt, 1, tm), xs, wts, mod, g_final, tm)
    return out
```

```python
import functools
import math

import jax
import jax.numpy as jnp
from jax import lax
from jax.experimental import pallas as pl
from jax.experimental.pallas import tpu as pltpu

F32 = jnp.float32
BF16 = jnp.bfloat16

HEAD_DIM = 64
GRID_W = 64
ROPE_THETA = 10000.0
NORM_EPS = 1e-6
MASK_VALUE = -1e30
LOG2E = math.log2(math.e)
CONV_K = 31
CONV_HALO = 16
WINDOW = 128
TOP_K = 2
LANES = 128
SUBLANES = 8
ONES_ROWS = 16
VMEM_LIMIT = 56 * 1024 * 1024


def _cparams(sem):
    return pltpu.CompilerParams(dimension_semantics=sem, vmem_limit_bytes=VMEM_LIMIT)


def _rms_mod(x, g, shift, scale):
    y = x * lax.rsqrt(jnp.mean(x * x, axis=-1, keepdims=True) + NORM_EPS)
    return (y * g) * (1.0 + scale) + shift


def _sigmoid(x):
    return 1.0 / (1.0 + jnp.exp(-x))


def _ada_kernel(c_ref, w_ref, b_ref, o_ref):
    c = c_ref[...]
    s = (c * _sigmoid(c)).astype(BF16)
    o_ref[0] = jnp.dot(s, w_ref[0].astype(BF16), preferred_element_type=F32) + b_ref[0]


def _ada(cvec, w_ada, b_ada):
    depth, d, n6 = w_ada.shape
    rows = cvec.shape[0]
    tn = 1024
    return pl.pallas_call(
        _ada_kernel,
        out_shape=jax.ShapeDtypeStruct((depth, rows, n6), F32),
        grid=(depth, n6 // tn),
        in_specs=[pl.BlockSpec((rows, d), lambda l, j: (0, 0)),
                  pl.BlockSpec((1, d, tn), lambda l, j: (l, 0, j)),
                  pl.BlockSpec((1, 1, tn), lambda l, j: (l, 0, j))],
        out_specs=pl.BlockSpec((1, rows, tn), lambda l, j: (l, 0, j)),
        compiler_params=_cparams(("arbitrary", "arbitrary")),
        name="ada",
    )(cvec, w_ada, b_ada.reshape(depth, 1, n6))


def _in_kernel(xc_ref, xl_ref, g_ref, mod_ref, w_ref, cos_ref, sa_ref, sb_ref,
               y_ref, qb_ref, qc_ref, kb_ref, vb_ref, kc_ref, vc_ref, *, sizes, nctx):
    conv_ch, n_bq, n_cq, n_bk, n_bv, n_ck, n_cv = sizes
    x = jnp.where(pl.program_id(1) < nctx, xc_ref[0], xl_ref[0])
    hb = _rms_mod(x, g_ref[...], mod_ref[0, 0:1, :], mod_ref[0, 1:2, :]).astype(BF16)

    def mm(lo, width):
        return jnp.dot(hb, w_ref[:, lo:lo + width], preferred_element_type=F32)

    cos, sa, sb = cos_ref[...], sa_ref[...], sb_ref[...]

    def rope_store(u, out_ref, scale, transposed=False):
        for gi in range(u.shape[1] // LANES):
            ug = u[:, gi * LANES:(gi + 1) * LANES]
            r = (ug * cos + pltpu.roll(ug, LANES - 16, 1) * sa + pltpu.roll(ug, 16, 1) * sb) * scale
            if transposed:
                out_ref[0, gi * LANES:(gi + 1) * LANES, :] = r.T.astype(out_ref.dtype)
            else:
                out_ref[0, :, gi * LANES:(gi + 1) * LANES] = r.astype(out_ref.dtype)

    off = 0
    u = mm(off, 2 * conv_ch)
    y_ref[0] = u[:, :conv_ch] * _sigmoid(u[:, conv_ch:])
    off += 2 * conv_ch
    qscale = HEAD_DIM ** -0.5
    rope_store(mm(off, n_bq), qb_ref, qscale * LOG2E, transposed=True)
    off += n_bq
    rope_store(mm(off, n_cq), qc_ref, qscale * LOG2E, transposed=True)
    off += n_cq
    rope_store(mm(off, n_bk), kb_ref, 1.0)
    off += n_bk
    v = mm(off, n_bv)
    for gi in range(n_bv // LANES):
        vb_ref[0, gi, 0, 0:LANES, :] = v[:, gi * LANES:(gi + 1) * LANES].T.astype(BF16)
        vb_ref[0, gi, 0, LANES:LANES + ONES_ROWS, :] = jnp.ones((ONES_ROWS, v.shape[0]), BF16)
    off += n_bv
    rope_store(mm(off, n_ck), kc_ref, 1.0)
    off += n_ck
    v = mm(off, n_cv)
    grp = LANES + ONES_ROWS
    for gi in range(n_cv // LANES):
        vc_ref[0, gi * grp:gi * grp + LANES, :] = v[:, gi * LANES:(gi + 1) * LANES].T.astype(BF16)
        vc_ref[0, gi * grp + LANES:(gi + 1) * grp, :] = jnp.ones((ONES_ROWS, v.shape[0]), BF16)


def _stream_specs(parts, lc, tm, q_off=0):
    nctx = lc // tm
    if len(parts) == 1:
        arrs, lat_off = (parts[0], parts[0]), nctx
    else:
        arrs, lat_off = parts, 0
    d = arrs[0].shape[2]
    ctx_spec = pl.BlockSpec((1, tm, d), lambda bi, j: (bi, jnp.minimum(j + q_off, nctx - 1), 0))
    lat_spec = pl.BlockSpec((1, tm, d), lambda bi, j: (bi, jnp.maximum(j + q_off - nctx, 0) + lat_off, 0))
    return arrs, [ctx_spec, lat_spec]


def _in_proj(parts, g, mod, w_bf16, tabs, sizes, lc, tm):
    b, d = parts[0].shape[0], parts[0].shape[2]
    s = sum(p.shape[1] for p in parts)
    conv_ch, n_bq, n_cq, n_bk, n_bv, n_ck, n_cv = sizes
    d_in = w_bf16.shape[1]
    nctx = lc // tm
    n_mod = mod.shape[0]

    def row(bi, j):
        return (bi, j, 0)

    def modrow(bi, j):
        return (jnp.where(j < nctx, n_mod - 1, bi), 0, 0)

    widths = (conv_ch, n_bq, n_cq, n_bk, n_bv, n_ck, n_cv)
    dts = (F32, BF16, BF16, BF16, BF16, BF16, BF16)
    out_shape = [jax.ShapeDtypeStruct((b, s, w), dt) for w, dt in zip(widths, dts)]
    out_specs = [pl.BlockSpec((1, tm, w), row) for w in widths]
    n_vh = n_bv // LANES
    out_shape[1] = jax.ShapeDtypeStruct((b, n_bq, s), BF16)
    out_specs[1] = pl.BlockSpec((1, n_bq, tm), lambda bi, j: (bi, 0, j))
    out_shape[4] = jax.ShapeDtypeStruct((b, n_vh, s // tm, LANES + ONES_ROWS, tm), BF16)
    out_specs[4] = pl.BlockSpec((1, n_vh, 1, LANES + ONES_ROWS, tm), lambda bi, j: (bi, 0, j, 0, 0))
    for idx, wdt in ((2, n_cq), (6, (n_cv // LANES) * (LANES + ONES_ROWS))):
        out_shape[idx] = jax.ShapeDtypeStruct((b, wdt, s), BF16)
        out_specs[idx] = pl.BlockSpec((1, wdt, tm), lambda bi, j: (bi, 0, j))
    tab_spec = pl.BlockSpec((tm, LANES), lambda bi, j: (j, 0))
    stream, stream_specs = _stream_specs(parts, lc, tm)
    return pl.pallas_call(
        functools.partial(_in_kernel, sizes=sizes, nctx=nctx),
        out_shape=out_shape,
        grid=(b, s // tm),
        in_specs=stream_specs + [
                  pl.BlockSpec((1, d), lambda bi, j: (0, 0)),
                  pl.BlockSpec((1, 6, d), modrow),
                  pl.BlockSpec((d, d_in), lambda bi, j: (0, 0)),
                  tab_spec, tab_spec, tab_spec],
        out_specs=out_specs,
        compiler_params=_cparams(("arbitrary", "arbitrary")),
        name="in_proj",
    )(*stream, g.reshape(1, d), mod, w_bf16, *tabs)


def _conv_kernel(yp_ref, yc_ref, yn_ref, w_ref, b_ref, g_ref, bb_ref, o_ref, win_ref, *, tm, seg_tile, sub):
    j = pl.program_id(1)
    nj = pl.num_programs(1)
    has_prev = jnp.logical_and(j > 0, j != seg_tile)
    has_next = jnp.logical_and(j + 1 < nj, j + 1 != seg_tile)
    h = CONV_HALO
    win_ref[0, 0:h, :] = jnp.where(has_prev, yp_ref[0], 0.0)
    win_ref[0, h:h + tm, :] = yc_ref[0]
    win_ref[0, h + tm:h + tm + h, :] = jnp.where(has_next, yn_ref[0], 0.0)
    span = tm + 2 * h - SUBLANES
    for r in range(1, SUBLANES):
        win_ref[r, 0:span, :] = win_ref[0, pl.ds(r, span), :]
    base = h - CONV_K // 2
    for r0 in range(0, tm, sub):
        acc = jnp.zeros((sub, win_ref.shape[2]), F32)
        for t in range(CONV_K):
            o = base + r0 + t
            acc = acc + w_ref[t:t + 1, :] * win_ref[o % SUBLANES, o - o % SUBLANES:o - o % SUBLANES + sub, :]
        y = acc + b_ref[...]
        mu = jnp.mean(y, axis=-1, keepdims=True)
        yc = y - mu
        var = jnp.mean(yc * yc, axis=-1, keepdims=True)
        z = yc * lax.rsqrt(var + NORM_EPS) * g_ref[...] + bb_ref[...]
        o_ref[0, r0:r0 + sub, :] = (z * _sigmoid(z)).astype(o_ref.dtype)


def _conv_module(y, w_dw, b_dw, ln_g, ln_b, lc, tm):
    b, s, ch = y.shape
    h = CONV_HALO
    per = tm // h
    nblk = s // h

    def prev(bi, j):
        return (bi, jnp.maximum(j * per - 1, 0), 0)

    def nxt(bi, j):
        return (bi, jnp.minimum((j + 1) * per, nblk - 1), 0)

    vec = pl.BlockSpec((1, ch), lambda bi, j: (0, 0))
    return pl.pallas_call(
        functools.partial(_conv_kernel, tm=tm, seg_tile=lc // tm, sub=32),
        out_shape=jax.ShapeDtypeStruct((b, s, ch), BF16),
        grid=(b, s // tm),
        in_specs=[pl.BlockSpec((1, h, ch), prev),
                  pl.BlockSpec((1, tm, ch), lambda bi, j: (bi, j, 0)),
                  pl.BlockSpec((1, h, ch), nxt),
                  pl.BlockSpec((CONV_K, ch), lambda bi, j: (0, 0)),
                  vec, vec, vec],
        out_specs=pl.BlockSpec((1, tm, ch), lambda bi, j: (bi, j, 0)),
        scratch_shapes=[pltpu.VMEM((SUBLANES, tm + 2 * h, ch), F32)],
        compiler_params=_cparams(("arbitrary", "arbitrary")),
        name="conv_module",
    )(y, y, y, w_dw.reshape(CONV_K, ch), b_dw.reshape(1, ch), ln_g.reshape(1, ch), ln_b.reshape(1, ch))


def _diff_kernel(lamp_ref, qt_ref, k_ref, vt_ref, g_ref, o_ref, m_scr, acc_scr, sa_scr, sb_scr, sc_scr,
                 *, tq, tk, unroll, q_off, n_ctx_qblk, ctx_chunks, all_chunks, lam_init):
    i = pl.program_id(2) + q_off
    qt = qt_ref[0]
    sub = lax.broadcasted_iota(jnp.int32, qt.shape, 0)
    zero = jnp.zeros_like(qt)
    rhs = jnp.concatenate([jnp.where(sub < HEAD_DIM, qt, zero), jnp.where(sub >= HEAD_DIM, qt, zero)], axis=1)
    m_scr[...] = jnp.full(m_scr.shape, MASK_VALUE, F32)
    acc_scr[...] = jnp.zeros(acc_scr.shape, F32)

    def scores(gidx, count, s_ref):
        mxs = []
        for half in range(2):
            cs = slice(half * tq, (half + 1) * tq)
            mx = None
            for u in range(count):
                start = pl.multiple_of((gidx * count + u) * tk, tk)
                s = jnp.dot(k_ref[0, pl.ds(start, tk), :], rhs[:, cs], preferred_element_type=F32)
                s_ref[u * tk:(u + 1) * tk, cs] = s
                cm = jnp.max(s, axis=0, keepdims=True)
                mx = cm if mx is None else jnp.maximum(mx, cm)
            mxs.append(mx)
        return mxs

    def accumulate(gidx, count, s_ref, mxs):
        for half in range(2):
            cs = slice(half * tq, (half + 1) * tq)
            m_prev = m_scr[:, cs]
            m_new = jnp.maximum(m_prev, mxs[half])
            alpha = jnp.exp2(m_prev - m_new)
            pv = None
            for u in range(count):
                p = jnp.exp2((s_ref[u * tk:(u + 1) * tk, cs] - m_new).astype(BF16))
                d = jnp.dot(vt_ref[0, 0, gidx * count + u], p, preferred_element_type=F32)
                pv = d if pv is None else pv + d
            acc_scr[:, cs] = alpha * acc_scr[:, cs] + pv
            m_scr[:, cs] = m_new

    bufs = (sa_scr, sb_scr, sc_scr)

    def context_keys_only():
        for c in range(ctx_chunks):
            accumulate(c, 1, sa_scr, scores(c, 1, sa_scr))

    def all_keys():
        n_groups = all_chunks // unroll
        mx = [scores(g, unroll, bufs[g % 3]) for g in range(min(2, n_groups))]
        for g in range(n_groups):
            accumulate(g, unroll, bufs[g % 3], mx[g])
            if g + 2 < n_groups:
                mx.append(scores(g + 2, unroll, bufs[(g + 2) % 3]))

    if q_off >= n_ctx_qblk:
        all_keys()
    else:
        pl.when(i < n_ctx_qblk)(context_keys_only)
        pl.when(i >= n_ctx_qblk)(all_keys)

    vdim = 2 * HEAD_DIM
    o = acc_scr[0:vdim, :] / acc_scr[vdim:vdim + 1, :]
    lp = lamp_ref[...]
    lam = (jnp.exp(jnp.sum(lp[0:1] * lp[1:2], axis=1, keepdims=True))
           - jnp.exp(jnp.sum(lp[2:3] * lp[3:4], axis=1, keepdims=True)) + lam_init)
    od = o[:, :tq] - lam * o[:, tq:]
    y = od * lax.rsqrt(jnp.mean(od * od, axis=0, keepdims=True) + NORM_EPS)
    o_ref[0] = (y * g_ref[...] * (1.0 - lam_init)).T.astype(o_ref.dtype)


def _diff_attention(qbt, kb, vbt, lam_params, g_subln, lam_init, lc, with_ctx, tq, unroll):
    b, s, w = kb.shape
    heads = w // (2 * HEAD_DIM)
    vrows, tk = vbt.shape[3], vbt.shape[4]
    assert vrows == 2 * HEAD_DIM + ONES_ROWS
    q_off = 0 if with_ctx else lc // tq
    nq = s // tq - q_off
    assert (s // tk) % unroll == 0 and lc % tk == 0
    kern = functools.partial(_diff_kernel, tq=tq, tk=tk, unroll=unroll, q_off=q_off, n_ctx_qblk=lc // tq,
                             ctx_chunks=lc // tk, all_chunks=s // tk, lam_init=lam_init)
    return pl.pallas_call(
        kern,
        out_shape=jax.ShapeDtypeStruct((b, nq * tq, w), BF16),
        grid=(b, heads, nq),
        in_specs=[pl.BlockSpec((4, HEAD_DIM), lambda bi, h, i: (0, 0)),
                  pl.BlockSpec((1, 2 * HEAD_DIM, tq), lambda bi, h, i: (bi, h, i + q_off)),
                  pl.BlockSpec((1, s, 2 * HEAD_DIM), lambda bi, h, i: (bi, 0, h)),
                  pl.BlockSpec((1, 1, s // tk, vrows, tk), lambda bi, h, i: (bi, h, 0, 0, 0)),
                  pl.BlockSpec((2 * HEAD_DIM, 1), lambda bi, h, i: (0, 0))],
        out_specs=pl.BlockSpec((1, tq, 2 * HEAD_DIM), lambda bi, h, i: (bi, i, h)),
        scratch_shapes=[pltpu.VMEM((1, 2 * tq), F32),
                        pltpu.VMEM((vrows, 2 * tq), F32),
                        pltpu.VMEM((unroll * tk, 2 * tq), F32), pltpu.VMEM((unroll * tk, 2 * tq), F32),
                        pltpu.VMEM((unroll * tk, 2 * tq), F32)],
        compiler_params=_cparams(("arbitrary", "arbitrary", "arbitrary")),
        name="diff_attention",
    )(lam_params, qbt, kb, vbt, g_subln.reshape(2 * HEAD_DIM, 1))


def _swa_kernel(sink_ref, qt_ref, kctx_ref, vtctx_ref, kp_ref, kc_ref, kn_ref, vtp_ref, vtc_ref, vtn_ref, o_ref,
                *, tq, q_off, n_ctx_qblk, n_lat, kv_heads, group):
    i = pl.program_id(1) + q_off
    is_lat = i >= n_ctx_qblk
    q0 = (i - n_ctx_qblk) * tq
    wq = group * tq
    kpos = q0 - tq + lax.broadcasted_iota(jnp.int32, (3 * tq, wq), 0)
    qpos = q0 + jnp.bitwise_and(lax.broadcasted_iota(jnp.int32, (3 * tq, wq), 1), tq - 1)
    valid = jnp.logical_and(jnp.abs(qpos - kpos) <= WINDOW, jnp.logical_and(kpos >= 0, kpos < n_lat))
    valid = jnp.logical_and(valid, is_lat)
    qt = qt_ref[0]
    kloc = jnp.concatenate([kp_ref[0], kc_ref[0], kn_ref[0]], axis=0)
    vtloc = jnp.concatenate([vtp_ref[0], vtc_ref[0], vtn_ref[0]], axis=1)
    kctx = kctx_ref[0]
    vtctx = vtctx_ref[0]
    zeros = jnp.zeros((HEAD_DIM, tq), qt.dtype)
    heads_out = []
    for h in range(kv_heads):
        gs = slice((h // 2) * LANES, (h // 2 + 1) * LANES)
        vs = slice((h // 2) * (LANES + ONES_ROWS), (h // 2 + 1) * (LANES + ONES_ROWS))
        upper = h % 2 == 1
        cols = []
        for gi in range(group):
            hq = h * group + gi
            qh = qt[hq * HEAD_DIM:(hq + 1) * HEAD_DIM, :]
            cols.append(jnp.concatenate([zeros, qh] if upper else [qh, zeros], axis=0))
        rhs = jnp.concatenate(cols, axis=1)
        sink = jnp.concatenate(
            [jnp.full((1, tq), sink_ref[h * group + gi] * LOG2E, F32) for gi in range(group)], axis=1)
        s_ctx = jnp.dot(kctx[:, gs], rhs, preferred_element_type=F32)
        s_loc = jnp.dot(kloc[:, gs], rhs, preferred_element_type=F32)
        s_loc = jnp.where(valid, s_loc, MASK_VALUE)
        m = jnp.maximum(jnp.maximum(jnp.max(s_ctx, axis=0, keepdims=True), jnp.max(s_loc, axis=0, keepdims=True)), sink)
        e_ctx = jnp.exp2((s_ctx - m).astype(BF16))
        e_loc = jnp.exp2((s_loc - m).astype(BF16))
        ot = (jnp.dot(vtctx[vs, :], e_ctx, preferred_element_type=F32)
              + jnp.dot(vtloc[vs, :], e_loc, preferred_element_type=F32))
        denom = ot[LANES:LANES + 1, :] + jnp.exp2(sink - m)
        ot = (ot[HEAD_DIM:LANES, :] if upper else ot[:HEAD_DIM, :]) / denom
        for gi in range(group):
            heads_out.append(ot[:, gi * tq:(gi + 1) * tq])
    for pi in range(len(heads_out) // 2):
        pair = jnp.concatenate([heads_out[2 * pi], heads_out[2 * pi + 1]], axis=0)
        o_ref[0, :, pi * LANES:(pi + 1) * LANES] = pair.T.astype(o_ref.dtype)


def _swa(qct, kc, vct, sink, lc, with_ctx, tq):
    b, s, kvw = kc.shape
    w = qct.shape[1]
    kv_heads = kvw // HEAD_DIM
    group = w // kvw
    assert tq & (tq - 1) == 0 and kv_heads % 2 == 0
    q_off = 0 if with_ctx else lc // tq
    nq = s // tq - q_off
    nblk = s // tq

    def rows(f):
        return pl.BlockSpec((1, tq, kvw), lambda bi, i: (bi, f(i + q_off), 0))

    vrows = vct.shape[1]
    assert vrows == (kvw // LANES) * (LANES + ONES_ROWS)

    def cols(f):
        return pl.BlockSpec((1, vrows, tq), lambda bi, i: (bi, 0, f(i + q_off)))

    cur = lambda i: i
    prev = lambda i: jnp.maximum(i - 1, 0)
    nxt = lambda i: jnp.minimum(i + 1, nblk - 1)
    kern = functools.partial(_swa_kernel, tq=tq, q_off=q_off, n_ctx_qblk=lc // tq, n_lat=s - lc,
                             kv_heads=kv_heads, group=group)
    return pl.pallas_call(
        kern,
        out_shape=jax.ShapeDtypeStruct((b, nq * tq, w), BF16),
        grid=(b, nq),
        in_specs=[pl.BlockSpec(memory_space=pltpu.SMEM),
                  pl.BlockSpec((1, w, tq), lambda bi, i: (bi, 0, i + q_off)),
                  pl.BlockSpec((1, lc, kvw), lambda bi, i: (bi, 0, 0)),
                  pl.BlockSpec((1, vrows, lc), lambda bi, i: (bi, 0, 0)),
                  rows(prev), rows(cur), rows(nxt), cols(prev), cols(cur), cols(nxt)],
        out_specs=pl.BlockSpec((1, tq, w), lambda bi, i: (bi, i, 0)),
        compiler_params=_cparams(("arbitrary", "arbitrary")),
        name="swa",
    )(sink, qct, kc, vct, kc, kc, kc, vct, vct, vct)


def _out_kernel(a_ref, ob_ref, oc_ref, xc_ref, xl_ref, w_ref, mod_ref, g_ref, xo_ref, h_ref, *, n_ctx_tiles):
    na, nb = a_ref.shape[2], ob_ref.shape[2]
    mix = jnp.dot(a_ref[0], w_ref[0:na, :], preferred_element_type=F32)
    mix = mix + jnp.dot(ob_ref[0], w_ref[na:na + nb, :], preferred_element_type=F32)
    mix = mix + jnp.dot(oc_ref[0], w_ref[na + nb:, :], preferred_element_type=F32)
    x = jnp.where(pl.program_id(1) < n_ctx_tiles, xc_ref[0], xl_ref[0])
    xn = x + mod_ref[0, 2:3, :] * mix
    xo_ref[0] = xn
    h_ref[0] = _rms_mod(xn, g_ref[...], mod_ref[0, 3:4, :], mod_ref[0, 4:5, :]).astype(h_ref.dtype)


def _out_proj(a, ob, oc, parts, w_bf16, mod, g2, lc, with_ctx, tm, h_dtype):
    b, d = parts[0].shape[0], parts[0].shape[2]
    s = sum(p.shape[1] for p in parts)
    nctx = lc // tm
    q_off = 0 if with_ctx else nctx
    nt = s // tm - q_off
    n_mod = mod.shape[0]
    out_rows = s - q_off * tm

    def row(bi, j):
        return (bi, j + q_off, 0)

    def modrow(bi, j):
        return (jnp.where(j + q_off < nctx, n_mod - 1, bi), 0, 0)

    def own(bi, j):
        return (bi, j, 0)

    assert ob.shape[1] == out_rows and oc.shape[1] == out_rows
    stream, stream_specs = _stream_specs(parts, lc, tm, q_off)
    return pl.pallas_call(
        functools.partial(_out_kernel, n_ctx_tiles=nctx - q_off),
        out_shape=(jax.ShapeDtypeStruct((b, out_rows, d), F32), jax.ShapeDtypeStruct((b, out_rows, d), h_dtype)),
        grid=(b, nt),
        in_specs=[pl.BlockSpec((1, tm, a.shape[2]), row), pl.BlockSpec((1, tm, ob.shape[2]), own),
                  pl.BlockSpec((1, tm, oc.shape[2]), own)] + stream_specs + [
                  pl.BlockSpec(w_bf16.shape, lambda bi, j: (0, 0)),
                  pl.BlockSpec((1, 6, d), modrow),
                  pl.BlockSpec((1, d), lambda bi, j: (0, 0))],
        out_specs=(pl.BlockSpec((1, tm, d), own), pl.BlockSpec((1, tm, d), own)),
        compiler_params=_cparams(("arbitrary", "arbitrary")),
        name="out_proj",
    )(a, ob, oc, *stream, w_bf16, mod, g2.reshape(1, d))


def _ffn_kernel(h_ref, x_ref, wg_ref, wu_ref, wd_ref, modb_ref, modc_ref, o_ref, *, tm, lc):
    f = pl.program_id(2)

    @pl.when(f == 0)
    def _():
        o_ref[...] = jnp.zeros(o_ref.shape, F32)

    h = h_ref[0]
    g = jnp.dot(h, wg_ref[...], preferred_element_type=F32)
    u = jnp.dot(h, wu_ref[...], preferred_element_type=F32)
    act = (g * _sigmoid(g) * u).astype(BF16)
    o_ref[0] += jnp.dot(act, wd_ref[...], preferred_element_type=F32)

    @pl.when(f == pl.num_programs(2) - 1)
    def _():
        rows = pl.program_id(1) * tm + lax.broadcasted_iota(jnp.int32, (tm, 1), 0)
        gate = jnp.where(rows < lc, modc_ref[0, 5:6, :], modb_ref[0, 5:6, :])
        o_ref[0] = x_ref[0] + gate * o_ref[0]


def _ffn_dense(h, x, wg, wu, wd, mod, lc, tm, tf):
    b, s, d = x.shape
    dff = wg.shape[1]
    n_mod = mod.shape[0]
    row = lambda bi, j, f: (bi, j, 0)
    return pl.pallas_call(
        functools.partial(_ffn_kernel, tm=tm, lc=lc),
        out_shape=jax.ShapeDtypeStruct((b, s, d), F32),
        grid=(b, s // tm, dff // tf),
        in_specs=[pl.BlockSpec((1, tm, d), row), pl.BlockSpec((1, tm, d), row),
                  pl.BlockSpec((d, tf), lambda bi, j, f: (0, f)),
                  pl.BlockSpec((d, tf), lambda bi, j, f: (0, f)),
                  pl.BlockSpec((tf, d), lambda bi, j, f: (f, 0)),
                  pl.BlockSpec((1, 6, d), lambda bi, j, f: (bi, 0, 0)),
                  pl.BlockSpec((1, 6, d), lambda bi, j, f: (n_mod - 1, 0, 0))],
        out_specs=pl.BlockSpec((1, tm, d), row),
        compiler_params=_cparams(("arbitrary", "arbitrary", "arbitrary")),
        name="ffn_dense",
    )(h, x, wg, wu, wd, mod, mod)


def _router_kernel(h_ref, w_ref, idx_ref, wt_ref):
    h = h_ref[...]
    w = w_ref[...]
    h_hi = h.astype(BF16)
    w_hi = w.astype(BF16)
    h_lo = (h - h_hi.astype(F32)).astype(BF16)
    w_lo = (w - w_hi.astype(F32)).astype(BF16)
    logits = (jnp.dot(h_hi, w_hi, preferred_element_type=F32) + jnp.dot(h_hi, w_lo, preferred_element_type=F32)
              + jnp.dot(h_lo, w_hi, preferred_element_type=F32))
    n_e = logits.shape[1]
    lane = lax.broadcasted_iota(jnp.int32, logits.shape, 1)
    m1 = jnp.max(logits, axis=1, keepdims=True)
    i1 = jnp.min(jnp.where(logits == m1, lane, n_e), axis=1, keepdims=True)
    rest = jnp.where(lane == i1, -jnp.inf, logits)
    m2 = jnp.max(rest, axis=1, keepdims=True)
    i2 = jnp.min(jnp.where(rest == m2, lane, n_e), axis=1, keepdims=True)
    e2 = jnp.exp(m2 - m1)
    den = 1.0 + e2
    out_lane = lax.broadcasted_iota(jnp.int32, idx_ref.shape, 1)
    idx_ref[...] = jnp.where(out_lane == 0, i1, jnp.where(out_lane == 1, i2, 0))
    wt_ref[...] = jnp.where(out_lane == 0, 1.0 / den, jnp.where(out_lane == 1, e2 / den, 0.0))


def _router(h_flat, w_router, tm):
    t, d = h_flat.shape
    n_e = w_router.shape[1]
    return pl.pallas_call(
        _router_kernel,
        out_shape=(jax.ShapeDtypeStruct((t, LANES), jnp.int32), jax.ShapeDtypeStruct((t, LANES), F32)),
        grid=(t // tm,),
        in_specs=[pl.BlockSpec((tm, d), lambda i: (i, 0)), pl.BlockSpec((d, n_e), lambda i: (0, 0))],
        out_specs=(pl.BlockSpec((tm, LANES), lambda i: (i, 0)), pl.BlockSpec((tm, LANES), lambda i: (i, 0))),
        compiler_params=_cparams(("arbitrary",)),
        name="router",
    )(h_flat, w_router)


def _row_copy(src_hbm, row, dst, r, sem):
    return pltpu.make_async_copy(src_hbm.at[pl.ds(row, 1)], dst.at[pl.ds(r, 1)], sem)


def _start_rows(src_hbm, idx_ref, dst, sem, n):
    def body(r, carry):
        _row_copy(src_hbm, idx_ref[0, 0, r], dst, r, sem).start()
        return carry

    lax.fori_loop(0, n, body, 0, unroll=8)


def _wait_rows(src_hbm, dst, sem, n):
    def body(r, carry):
        _row_copy(src_hbm, 0, dst, r, sem).wait()
        return carry

    lax.fori_loop(0, n, body, 0, unroll=8)


def _gather_kernel(nact_ref, tok_ref, tok_next_ref, h_hbm, o_ref, buf, sem, *, tm):
    i = pl.program_id(0)
    nact = nact_ref[0]
    slot = lax.rem(i, 2)

    @pl.when(jnp.logical_and(i == 0, nact > 0))
    def _():
        _start_rows(h_hbm, tok_ref, buf.at[0], sem.at[0], tm)

    @pl.when(i + 1 < nact)
    def _():
        _start_rows(h_hbm, tok_next_ref, buf.at[1 - slot], sem.at[1 - slot], tm)

    @pl.when(i < nact)
    def _():
        _wait_rows(h_hbm, buf.at[slot], sem.at[slot], tm)
        o_ref[...] = buf[slot].astype(o_ref.dtype)

    @pl.when(i >= nact)
    def _():
        o_ref[...] = jnp.zeros(o_ref.shape, o_ref.dtype)


def _gather_rows(h_flat, tok, n_active, tm):
    n_tiles = tok.shape[0]
    d = h_flat.shape[1]
    grid_spec = pltpu.PrefetchScalarGridSpec(
        num_scalar_prefetch=1,
        grid=(n_tiles,),
        in_specs=[pl.BlockSpec((1, 1, tm), lambda i, na: (i, 0, 0), memory_space=pltpu.SMEM),
                  pl.BlockSpec((1, 1, tm), lambda i, na: (jnp.minimum(i + 1, n_tiles - 1), 0, 0),
                               memory_space=pltpu.SMEM),
                  pl.BlockSpec(memory_space=pl.ANY)],
        out_specs=pl.BlockSpec((tm, d), lambda i, na: (i, 0)),
        scratch_shapes=[pltpu.VMEM((2, tm, d), h_flat.dtype), pltpu.SemaphoreType.DMA((2,))],
    )
    return pl.pallas_call(
        functools.partial(_gather_kernel, tm=tm),
        out_shape=jax.ShapeDtypeStruct((n_tiles * tm, d), BF16),
        grid_spec=grid_spec,
        compiler_params=_cparams(("arbitrary",)),
        name="moe_gather",
    )(n_active, tok, tok, h_flat)


def _moe_ffn_kernel(te_ref, rows_ref, nact_ref, x_ref, wg_ref, wu_ref, wd_ref, o_ref, *, parts):
    i = pl.program_id(0)
    f = pl.program_id(1)
    pm = o_ref.shape[0] // parts

    @pl.when(f == 0)
    def _():
        o_ref[...] = jnp.zeros(o_ref.shape, o_ref.dtype)

    wg = wg_ref[0].astype(BF16)
    wu = wu_ref[0].astype(BF16)
    wd = wd_ref[0].astype(BF16)
    for part in range(parts):
        @pl.when(rows_ref[i] > part * pm)
        def _(part=part):
            rs = slice(part * pm, (part + 1) * pm)
            x = x_ref[rs, :]
            g = jnp.dot(x, wg, preferred_element_type=F32)
            u = jnp.dot(x, wu, preferred_element_type=F32)
            act = (g * _sigmoid(g) * u).astype(BF16)
            o_ref[rs, :] += jnp.dot(act, wd, preferred_element_type=F32)


def _moe_ffn(xs, wg, wu, wd, tile_expert, tile_rows, n_active, tm, tf):
    p, d = xs.shape
    dff = wg.shape[2]
    nf = dff // tf
    n_tiles = p // tm

    def tile(i, f, te, tr, na):
        return (jnp.minimum(i, na[0] - 1), 0)

    def fcol(i, f, na):
        return jnp.where(i < na[0], f, nf - 1)

    grid_spec = pltpu.PrefetchScalarGridSpec(
        num_scalar_prefetch=3,
        grid=(n_tiles, nf),
        in_specs=[pl.BlockSpec((tm, d), tile),
                  pl.BlockSpec((1, d, tf), lambda i, f, te, tr, na: (te[i], 0, fcol(i, f, na))),
                  pl.BlockSpec((1, d, tf), lambda i, f, te, tr, na: (te[i], 0, fcol(i, f, na))),
                  pl.BlockSpec((1, tf, d), lambda i, f, te, tr, na: (te[i], fcol(i, f, na), 0))],
        out_specs=pl.BlockSpec((tm, d), lambda i, f, te, tr, na: (i, 0)),
    )
    return pl.pallas_call(
        functools.partial(_moe_ffn_kernel, parts=1),
        out_shape=jax.ShapeDtypeStruct((p, d), F32),
        grid_spec=grid_spec,
        compiler_params=_cparams(("arbitrary", "arbitrary")),
        name="moe_ffn",
    )(tile_expert, tile_rows, n_active, xs, wg, wu, wd)


def _combine_kernel(p0_ref, p1_ref, p0n_ref, p1n_ref, y_hbm, x_ref, wt_ref, mod_ref, g_ref, o_ref, buf, sem, *, tm):
    t = pl.program_id(0) * pl.num_programs(1) + pl.program_id(1)
    n_tiles = pl.num_programs(0) * pl.num_programs(1)
    slot = lax.rem(t, 2)

    def start(r0, r1, s):
        _start_rows(y_hbm, r0, buf.at[s, 0], sem.at[s, 0], tm)
        _start_rows(y_hbm, r1, buf.at[s, 1], sem.at[s, 1], tm)

    @pl.when(t == 0)
    def _():
        start(p0_ref, p1_ref, 0)

    @pl.when(t + 1 < n_tiles)
    def _():
        start(p0n_ref, p1n_ref, 1 - slot)

    _wait_rows(y_hbm, buf.at[slot, 0], sem.at[slot, 0], tm)
    _wait_rows(y_hbm, buf.at[slot, 1], sem.at[slot, 1], tm)
    wt = wt_ref[...]
    f = wt[:, 0:1] * buf[slot, 0] + wt[:, 1:2] * buf[slot, 1]
    xn = x_ref[0] + mod_ref[0, 5:6, :] * f
    y = xn * lax.rsqrt(jnp.mean(xn * xn, axis=-1, keepdims=True) + NORM_EPS)
    o_ref[0] = y * g_ref[...]


def _combine(y, pos0, pos1, x, wts, mod, g_final, tm):
    b, n, d = x.shape
    nt = n // tm
    last = b * nt - 1
    idx = lambda f: pl.BlockSpec((1, 1, tm), lambda bi, j: (f(bi * nt + j), 0, 0), memory_space=pltpu.SMEM)
    cur = lambda t: t
    nxt = lambda t: jnp.minimum(t + 1, last)
    return pl.pallas_call(
        functools.partial(_combine_kernel, tm=tm),
        out_shape=jax.ShapeDtypeStruct((b, n, d), F32),
        grid=(b, nt),
        in_specs=[idx(cur), idx(cur), idx(nxt), idx(nxt),
                  pl.BlockSpec(memory_space=pl.ANY),
                  pl.BlockSpec((1, tm, d), lambda bi, j: (bi, j, 0)),
                  pl.BlockSpec((tm, LANES), lambda bi, j: (bi * nt + j, 0)),
                  pl.BlockSpec((1, 6, d), lambda bi, j: (bi, 0, 0)),
                  pl.BlockSpec((1, d), lambda bi, j: (0, 0))],
        out_specs=pl.BlockSpec((1, tm, d), lambda bi, j: (bi, j, 0)),
        scratch_shapes=[pltpu.VMEM((2, 2, tm, d), F32), pltpu.SemaphoreType.DMA((2, 2))],
        compiler_params=_cparams(("arbitrary", "arbitrary")),
        name="moe_combine",
    )(pos0, pos1, pos0, pos1, y, x, wts, mod, g_final.reshape(1, d))


def _dispatch_plan(idx, n_experts, tm):
    t = idx.shape[0]
    e_flat = idx.reshape(-1)
    onehot = (e_flat[:, None] == jnp.arange(n_experts, dtype=jnp.int32)[None, :]).astype(jnp.int32)
    rank = jnp.sum((jnp.cumsum(onehot, axis=0) - 1) * onehot, axis=1)
    counts = jnp.sum(onehot, axis=0)
    padded = ((counts + tm - 1) // tm) * tm
    ends = jnp.cumsum(padded)
    starts = ends - padded
    dest = starts[e_flat] + rank
    n_tiles = (TOP_K * t) // tm + n_experts
    tok = jnp.zeros((n_tiles * tm,), jnp.int32).at[dest].set(jnp.arange(TOP_K * t, dtype=jnp.int32) // TOP_K)
    tile_row0 = jnp.arange(n_tiles, dtype=jnp.int32) * tm
    tile_expert = jnp.sum((ends[None, :] <= tile_row0[:, None]).astype(jnp.int32), axis=1)
    tile_expert = jnp.minimum(tile_expert, n_experts - 1)
    n_active = (ends[-1] // tm).astype(jnp.int32).reshape(1)
    active = jnp.arange(n_tiles) < n_active[0]
    tile_rows = jnp.clip((starts + counts)[tile_expert] - tile_row0, 0, tm)
    tile_rows = jnp.where(active, tile_rows, 0).astype(jnp.int32)
    last = jnp.maximum(n_active[0] - 1, 0)
    tile_expert = jnp.where(active, tile_expert, tile_expert[last])
    pos = dest.reshape(t, TOP_K)
    return tok.reshape(n_tiles, 1, tm), tile_expert, tile_rows, n_active, pos[:, 0], pos[:, 1]


def _rope_tables(lc, n):
    rows = n // GRID_W
    row = jnp.repeat(jnp.arange(rows), GRID_W).astype(F32)
    col = jnp.tile(jnp.arange(GRID_W), rows).astype(F32)
    axis_dim = HEAD_DIM // 2
    inv = ROPE_THETA ** (-jnp.arange(axis_dim // 2, dtype=F32) / (axis_dim // 2))
    ang_r = row[:, None] * inv[None, :]
    ang_c = col[:, None] * inv[None, :]
    ang = jnp.concatenate([ang_r, ang_r, ang_c, ang_c], axis=-1)
    cos, sin = jnp.cos(ang), jnp.sin(ang)
    first_half = (jnp.arange(HEAD_DIM) % (HEAD_DIM // 2)) < (HEAD_DIM // 4)
    sa = jnp.where(first_half[None, :], -sin, 0.0)
    sb = jnp.where(first_half[None, :], 0.0, sin)
    ident = lambda v: jnp.full((lc, HEAD_DIM), v, F32)
    full = lambda ctx_v, t: jnp.tile(jnp.concatenate([ident(ctx_v), t], axis=0), (1, LANES // HEAD_DIM))
    return full(1.0, cos), full(0.0, sa), full(0.0, sb)


def _pick_tile(n, candidates):
    for c in candidates:
        if n % c == 0:
            return c
    raise ValueError(f"no tile size among {candidates} divides {n}")


def kernel(x, c, ctx, c_ctx, w_ada, b_ada, g_norm1, g_norm2, w_in, w_dw, b_dw, ln_g, ln_b, lam_q1, lam_k1, lam_q2, lam_k2, g_subln, sink, w_out, w_ff_gate, w_ff_up, w_ff_down, w_router, w_ex_gate, w_ex_up, w_ex_down, g_final):
    b, n, d = x.shape
    lc = ctx.shape[1]
    s = lc + n
    depth = w_ada.shape[0]
    conv_ch = w_dw.shape[-1]
    n_swa_q = sink.shape[1] * HEAD_DIM
    n_diff_v = d - conv_ch - n_swa_q
    n_kv = n_swa_q // 3
    sizes = (conv_ch, n_diff_v, n_swa_q, n_diff_v, n_diff_v, n_kv, n_kv)
    assert sum(sizes) + conv_ch == w_in.shape[2]
    n_experts = w_router.shape[2]

    tm = 256
    assert lc % tm == 0 and n % tm == 0
    tm_ffn = _pick_tile(s, (768, 512, 256))
    tf_dense = _pick_tile(w_ff_gate.shape[2], (512, 256, 128))
    tf_moe = _pick_tile(w_ex_gate.shape[3], (256, 128))
    tm_moe = 1024

    stream = (ctx, x)
    cvec = jnp.concatenate([c, c_ctx[None, :]], axis=0)
    mods = _ada(cvec, w_ada, b_ada).reshape(depth, b + 1, 6, d)
    tabs = _rope_tables(lc, n)

    out = None
    for l in range(depth):
        last = l == depth - 1
        lam_init = 0.8 - 0.6 * math.exp(-0.3 * l)
        mod = mods[l]
        lam_params = jnp.stack([lam_q1[l], lam_k1[l], lam_q2[l], lam_k2[l]], axis=0)
        y, qb, qc, kb, vb, kc, vc = _in_proj(stream, g_norm1[l], mod, w_in[l].astype(BF16), tabs, sizes, lc, tm)
        a = _conv_module(y, w_dw[l], b_dw[l], ln_g[l], ln_b[l], lc, tm)
        ob = _diff_attention(qb, kb, vb, lam_params, g_subln[l], lam_init, lc, not last, tq=256, unroll=3)
        oc = _swa(qc, kc, vc, sink[l], lc, not last, tq=WINDOW)
        xs, h2 = _out_proj(a, ob, oc, stream, w_out[l].astype(BF16), mod, g_norm2[l], lc, not last, tm,
                           F32 if l % 2 == 1 else BF16)
        i = l // 2
        if l % 2 == 0:
            assert not last, "a final dense layer would need its own final-norm epilogue"
            xs = _ffn_dense(h2, xs, w_ff_gate[i].astype(BF16), w_ff_up[i].astype(BF16),
                            w_ff_down[i].astype(BF16), mod, lc, tm_ffn, tf_dense)
            stream = (xs,)
        else:
            assert last, "the MoE layer carries the final norm and drops the context rows"
            h_flat = h2.reshape(b * n, d)
            idx, wts = _router(h_flat, w_router[i], tm)
            tok, tile_expert, tile_rows, n_active, pos0, pos1 = _dispatch_plan(idx[:, :TOP_K], n_experts, tm_moe)
            xg = _gather_rows(h_flat, tok, n_active, tm_moe)
            yexp = _moe_ffn(xg, w_ex_gate[i], w_ex_up[i], w_ex_down[i], tile_expert, tile_rows, n_active,
                            tm_moe, tf_moe)
            nt = (b * n) // tm
            out = _combine(yexp, pos0.reshape(nt, 1, tm), pos1.reshape(nt, 1, tm), xs, wts, mod, g_final, tm)
    return out
```

```python
import functools
import math

import jax
import jax.numpy as jnp
from jax import lax
from jax.experimental import pallas as pl
from jax.experimental.pallas import tpu as pltpu

F32 = jnp.float32
BF16 = jnp.bfloat16

HEAD_DIM = 64
GRID_W = 64
ROPE_THETA = 10000.0
NORM_EPS = 1e-6
MASK_VALUE = -1e30
LOG2E = math.log2(math.e)
CONV_K = 31
CONV_HALO = 16
WINDOW = 128
TOP_K = 2
LANES = 128
SUBLANES = 8
ONES_ROWS = 16
VMEM_LIMIT = 56 * 1024 * 1024


def _cparams(sem):
    return pltpu.CompilerParams(dimension_semantics=sem, vmem_limit_bytes=VMEM_LIMIT)


def _rms_mod(x, g, shift, scale):
    y = x * lax.rsqrt(jnp.mean(x * x, axis=-1, keepdims=True) + NORM_EPS)
    return (y * g) * (1.0 + scale) + shift


def _sigmoid(x):
    return 1.0 / (1.0 + jnp.exp(-x))


def _ada_kernel(c_ref, w_ref, b_ref, o_ref):
    c = c_ref[...]
    s = (c * _sigmoid(c)).astype(BF16)
    o_ref[0] = jnp.dot(s, w_ref[0].astype(BF16), preferred_element_type=F32) + b_ref[0]


def _ada(cvec, w_ada, b_ada):
    depth, d, n6 = w_ada.shape
    rows = cvec.shape[0]
    tn = 1024
    return pl.pallas_call(
        _ada_kernel,
        out_shape=jax.ShapeDtypeStruct((depth, rows, n6), F32),
        grid=(depth, n6 // tn),
        in_specs=[pl.BlockSpec((rows, d), lambda l, j: (0, 0)),
                  pl.BlockSpec((1, d, tn), lambda l, j: (l, 0, j)),
                  pl.BlockSpec((1, 1, tn), lambda l, j: (l, 0, j))],
        out_specs=pl.BlockSpec((1, rows, tn), lambda l, j: (l, 0, j)),
        compiler_params=_cparams(("arbitrary", "arbitrary")),
        name="ada",
    )(cvec, w_ada, b_ada.reshape(depth, 1, n6))


def _in_kernel(xc_ref, xl_ref, g_ref, mod_ref, w_ref, cos_ref, sa_ref, sb_ref,
               y_ref, qb_ref, qc_ref, kb_ref, vb_ref, kc_ref, vc_ref, *, sizes, nctx):
    conv_ch, n_bq, n_cq, n_bk, n_bv, n_ck, n_cv = sizes
    x = jnp.where(pl.program_id(1) < nctx, xc_ref[0], xl_ref[0])
    hb = _rms_mod(x, g_ref[...], mod_ref[0, 0:1, :], mod_ref[0, 1:2, :]).astype(BF16)

    def mm(lo, width):
        return jnp.dot(hb, w_ref[:, lo:lo + width], preferred_element_type=F32)

    cos, sa, sb = cos_ref[...], sa_ref[...], sb_ref[...]

    def rope_store(u, out_ref, scale, transposed=False):
        for gi in range(u.shape[1] // LANES):
            ug = u[:, gi * LANES:(gi + 1) * LANES]
            r = (ug * cos + pltpu.roll(ug, LANES - 16, 1) * sa + pltpu.roll(ug, 16, 1) * sb) * scale
            if transposed:
                out_ref[0, gi * LANES:(gi + 1) * LANES, :] = r.T.astype(out_ref.dtype)
            else:
                out_ref[0, :, gi * LANES:(gi + 1) * LANES] = r.astype(out_ref.dtype)

    off = 0
    u = mm(off, 2 * conv_ch)
    y_ref[0] = u[:, :conv_ch] * _sigmoid(u[:, conv_ch:])
    off += 2 * conv_ch
    qscale = HEAD_DIM ** -0.5
    rope_store(mm(off, n_bq), qb_ref, qscale * LOG2E, transposed=True)
    off += n_bq
    rope_store(mm(off, n_cq), qc_ref, qscale * LOG2E, transposed=True)
    off += n_cq
    rope_store(mm(off, n_bk), kb_ref, 1.0)
    off += n_bk
    v = mm(off, n_bv)
    for gi in range(n_bv // LANES):
        vb_ref[0, gi, 0, 0:LANES, :] = v[:, gi * LANES:(gi + 1) * LANES].T.astype(BF16)
        vb_ref[0, gi, 0, LANES:LANES + ONES_ROWS, :] = jnp.ones((ONES_ROWS, v.shape[0]), BF16)
    off += n_bv
    rope_store(mm(off, n_ck), kc_ref, 1.0)
    off += n_ck
    v = mm(off, n_cv)
    grp = LANES + ONES_ROWS
    for gi in range(n_cv // LANES):
        vc_ref[0, gi * grp:gi * grp + LANES, :] = v[:, gi * LANES:(gi + 1) * LANES].T.astype(BF16)
        vc_ref[0, gi * grp + LANES:(gi + 1) * grp, :] = jnp.ones((ONES_ROWS, v.shape[0]), BF16)


def _stream_specs(parts, lc, tm, q_off=0):
    nctx = lc // tm
    if len(parts) == 1:
        arrs, lat_off = (parts[0], parts[0]), nctx
    else:
        arrs, lat_off = parts, 0
    d = arrs[0].shape[2]
    ctx_spec = pl.BlockSpec((1, tm, d), lambda bi, j: (bi, jnp.minimum(j + q_off, nctx - 1), 0))
    lat_spec = pl.BlockSpec((1, tm, d), lambda bi, j: (bi, jnp.maximum(j + q_off - nctx, 0) + lat_off, 0))
    return arrs, [ctx_spec, lat_spec]


def _in_proj(parts, g, mod, w_bf16, tabs, sizes, lc, tm):
    b, d = parts[0].shape[0], parts[0].shape[2]
    s = sum(p.shape[1] for p in parts)
    conv_ch, n_bq, n_cq, n_bk, n_bv, n_ck, n_cv = sizes
    d_in = w_bf16.shape[1]
    nctx = lc // tm
    n_mod = mod.shape[0]

    def row(bi, j):
        return (bi, j, 0)

    def modrow(bi, j):
        return (jnp.where(j < nctx, n_mod - 1, bi), 0, 0)

    widths = (conv_ch, n_bq, n_cq, n_bk, n_bv, n_ck, n_cv)
    dts = (F32, BF16, BF16, BF16, BF16, BF16, BF16)
    out_shape = [jax.ShapeDtypeStruct((b, s, w), dt) for w, dt in zip(widths, dts)]
    out_specs = [pl.BlockSpec((1, tm, w), row) for w in widths]
    n_vh = n_bv // LANES
    out_shape[1] = jax.ShapeDtypeStruct((b, n_bq, s), BF16)
    out_specs[1] = pl.BlockSpec((1, n_bq, tm), lambda bi, j: (bi, 0, j))
    out_shape[4] = jax.ShapeDtypeStruct((b, n_vh, s // tm, LANES + ONES_ROWS, tm), BF16)
    out_specs[4] = pl.BlockSpec((1, n_vh, 1, LANES + ONES_ROWS, tm), lambda bi, j: (bi, 0, j, 0, 0))
    for idx, wdt in ((2, n_cq), (6, (n_cv // LANES) * (LANES + ONES_ROWS))):
        out_shape[idx] = jax.ShapeDtypeStruct((b, wdt, s), BF16)
        out_specs[idx] = pl.BlockSpec((1, wdt, tm), lambda bi, j: (bi, 0, j))
    tab_spec = pl.BlockSpec((tm, LANES), lambda bi, j: (j, 0))
    stream, stream_specs = _stream_specs(parts, lc, tm)
    return pl.pallas_call(
        functools.partial(_in_kernel, sizes=sizes, nctx=nctx),
        out_shape=out_shape,
        grid=(b, s // tm),
        in_specs=stream_specs + [
                  pl.BlockSpec((1, d), lambda bi, j: (0, 0)),
                  pl.BlockSpec((1, 6, d), modrow),
                  pl.BlockSpec((d, d_in), lambda bi, j: (0, 0)),
                  tab_spec, tab_spec, tab_spec],
        out_specs=out_specs,
        compiler_params=_cparams(("arbitrary", "arbitrary")),
        name="in_proj",
    )(*stream, g.reshape(1, d), mod, w_bf16, *tabs)


def _conv_kernel(yp_ref, yc_ref, yn_ref, w_ref, b_ref, g_ref, bb_ref, o_ref, win_ref, *, tm, seg_tile, sub):
    j = pl.program_id(1)
    nj = pl.num_programs(1)
    has_prev = jnp.logical_and(j > 0, j != seg_tile)
    has_next = jnp.logical_and(j + 1 < nj, j + 1 != seg_tile)
    h = CONV_HALO
    win_ref[0, 0:h, :] = jnp.where(has_prev, yp_ref[0], 0.0)
    win_ref[0, h:h + tm, :] = yc_ref[0]
    win_ref[0, h + tm:h + tm + h, :] = jnp.where(has_next, yn_ref[0], 0.0)
    span = tm + 2 * h - SUBLANES
    for r in range(1, SUBLANES):
        win_ref[r, 0:span, :] = win_ref[0, pl.ds(r, span), :]
    base = h - CONV_K // 2
    for r0 in range(0, tm, sub):
        acc = jnp.zeros((sub, win_ref.shape[2]), F32)
        for t in range(CONV_K):
            o = base + r0 + t
            acc = acc + w_ref[t:t + 1, :] * win_ref[o % SUBLANES, o - o % SUBLANES:o - o % SUBLANES + sub, :]
        y = acc + b_ref[...]
        mu = jnp.mean(y, axis=-1, keepdims=True)
        yc = y - mu
        var = jnp.mean(yc * yc, axis=-1, keepdims=True)
        z = yc * lax.rsqrt(var + NORM_EPS) * g_ref[...] + bb_ref[...]
        o_ref[0, r0:r0 + sub, :] = (z * _sigmoid(z)).astype(o_ref.dtype)


def _conv_module(y, w_dw, b_dw, ln_g, ln_b, lc, tm):
    b, s, ch = y.shape
    h = CONV_HALO
    per = tm // h
    nblk = s // h

    def prev(bi, j):
        return (bi, jnp.maximum(j * per - 1, 0), 0)

    def nxt(bi, j):
        return (bi, jnp.minimum((j + 1) * per, nblk - 1), 0)

    vec = pl.BlockSpec((1, ch), lambda bi, j: (0, 0))
    return pl.pallas_call(
        functools.partial(_conv_kernel, tm=tm, seg_tile=lc // tm, sub=32),
        out_shape=jax.ShapeDtypeStruct((b, s, ch), BF16),
        grid=(b, s // tm),
        in_specs=[pl.BlockSpec((1, h, ch), prev),
                  pl.BlockSpec((1, tm, ch), lambda bi, j: (bi, j, 0)),
                  pl.BlockSpec((1, h, ch), nxt),
                  pl.BlockSpec((CONV_K, ch), lambda bi, j: (0, 0)),
                  vec, vec, vec],
        out_specs=pl.BlockSpec((1, tm, ch), lambda bi, j: (bi, j, 0)),
        scratch_shapes=[pltpu.VMEM((SUBLANES, tm + 2 * h, ch), F32)],
        compiler_params=_cparams(("arbitrary", "arbitrary")),
        name="conv_module",
    )(y, y, y, w_dw.reshape(CONV_K, ch), b_dw.reshape(1, ch), ln_g.reshape(1, ch), ln_b.reshape(1, ch))


def _diff_kernel(lamp_ref, qt_ref, k_ref, vt_ref, g_ref, o_ref, m_scr, acc_scr, sa_scr, sb_scr, sc_scr,
                 *, tq, tk, unroll, q_off, n_ctx_qblk, ctx_chunks, all_chunks, lam_init):
    i = pl.program_id(2) + q_off
    qt = qt_ref[0]
    sub = lax.broadcasted_iota(jnp.int32, qt.shape, 0)
    zero = jnp.zeros_like(qt)
    rhs = jnp.concatenate([jnp.where(sub < HEAD_DIM, qt, zero), jnp.where(sub >= HEAD_DIM, qt, zero)], axis=1)
    m_scr[...] = jnp.full(m_scr.shape, MASK_VALUE, F32)
    acc_scr[...] = jnp.zeros(acc_scr.shape, F32)

    def scores(gidx, count, s_ref):
        mxs = []
        for half in range(2):
            cs = slice(half * tq, (half + 1) * tq)
            mx = None
            for u in range(count):
                start = pl.multiple_of((gidx * count + u) * tk, tk)
                s = jnp.dot(k_ref[0, pl.ds(start, tk), :], rhs[:, cs], preferred_element_type=F32)
                s_ref[u * tk:(u + 1) * tk, cs] = s
                cm = jnp.max(s, axis=0, keepdims=True)
                mx = cm if mx is None else jnp.maximum(mx, cm)
            mxs.append(mx)
        return mxs

    def accumulate(gidx, count, s_ref, mxs):
        for half in range(2):
            cs = slice(half * tq, (half + 1) * tq)
            m_prev = m_scr[:, cs]
            m_new = jnp.maximum(m_prev, mxs[half])
            alpha = jnp.exp2(m_prev - m_new)
            pv = None
            for u in range(count):
                p = jnp.exp2((s_ref[u * tk:(u + 1) * tk, cs] - m_new).astype(BF16))
                d = jnp.dot(vt_ref[0, 0, gidx * count + u], p, preferred_element_type=F32)
                pv = d if pv is None else pv + d
            acc_scr[:, cs] = alpha * acc_scr[:, cs] + pv
            m_scr[:, cs] = m_new

    bufs = (sa_scr, sb_scr, sc_scr)

    def context_keys_only():
        for c in range(ctx_chunks):
            accumulate(c, 1, sa_scr, scores(c, 1, sa_scr))

    def all_keys():
        n_groups = all_chunks // unroll
        mx = [scores(g, unroll, bufs[g % 3]) for g in range(min(2, n_groups))]
        for g in range(n_groups):
            accumulate(g, unroll, bufs[g % 3], mx[g])
            if g + 2 < n_groups:
                mx.append(scores(g + 2, unroll, bufs[(g + 2) % 3]))

    if q_off >= n_ctx_qblk:
        all_keys()
    else:
        pl.when(i < n_ctx_qblk)(context_keys_only)
        pl.when(i >= n_ctx_qblk)(all_keys)

    vdim = 2 * HEAD_DIM
    o = acc_scr[0:vdim, :] / acc_scr[vdim:vdim + 1, :]
    lp = lamp_ref[...]
    lam = (jnp.exp(jnp.sum(lp[0:1] * lp[1:2], axis=1, keepdims=True))
           - jnp.exp(jnp.sum(lp[2:3] * lp[3:4], axis=1, keepdims=True)) + lam_init)
    od = o[:, :tq] - lam * o[:, tq:]
    y = od * lax.rsqrt(jnp.mean(od * od, axis=0, keepdims=True) + NORM_EPS)
    o_ref[0] = (y * g_ref[...] * (1.0 - lam_init)).T.astype(o_ref.dtype)


def _diff_attention(qbt, kb, vbt, lam_params, g_subln, lam_init, lc, with_ctx, tq, unroll):
    b, s, w = kb.shape
    heads = w // (2 * HEAD_DIM)
    vrows, tk = vbt.shape[3], vbt.shape[4]
    assert vrows == 2 * HEAD_DIM + ONES_ROWS
    q_off = 0 if with_ctx else lc // tq
    nq = s // tq - q_off
    assert (s // tk) % unroll == 0 and lc % tk == 0
    kern = functools.partial(_diff_kernel, tq=tq, tk=tk, unroll=unroll, q_off=q_off, n_ctx_qblk=lc // tq,
                             ctx_chunks=lc // tk, all_chunks=s // tk, lam_init=lam_init)
    return pl.pallas_call(
        kern,
        out_shape=jax.ShapeDtypeStruct((b, nq * tq, w), BF16),
        grid=(b, heads, nq),
        in_specs=[pl.BlockSpec((4, HEAD_DIM), lambda bi, h, i: (0, 0)),
                  pl.BlockSpec((1, 2 * HEAD_DIM, tq), lambda bi, h, i: (bi, h, i + q_off)),
                  pl.BlockSpec((1, s, 2 * HEAD_DIM), lambda bi, h, i: (bi, 0, h)),
                  pl.BlockSpec((1, 1, s // tk, vrows, tk), lambda bi, h, i: (bi, h, 0, 0, 0)),
                  pl.BlockSpec((2 * HEAD_DIM, 1), lambda bi, h, i: (0, 0))],
        out_specs=pl.BlockSpec((1, tq, 2 * HEAD_DIM), lambda bi, h, i: (bi, i, h)),
        scratch_shapes=[pltpu.VMEM((1, 2 * tq), F32),
                        pltpu.VMEM((vrows, 2 * tq), F32),
                        pltpu.VMEM((unroll * tk, 2 * tq), F32), pltpu.VMEM((unroll * tk, 2 * tq), F32),
                        pltpu.VMEM((unroll * tk, 2 * tq), F32)],
        compiler_params=_cparams(("arbitrary", "arbitrary", "arbitrary")),
        name="diff_attention",
    )(lam_params, qbt, kb, vbt, g_subln.reshape(2 * HEAD_DIM, 1))


def _swa_kernel(sink_ref, qt_ref, kctx_ref, vtctx_ref, kp_ref, kc_ref, kn_ref, vtp_ref, vtc_ref, vtn_ref, o_ref,
                *, tq, q_off, n_ctx_qblk, n_lat, kv_heads, group):
    i = pl.program_id(1) + q_off
    is_lat = i >= n_ctx_qblk
    q0 = (i - n_ctx_qblk) * tq
    wq = group * tq
    kpos = q0 - tq + lax.broadcasted_iota(jnp.int32, (3 * tq, wq), 0)
    qpos = q0 + jnp.bitwise_and(lax.broadcasted_iota(jnp.int32, (3 * tq, wq), 1), tq - 1)
    valid = jnp.logical_and(jnp.abs(qpos - kpos) <= WINDOW, jnp.logical_and(kpos >= 0, kpos < n_lat))
    valid = jnp.logical_and(valid, is_lat)
    qt = qt_ref[0]
    kloc = jnp.concatenate([kp_ref[0], kc_ref[0], kn_ref[0]], axis=0)
    vtloc = jnp.concatenate([vtp_ref[0], vtc_ref[0], vtn_ref[0]], axis=1)
    kctx = kctx_ref[0]
    vtctx = vtctx_ref[0]
    zeros = jnp.zeros((HEAD_DIM, tq), qt.dtype)
    heads_out = []
    for h in range(kv_heads):
        gs = slice((h // 2) * LANES, (h // 2 + 1) * LANES)
        vs = slice((h // 2) * (LANES + ONES_ROWS), (h // 2 + 1) * (LANES + ONES_ROWS))
        upper = h % 2 == 1
        cols = []
        for gi in range(group):
            hq = h * group + gi
            qh = qt[hq * HEAD_DIM:(hq + 1) * HEAD_DIM, :]
            cols.append(jnp.concatenate([zeros, qh] if upper else [qh, zeros], axis=0))
        rhs = jnp.concatenate(cols, axis=1)
        sink = jnp.concatenate(
            [jnp.full((1, tq), sink_ref[h * group + gi] * LOG2E, F32) for gi in range(group)], axis=1)
        s_ctx = jnp.dot(kctx[:, gs], rhs, preferred_element_type=F32)
        s_loc = jnp.dot(kloc[:, gs], rhs, preferred_element_type=F32)
        s_loc = jnp.where(valid, s_loc, MASK_VALUE)
        m = jnp.maximum(jnp.maximum(jnp.max(s_ctx, axis=0, keepdims=True), jnp.max(s_loc, axis=0, keepdims=True)), sink)
        e_ctx = jnp.exp2((s_ctx - m).astype(BF16))
        e_loc = jnp.exp2((s_loc - m).astype(BF16))
        ot = (jnp.dot(vtctx[vs, :], e_ctx, preferred_element_type=F32)
              + jnp.dot(vtloc[vs, :], e_loc, preferred_element_type=F32))
        denom = ot[LANES:LANES + 1, :] + jnp.exp2(sink - m)
        ot = (ot[HEAD_DIM:LANES, :] if upper else ot[:HEAD_DIM, :]) / denom
        for gi in range(group):
            heads_out.append(ot[:, gi * tq:(gi + 1) * tq])
    for pi in range(len(heads_out) // 2):
        pair = jnp.concatenate([heads_out[2 * pi], heads_out[2 * pi + 1]], axis=0)
        o_ref[0, :, pi * LANES:(pi + 1) * LANES] = pair.T.astype(o_ref.dtype)


def _swa(qct, kc, vct, sink, lc, with_ctx, tq):
    b, s, kvw = kc.shape
    w = qct.shape[1]
    kv_heads = kvw // HEAD_DIM
    group = w // kvw
    assert tq & (tq - 1) == 0 and kv_heads % 2 == 0
    q_off = 0 if with_ctx else lc // tq
    nq = s // tq - q_off
    nblk = s // tq

    def rows(f):
        return pl.BlockSpec((1, tq, kvw), lambda bi, i: (bi, f(i + q_off), 0))

    vrows = vct.shape[1]
    assert vrows == (kvw // LANES) * (LANES + ONES_ROWS)

    def cols(f):
        return pl.BlockSpec((1, vrows, tq), lambda bi, i: (bi, 0, f(i + q_off)))

    cur = lambda i: i
    prev = lambda i: jnp.maximum(i - 1, 0)
    nxt = lambda i: jnp.minimum(i + 1, nblk - 1)
    kern = functools.partial(_swa_kernel, tq=tq, q_off=q_off, n_ctx_qblk=lc // tq, n_lat=s - lc,
                             kv_heads=kv_heads, group=group)
    return pl.pallas_call(
        kern,
        out_shape=jax.ShapeDtypeStruct((b, nq * tq, w), BF16),
        grid=(b, nq),
        in_specs=[pl.BlockSpec(memory_space=pltpu.SMEM),
                  pl.BlockSpec((1, w, tq), lambda bi, i: (bi, 0, i + q_off)),
                  pl.BlockSpec((1, lc, kvw), lambda bi, i: (bi, 0, 0)),
                  pl.BlockSpec((1, vrows, lc), lambda bi, i: (bi, 0, 0)),
                  rows(prev), rows(cur), rows(nxt), cols(prev), cols(cur), cols(nxt)],
        out_specs=pl.BlockSpec((1, tq, w), lambda bi, i: (bi, i, 0)),
        compiler_params=_cparams(("arbitrary", "arbitrary")),
        name="swa",
    )(sink, qct, kc, vct, kc, kc, kc, vct, vct, vct)


def _out_kernel(a_ref, ob_ref, oc_ref, xc_ref, xl_ref, w_ref, mod_ref, g_ref, xo_ref, h_ref, *, n_ctx_tiles):
    na, nb = a_ref.shape[2], ob_ref.shape[2]
    mix = jnp.dot(a_ref[0], w_ref[0:na, :], preferred_element_type=F32)
    mix = mix + jnp.dot(ob_ref[0], w_ref[na:na + nb, :], preferred_element_type=F32)
    mix = mix + jnp.dot(oc_ref[0], w_ref[na + nb:, :], preferred_element_type=F32)
    x = jnp.where(pl.program_id(1) < n_ctx_tiles, xc_ref[0], xl_ref[0])
    xn = x + mod_ref[0, 2:3, :] * mix
    xo_ref[0] = xn
    h_ref[0] = _rms_mod(xn, g_ref[...], mod_ref[0, 3:4, :], mod_ref[0, 4:5, :]).astype(h_ref.dtype)


def _out_proj(a, ob, oc, parts, w_bf16, mod, g2, lc, with_ctx, tm, h_dtype):
    b, d = parts[0].shape[0], parts[0].shape[2]
    s = sum(p.shape[1] for p in parts)
    nctx = lc // tm
    q_off = 0 if with_ctx else nctx
    nt = s // tm - q_off
    n_mod = mod.shape[0]
    out_rows = s - q_off * tm

    def row(bi, j):
        return (bi, j + q_off, 0)

    def modrow(bi, j):
        return (jnp.where(j + q_off < nctx, n_mod - 1, bi), 0, 0)

    def own(bi, j):
        return (bi, j, 0)

    assert ob.shape[1] == out_rows and oc.shape[1] == out_rows
    stream, stream_specs = _stream_specs(parts, lc, tm, q_off)
    return pl.pallas_call(
        functools.partial(_out_kernel, n_ctx_tiles=nctx - q_off),
        out_shape=(jax.ShapeDtypeStruct((b, out_rows, d), F32), jax.ShapeDtypeStruct((b, out_rows, d), h_dtype)),
        grid=(b, nt),
        in_specs=[pl.BlockSpec((1, tm, a.shape[2]), row), pl.BlockSpec((1, tm, ob.shape[2]), own),
                  pl.BlockSpec((1, tm, oc.shape[2]), own)] + stream_specs + [
                  pl.BlockSpec(w_bf16.shape, lambda bi, j: (0, 0)),
                  pl.BlockSpec((1, 6, d), modrow),
                  pl.BlockSpec((1, d), lambda bi, j: (0, 0))],
        out_specs=(pl.BlockSpec((1, tm, d), own), pl.BlockSpec((1, tm, d), own)),
        compiler_params=_cparams(("arbitrary", "arbitrary")),
        name="out_proj",
    )(a, ob, oc, *stream, w_bf16, mod, g2.reshape(1, d))


def _ffn_kernel(h_ref, x_ref, wg_ref, wu_ref, wd_ref, modb_ref, modc_ref, o_ref, *, tm, lc):
    f = pl.program_id(2)

    @pl.when(f == 0)
    def _():
        o_ref[...] = jnp.zeros(o_ref.shape, F32)

    h = h_ref[0]
    g = jnp.dot(h, wg_ref[...], preferred_element_type=F32)
    u = jnp.dot(h, wu_ref[...], preferred_element_type=F32)
    act = (g * _sigmoid(g) * u).astype(BF16)
    o_ref[0] += jnp.dot(act, wd_ref[...], preferred_element_type=F32)

    @pl.when(f == pl.num_programs(2) - 1)
    def _():
        rows = pl.program_id(1) * tm + lax.broadcasted_iota(jnp.int32, (tm, 1), 0)
        gate = jnp.where(rows < lc, modc_ref[0, 5:6, :], modb_ref[0, 5:6, :])
        o_ref[0] = x_ref[0] + gate * o_ref[0]


def _ffn_dense(h, x, wg, wu, wd, mod, lc, tm, tf):
    b, s, d = x.shape
    dff = wg.shape[1]
    n_mod = mod.shape[0]
    row = lambda bi, j, f: (bi, j, 0)
    return pl.pallas_call(
        functools.partial(_ffn_kernel, tm=tm, lc=lc),
        out_shape=jax.ShapeDtypeStruct((b, s, d), F32),
        grid=(b, s // tm, dff // tf),
        in_specs=[pl.BlockSpec((1, tm, d), row), pl.BlockSpec((1, tm, d), row),
                  pl.BlockSpec((d, tf), lambda bi, j, f: (0, f)),
                  pl.BlockSpec((d, tf), lambda bi, j, f: (0, f)),
                  pl.BlockSpec((tf, d), lambda bi, j, f: (f, 0)),
                  pl.BlockSpec((1, 6, d), lambda bi, j, f: (bi, 0, 0)),
                  pl.BlockSpec((1, 6, d), lambda bi, j, f: (n_mod - 1, 0, 0))],
        out_specs=pl.BlockSpec((1, tm, d), row),
        compiler_params=_cparams(("arbitrary", "arbitrary", "arbitrary")),
        name="ffn_dense",
    )(h, x, wg, wu, wd, mod, mod)


def _router_kernel(h_ref, w_ref, idx_ref, wt_ref):
    h = h_ref[...]
    w = w_ref[...]
    h_hi = h.astype(BF16)
    w_hi = w.astype(BF16)
    h_lo = (h - h_hi.astype(F32)).astype(BF16)
    w_lo = (w - w_hi.astype(F32)).astype(BF16)
    logits = (jnp.dot(h_hi, w_hi, preferred_element_type=F32) + jnp.dot(h_hi, w_lo, preferred_element_type=F32)
              + jnp.dot(h_lo, w_hi, preferred_element_type=F32))
    n_e = logits.shape[1]
    lane = lax.broadcasted_iota(jnp.int32, logits.shape, 1)
    m1 = jnp.max(logits, axis=1, keepdims=True)
    i1 = jnp.min(jnp.where(logits == m1, lane, n_e), axis=1, keepdims=True)
    rest = jnp.where(lane == i1, -jnp.inf, logits)
    m2 = jnp.max(rest, axis=1, keepdims=True)
    i2 = jnp.min(jnp.where(rest == m2, lane, n_e), axis=1, keepdims=True)
    e2 = jnp.exp(m2 - m1)
    den = 1.0 + e2
    out_lane = lax.broadcasted_iota(jnp.int32, idx_ref.shape, 1)
    idx_ref[...] = jnp.where(out_lane == 0, i1, jnp.where(out_lane == 1, i2, 0))
    wt_ref[...] = jnp.where(out_lane == 0, 1.0 / den, jnp.where(out_lane == 1, e2 / den, 0.0))


def _router(h_flat, w_router, tm):
    t, d = h_flat.shape
    n_e = w_router.shape[1]
    return pl.pallas_call(
        _router_kernel,
        out_shape=(jax.ShapeDtypeStruct((t, LANES), jnp.int32), jax.ShapeDtypeStruct((t, LANES), F32)),
        grid=(t // tm,),
        in_specs=[pl.BlockSpec((tm, d), lambda i: (i, 0)), pl.BlockSpec((d, n_e), lambda i: (0, 0))],
        out_specs=(pl.BlockSpec((tm, LANES), lambda i: (i, 0)), pl.BlockSpec((tm, LANES), lambda i: (i, 0))),
        compiler_params=_cparams(("arbitrary",)),
        name="router",
    )(h_flat, w_router)


def _row_copy(src_hbm, row, dst, r, sem):
    return pltpu.make_async_copy(src_hbm.at[pl.ds(row, 1)], dst.at[pl.ds(r, 1)], sem)


def _start_rows(src_hbm, idx_ref, dst, sem, n):
    def body(r, carry):
        _row_copy(src_hbm, idx_ref[0, 0, r], dst, r, sem).start()
        return carry

    lax.fori_loop(0, n, body, 0, unroll=8)


def _wait_rows(src_hbm, dst, sem, n):
    def body(r, carry):
        _row_copy(src_hbm, 0, dst, r, sem).wait()
        return carry

    lax.fori_loop(0, n, body, 0, unroll=8)


def _gather_kernel(nact_ref, tok_ref, tok_next_ref, h_hbm, o_ref, buf, sem, *, tm):
    i = pl.program_id(0)
    nact = nact_ref[0]
    slot = lax.rem(i, 2)

    @pl.when(jnp.logical_and(i == 0, nact > 0))
    def _():
        _start_rows(h_hbm, tok_ref, buf.at[0], sem.at[0], tm)

    @pl.when(i + 1 < nact)
    def _():
        _start_rows(h_hbm, tok_next_ref, buf.at[1 - slot], sem.at[1 - slot], tm)

    @pl.when(i < nact)
    def _():
        _wait_rows(h_hbm, buf.at[slot], sem.at[slot], tm)
        o_ref[...] = buf[slot].astype(o_ref.dtype)

    @pl.when(i >= nact)
    def _():
        o_ref[...] = jnp.zeros(o_ref.shape, o_ref.dtype)


def _gather_rows(h_flat, tok, n_active, tm):
    n_tiles = tok.shape[0]
    d = h_flat.shape[1]
    grid_spec = pltpu.PrefetchScalarGridSpec(
        num_scalar_prefetch=1,
        grid=(n_tiles,),
        in_specs=[pl.BlockSpec((1, 1, tm), lambda i, na: (i, 0, 0), memory_space=pltpu.SMEM),
                  pl.BlockSpec((1, 1, tm), lambda i, na: (jnp.minimum(i + 1, n_tiles - 1), 0, 0),
                               memory_space=pltpu.SMEM),
                  pl.BlockSpec(memory_space=pl.ANY)],
        out_specs=pl.BlockSpec((tm, d), lambda i, na: (i, 0)),
        scratch_shapes=[pltpu.VMEM((2, tm, d), h_flat.dtype), pltpu.SemaphoreType.DMA((2,))],
    )
    return pl.pallas_call(
        functools.partial(_gather_kernel, tm=tm),
        out_shape=jax.ShapeDtypeStruct((n_tiles * tm, d), BF16),
        grid_spec=grid_spec,
        compiler_params=_cparams(("arbitrary",)),
        name="moe_gather",
    )(n_active, tok, tok, h_flat)


def _moe_ffn_kernel(te_ref, rows_ref, nact_ref, x_ref, wg_ref, wu_ref, wd_ref, o_ref):
    i = pl.program_id(0)
    f = pl.program_id(1)
    half = o_ref.shape[0] // 2

    @pl.when(f == 0)
    def _():
        o_ref[...] = jnp.zeros(o_ref.shape, o_ref.dtype)

    def swiglu_rows(n_rows):
        x = x_ref[0:n_rows, :]
        g = jnp.dot(x, wg_ref[0].astype(BF16), preferred_element_type=F32)
        u = jnp.dot(x, wu_ref[0].astype(BF16), preferred_element_type=F32)
        act = (g * _sigmoid(g) * u).astype(BF16)
        o_ref[0:n_rows, :] += jnp.dot(act, wd_ref[0].astype(BF16), preferred_element_type=F32)

    rows = rows_ref[i]
    pl.when(rows > half)(functools.partial(swiglu_rows, o_ref.shape[0]))
    pl.when(jnp.logical_and(rows > 0, rows <= half))(functools.partial(swiglu_rows, half))


def _moe_ffn(xs, wg, wu, wd, tile_expert, tile_rows, n_active, tm, tf):
    p, d = xs.shape
    dff = wg.shape[2]
    nf = dff // tf
    n_tiles = p // tm

    def tile(i, f, te, tr, na):
        return (jnp.minimum(i, na[0] - 1), 0)

    def fcol(i, f, na):
        return jnp.where(i < na[0], f, nf - 1)

    grid_spec = pltpu.PrefetchScalarGridSpec(
        num_scalar_prefetch=3,
        grid=(n_tiles, nf),
        in_specs=[pl.BlockSpec((tm, d), tile),
                  pl.BlockSpec((1, d, tf), lambda i, f, te, tr, na: (te[i], 0, fcol(i, f, na))),
                  pl.BlockSpec((1, d, tf), lambda i, f, te, tr, na: (te[i], 0, fcol(i, f, na))),
                  pl.BlockSpec((1, tf, d), lambda i, f, te, tr, na: (te[i], fcol(i, f, na), 0))],
        out_specs=pl.BlockSpec((tm, d), lambda i, f, te, tr, na: (i, 0)),
    )
    return pl.pallas_call(
        _moe_ffn_kernel,
        out_shape=jax.ShapeDtypeStruct((p, d), F32),
        grid_spec=grid_spec,
        compiler_params=_cparams(("arbitrary", "arbitrary")),
        name="moe_ffn",
    )(tile_expert, tile_rows, n_active, xs, wg, wu, wd)


def _combine_kernel(p0_ref, p1_ref, p0n_ref, p1n_ref, y_hbm, x_ref, wt_ref, mod_ref, g_ref, o_ref, buf, sem, *, tm):
    t = pl.program_id(0) * pl.num_programs(1) + pl.program_id(1)
    n_tiles = pl.num_programs(0) * pl.num_programs(1)
    slot = lax.rem(t, 2)

    def start(r0, r1, s):
        _start_rows(y_hbm, r0, buf.at[s, 0], sem.at[s, 0], tm)
        _start_rows(y_hbm, r1, buf.at[s, 1], sem.at[s, 1], tm)

    @pl.when(t == 0)
    def _():
        start(p0_ref, p1_ref, 0)

    @pl.when(t + 1 < n_tiles)
    def _():
        start(p0n_ref, p1n_ref, 1 - slot)

    _wait_rows(y_hbm, buf.at[slot, 0], sem.at[slot, 0], tm)
    _wait_rows(y_hbm, buf.at[slot, 1], sem.at[slot, 1], tm)
    wt = wt_ref[...]
    f = wt[:, 0:1] * buf[slot, 0] + wt[:, 1:2] * buf[slot, 1]
    xn = x_ref[0] + mod_ref[0, 5:6, :] * f
    y = xn * lax.rsqrt(jnp.mean(xn * xn, axis=-1, keepdims=True) + NORM_EPS)
    o_ref[0] = y * g_ref[...]


def _combine(y, pos0, pos1, x, wts, mod, g_final, tm):
    b, n, d = x.shape
    nt = n // tm
    last = b * nt - 1
    idx = lambda f: pl.BlockSpec((1, 1, tm), lambda bi, j: (f(bi * nt + j), 0, 0), memory_space=pltpu.SMEM)
    cur = lambda t: t
    nxt = lambda t: jnp.minimum(t + 1, last)
    return pl.pallas_call(
        functools.partial(_combine_kernel, tm=tm),
        out_shape=jax.ShapeDtypeStruct((b, n, d), F32),
        grid=(b, nt),
        in_specs=[idx(cur), idx(cur), idx(nxt), idx(nxt),
                  pl.BlockSpec(memory_space=pl.ANY),
                  pl.BlockSpec((1, tm, d), lambda bi, j: (bi, j, 0)),
                  pl.BlockSpec((tm, LANES), lambda bi, j: (bi * nt + j, 0)),
                  pl.BlockSpec((1, 6, d), lambda bi, j: (bi, 0, 0)),
                  pl.BlockSpec((1, d), lambda bi, j: (0, 0))],
        out_specs=pl.BlockSpec((1, tm, d), lambda bi, j: (bi, j, 0)),
        scratch_shapes=[pltpu.VMEM((2, 2, tm, d), F32), pltpu.SemaphoreType.DMA((2, 2))],
        compiler_params=_cparams(("arbitrary", "arbitrary")),
        name="moe_combine",
    )(pos0, pos1, pos0, pos1, y, x, wts, mod, g_final.reshape(1, d))


def _dispatch_plan(idx, n_experts, tm):
    t = idx.shape[0]
    e_flat = idx.reshape(-1)
    onehot = (e_flat[:, None] == jnp.arange(n_experts, dtype=jnp.int32)[None, :]).astype(jnp.int32)
    rank = jnp.sum((jnp.cumsum(onehot, axis=0) - 1) * onehot, axis=1)
    counts = jnp.sum(onehot, axis=0)
    padded = ((counts + tm - 1) // tm) * tm
    ends = jnp.cumsum(padded)
    starts = ends - padded
    dest = starts[e_flat] + rank
    n_tiles = (TOP_K * t) // tm + n_experts
    tok = jnp.zeros((n_tiles * tm,), jnp.int32).at[dest].set(jnp.arange(TOP_K * t, dtype=jnp.int32) // TOP_K)
    tile_row0 = jnp.arange(n_tiles, dtype=jnp.int32) * tm
    tile_expert = jnp.sum((ends[None, :] <= tile_row0[:, None]).astype(jnp.int32), axis=1)
    tile_expert = jnp.minimum(tile_expert, n_experts - 1)
    n_active = (ends[-1] // tm).astype(jnp.int32).reshape(1)
    active = jnp.arange(n_tiles) < n_active[0]
    tile_rows = jnp.clip((starts + counts)[tile_expert] - tile_row0, 0, tm)
    tile_rows = jnp.where(active, tile_rows, 0).astype(jnp.int32)
    last = jnp.maximum(n_active[0] - 1, 0)
    tile_expert = jnp.where(active, tile_expert, tile_expert[last])
    pos = dest.reshape(t, TOP_K)
    return tok.reshape(n_tiles, 1, tm), tile_expert, tile_rows, n_active, pos[:, 0], pos[:, 1]


def _rope_tables(lc, n):
    rows = n // GRID_W
    row = jnp.repeat(jnp.arange(rows), GRID_W).astype(F32)
    col = jnp.tile(jnp.arange(GRID_W), rows).astype(F32)
    axis_dim = HEAD_DIM // 2
    inv = ROPE_THETA ** (-jnp.arange(axis_dim // 2, dtype=F32) / (axis_dim // 2))
    ang_r = row[:, None] * inv[None, :]
    ang_c = col[:, None] * inv[None, :]
    ang = jnp.concatenate([ang_r, ang_r, ang_c, ang_c], axis=-1)
    cos, sin = jnp.cos(ang), jnp.sin(ang)
    first_half = (jnp.arange(HEAD_DIM) % (HEAD_DIM // 2)) < (HEAD_DIM // 4)
    sa = jnp.where(first_half[None, :], -sin, 0.0)
    sb = jnp.where(first_half[None, :], 0.0, sin)
    ident = lambda v: jnp.full((lc, HEAD_DIM), v, F32)
    full = lambda ctx_v, t: jnp.tile(jnp.concatenate([ident(ctx_v), t], axis=0), (1, LANES // HEAD_DIM))
    return full(1.0, cos), full(0.0, sa), full(0.0, sb)


def _pick_tile(n, candidates):
    for c in candidates:
        if n % c == 0:
            return c
    raise ValueError(f"no tile size among {candidates} divides {n}")


def kernel(x, c, ctx, c_ctx, w_ada, b_ada, g_norm1, g_norm2, w_in, w_dw, b_dw, ln_g, ln_b, lam_q1, lam_k1, lam_q2, lam_k2, g_subln, sink, w_out, w_ff_gate, w_ff_up, w_ff_down, w_router, w_ex_gate, w_ex_up, w_ex_down, g_final):
    b, n, d = x.shape
    lc = ctx.shape[1]
    s = lc + n
    depth = w_ada.shape[0]
    conv_ch = w_dw.shape[-1]
    n_swa_q = sink.shape[1] * HEAD_DIM
    n_diff_v = d - conv_ch - n_swa_q
    n_kv = n_swa_q // 3
    sizes = (conv_ch, n_diff_v, n_swa_q, n_diff_v, n_diff_v, n_kv, n_kv)
    assert sum(sizes) + conv_ch == w_in.shape[2]
    n_experts = w_router.shape[2]

    tm = 256
    assert lc % tm == 0 and n % tm == 0
    tm_ffn = _pick_tile(s, (768, 512, 256))
    tf_dense = _pick_tile(w_ff_gate.shape[2], (512, 256, 128))
    tf_moe = _pick_tile(w_ex_gate.shape[3], (256, 128))
    tm_moe = 1024

    stream = (ctx, x)
    cvec = jnp.concatenate([c, c_ctx[None, :]], axis=0)
    mods = _ada(cvec, w_ada, b_ada).reshape(depth, b + 1, 6, d)
    tabs = _rope_tables(lc, n)

    out = None
    for l in range(depth):
        last = l == depth - 1
        lam_init = 0.8 - 0.6 * math.exp(-0.3 * l)
        mod = mods[l]
        lam_params = jnp.stack([lam_q1[l], lam_k1[l], lam_q2[l], lam_k2[l]], axis=0)
        y, qb, qc, kb, vb, kc, vc = _in_proj(stream, g_norm1[l], mod, w_in[l].astype(BF16), tabs, sizes, lc, tm)
        a = _conv_module(y, w_dw[l], b_dw[l], ln_g[l], ln_b[l], lc, tm)
        ob = _diff_attention(qb, kb, vb, lam_params, g_subln[l], lam_init, lc, not last, tq=256, unroll=3)
        oc = _swa(qc, kc, vc, sink[l], lc, not last, tq=WINDOW)
        xs, h2 = _out_proj(a, ob, oc, stream, w_out[l].astype(BF16), mod, g_norm2[l], lc, not last, tm,
                           F32 if l % 2 == 1 else BF16)
        i = l // 2
        if l % 2 == 0:
            assert not last, "a final dense layer would need its own final-norm epilogue"
            xs = _ffn_dense(h2, xs, w_ff_gate[i].astype(BF16), w_ff_up[i].astype(BF16),
                            w_ff_down[i].astype(BF16), mod, lc, tm_ffn, tf_dense)
            stream = (xs,)
        else:
            assert last, "the MoE layer carries the final norm and drops the context rows"
            h_flat = h2.reshape(b * n, d)
            idx, wts = _router(h_flat, w_router[i], tm)
            tok, tile_expert, tile_rows, n_active, pos0, pos1 = _dispatch_plan(idx[:, :TOP_K], n_experts, tm_moe)
            xg = _gather_rows(h_flat, tok, n_active, tm_moe)
            yexp = _moe_ffn(xg, w_ex_gate[i], w_ex_up[i], w_ex_down[i], tile_expert, tile_rows, n_active,
                            tm_moe, tf_moe)
            nt = (b * n) // tm
            out = _combine(yexp, pos0.reshape(nt, 1, tm), pos1.reshape(nt, 1, tm), xs, wts, mod, g_final, tm)
    return out
```

```python
import functools
import math

import jax
import jax.numpy as jnp
from jax import lax
from jax.experimental import pallas as pl
from jax.experimental.pallas import tpu as pltpu

F32 = jnp.float32
BF16 = jnp.bfloat16

HEAD_DIM = 64
GRID_W = 64
ROPE_THETA = 10000.0
NORM_EPS = 1e-6
MASK_VALUE = -1e30
LOG2E = math.log2(math.e)
CONV_K = 31
CONV_HALO = 16
WINDOW = 128
TOP_K = 2
LANES = 128
SUBLANES = 8
ONES_ROWS = 16
VMEM_LIMIT = 56 * 1024 * 1024


def _cparams(sem):
    return pltpu.CompilerParams(dimension_semantics=sem, vmem_limit_bytes=VMEM_LIMIT)


def _rms_mod(x, g, shift, scale):
    y = x * lax.rsqrt(jnp.mean(x * x, axis=-1, keepdims=True) + NORM_EPS)
    return (y * g) * (1.0 + scale) + shift


def _sigmoid(x):
    return 1.0 / (1.0 + jnp.exp(-x))


def _ada_kernel(c_ref, w_ref, b_ref, o_ref):
    c = c_ref[...]
    s = (c * _sigmoid(c)).astype(BF16)
    o_ref[0] = jnp.dot(s, w_ref[0].astype(BF16), preferred_element_type=F32) + b_ref[0]


def _ada(cvec, w_ada, b_ada):
    depth, d, n6 = w_ada.shape
    rows = cvec.shape[0]
    tn = 1024
    return pl.pallas_call(
        _ada_kernel,
        out_shape=jax.ShapeDtypeStruct((depth, rows, n6), F32),
        grid=(depth, n6 // tn),
        in_specs=[pl.BlockSpec((rows, d), lambda l, j: (0, 0)),
                  pl.BlockSpec((1, d, tn), lambda l, j: (l, 0, j)),
                  pl.BlockSpec((1, 1, tn), lambda l, j: (l, 0, j))],
        out_specs=pl.BlockSpec((1, rows, tn), lambda l, j: (l, 0, j)),
        compiler_params=_cparams(("arbitrary", "arbitrary")),
        name="ada",
    )(cvec, w_ada, b_ada.reshape(depth, 1, n6))


def _in_kernel(xc_ref, xl_ref, g_ref, mod_ref, w_ref, cos_ref, sa_ref, sb_ref,
               y_ref, qb_ref, qc_ref, kb_ref, vb_ref, kc_ref, vc_ref, *, sizes, nctx):
    conv_ch, n_bq, n_cq, n_bk, n_bv, n_ck, n_cv = sizes
    x = jnp.where(pl.program_id(1) < nctx, xc_ref[0], xl_ref[0])
    hb = _rms_mod(x, g_ref[...], mod_ref[0, 0:1, :], mod_ref[0, 1:2, :]).astype(BF16)

    def mm(lo, width):
        return jnp.dot(hb, w_ref[:, lo:lo + width], preferred_element_type=F32)

    cos, sa, sb = cos_ref[...], sa_ref[...], sb_ref[...]

    def rope_store(u, out_ref, scale, transposed=False):
        for gi in range(u.shape[1] // LANES):
            ug = u[:, gi * LANES:(gi + 1) * LANES]
            r = (ug * cos + pltpu.roll(ug, LANES - 16, 1) * sa + pltpu.roll(ug, 16, 1) * sb) * scale
            if transposed:
                out_ref[0, gi * LANES:(gi + 1) * LANES, :] = r.T.astype(out_ref.dtype)
            else:
                out_ref[0, :, gi * LANES:(gi + 1) * LANES] = r.astype(out_ref.dtype)

    off = 0
    u = mm(off, 2 * conv_ch)
    y_ref[0] = u[:, :conv_ch] * _sigmoid(u[:, conv_ch:])
    off += 2 * conv_ch
    qscale = HEAD_DIM ** -0.5
    rope_store(mm(off, n_bq), qb_ref, qscale * LOG2E, transposed=True)
    off += n_bq
    rope_store(mm(off, n_cq), qc_ref, qscale * LOG2E, transposed=True)
    off += n_cq
    rope_store(mm(off, n_bk), kb_ref, 1.0)
    off += n_bk
    v = mm(off, n_bv)
    for gi in range(n_bv // LANES):
        vb_ref[0, gi, 0, 0:LANES, :] = v[:, gi * LANES:(gi + 1) * LANES].T.astype(BF16)
        vb_ref[0, gi, 0, LANES:LANES + ONES_ROWS, :] = jnp.ones((ONES_ROWS, v.shape[0]), BF16)
    off += n_bv
    rope_store(mm(off, n_ck), kc_ref, 1.0)
    off += n_ck
    v = mm(off, n_cv)
    grp = LANES + ONES_ROWS
    for gi in range(n_cv // LANES):
        vc_ref[0, gi * grp:gi * grp + LANES, :] = v[:, gi * LANES:(gi + 1) * LANES].T.astype(BF16)
        vc_ref[0, gi * grp + LANES:(gi + 1) * grp, :] = jnp.ones((ONES_ROWS, v.shape[0]), BF16)


def _stream_specs(parts, lc, tm, q_off=0):
    nctx = lc // tm
    if len(parts) == 1:
        arrs, lat_off = (parts[0], parts[0]), nctx
    else:
        arrs, lat_off = parts, 0
    d = arrs[0].shape[2]
    ctx_spec = pl.BlockSpec((1, tm, d), lambda bi, j: (bi, jnp.minimum(j + q_off, nctx - 1), 0))
    lat_spec = pl.BlockSpec((1, tm, d), lambda bi, j: (bi, jnp.maximum(j + q_off - nctx, 0) + lat_off, 0))
    return arrs, [ctx_spec, lat_spec]


def _in_proj(parts, g, mod, w_bf16, tabs, sizes, lc, tm):
    b, d = parts[0].shape[0], parts[0].shape[2]
    s = sum(p.shape[1] for p in parts)
    conv_ch, n_bq, n_cq, n_bk, n_bv, n_ck, n_cv = sizes
    d_in = w_bf16.shape[1]
    nctx = lc // tm
    n_mod = mod.shape[0]

    def row(bi, j):
        return (bi, j, 0)

    def modrow(bi, j):
        return (jnp.where(j < nctx, n_mod - 1, bi), 0, 0)

    widths = (conv_ch, n_bq, n_cq, n_bk, n_bv, n_ck, n_cv)
    dts = (F32, BF16, BF16, BF16, BF16, BF16, BF16)
    out_shape = [jax.ShapeDtypeStruct((b, s, w), dt) for w, dt in zip(widths, dts)]
    out_specs = [pl.BlockSpec((1, tm, w), row) for w in widths]
    n_vh = n_bv // LANES
    out_shape[1] = jax.ShapeDtypeStruct((b, n_bq, s), BF16)
    out_specs[1] = pl.BlockSpec((1, n_bq, tm), lambda bi, j: (bi, 0, j))
    out_shape[4] = jax.ShapeDtypeStruct((b, n_vh, s // tm, LANES + ONES_ROWS, tm), BF16)
    out_specs[4] = pl.BlockSpec((1, n_vh, 1, LANES + ONES_ROWS, tm), lambda bi, j: (bi, 0, j, 0, 0))
    for idx, wdt in ((2, n_cq), (6, (n_cv // LANES) * (LANES + ONES_ROWS))):
        out_shape[idx] = jax.ShapeDtypeStruct((b, wdt, s), BF16)
        out_specs[idx] = pl.BlockSpec((1, wdt, tm), lambda bi, j: (bi, 0, j))
    tab_spec = pl.BlockSpec((tm, LANES), lambda bi, j: (j, 0))
    stream, stream_specs = _stream_specs(parts, lc, tm)
    return pl.pallas_call(
        functools.partial(_in_kernel, sizes=sizes, nctx=nctx),
        out_shape=out_shape,
        grid=(b, s // tm),
        in_specs=stream_specs + [
                  pl.BlockSpec((1, d), lambda bi, j: (0, 0)),
                  pl.BlockSpec((1, 6, d), modrow),
                  pl.BlockSpec((d, d_in), lambda bi, j: (0, 0)),
                  tab_spec, tab_spec, tab_spec],
        out_specs=out_specs,
        compiler_params=_cparams(("arbitrary", "arbitrary")),
        name="in_proj",
    )(*stream, g.reshape(1, d), mod, w_bf16, *tabs)


def _conv_kernel(yp_ref, yc_ref, yn_ref, w_ref, b_ref, g_ref, bb_ref, o_ref, win_ref, *, tm, seg_tile, sub):
    j = pl.program_id(1)
    nj = pl.num_programs(1)
    has_prev = jnp.logical_and(j > 0, j != seg_tile)
    has_next = jnp.logical_and(j + 1 < nj, j + 1 != seg_tile)
    h = CONV_HALO
    win_ref[0, 0:h, :] = jnp.where(has_prev, yp_ref[0], 0.0)
    win_ref[0, h:h + tm, :] = yc_ref[0]
    win_ref[0, h + tm:h + tm + h, :] = jnp.where(has_next, yn_ref[0], 0.0)
    span = tm + 2 * h - SUBLANES
    for r in range(1, SUBLANES):
        win_ref[r, 0:span, :] = win_ref[0, pl.ds(r, span), :]
    base = h - CONV_K // 2
    for r0 in range(0, tm, sub):
        acc = jnp.zeros((sub, win_ref.shape[2]), F32)
        for t in range(CONV_K):
            o = base + r0 + t
            acc = acc + w_ref[t:t + 1, :] * win_ref[o % SUBLANES, o - o % SUBLANES:o - o % SUBLANES + sub, :]
        y = acc + b_ref[...]
        mu = jnp.mean(y, axis=-1, keepdims=True)
        yc = y - mu
        var = jnp.mean(yc * yc, axis=-1, keepdims=True)
        z = yc * lax.rsqrt(var + NORM_EPS) * g_ref[...] + bb_ref[...]
        o_ref[0, r0:r0 + sub, :] = (z * _sigmoid(z)).astype(o_ref.dtype)


def _conv_module(y, w_dw, b_dw, ln_g, ln_b, lc, tm):
    b, s, ch = y.shape
    h = CONV_HALO
    per = tm // h
    nblk = s // h

    def prev(bi, j):
        return (bi, jnp.maximum(j * per - 1, 0), 0)

    def nxt(bi, j):
        return (bi, jnp.minimum((j + 1) * per, nblk - 1), 0)

    vec = pl.BlockSpec((1, ch), lambda bi, j: (0, 0))
    return pl.pallas_call(
        functools.partial(_conv_kernel, tm=tm, seg_tile=lc // tm, sub=32),
        out_shape=jax.ShapeDtypeStruct((b, s, ch), BF16),
        grid=(b, s // tm),
        in_specs=[pl.BlockSpec((1, h, ch), prev),
                  pl.BlockSpec((1, tm, ch), lambda bi, j: (bi, j, 0)),
                  pl.BlockSpec((1, h, ch), nxt),
                  pl.BlockSpec((CONV_K, ch), lambda bi, j: (0, 0)),
                  vec, vec, vec],
        out_specs=pl.BlockSpec((1, tm, ch), lambda bi, j: (bi, j, 0)),
        scratch_shapes=[pltpu.VMEM((SUBLANES, tm + 2 * h, ch), F32)],
        compiler_params=_cparams(("arbitrary", "arbitrary")),
        name="conv_module",
    )(y, y, y, w_dw.reshape(CONV_K, ch), b_dw.reshape(1, ch), ln_g.reshape(1, ch), ln_b.reshape(1, ch))


def _diff_kernel(lamp_ref, qt_ref, k_ref, vt_ref, g_ref, o_ref, m_scr, acc_scr, sa_scr, sb_scr, sc_scr, rhs_scr,
                 *, tq, tk, unroll, q_off, n_ctx_qblk, ctx_chunks, all_chunks, lam_init):
    i = pl.program_id(2) + q_off
    qt = qt_ref[0]
    sub = lax.broadcasted_iota(jnp.int32, qt.shape, 0)
    zero = jnp.zeros_like(qt)
    rhs_scr[:, 0:tq] = jnp.where(sub < HEAD_DIM, qt, zero)
    rhs_scr[:, tq:2 * tq] = jnp.where(sub >= HEAD_DIM, qt, zero)
    m_scr[...] = jnp.full(m_scr.shape, MASK_VALUE, F32)
    acc_scr[...] = jnp.zeros(acc_scr.shape, F32)

    def scores(gidx, count, s_ref):
        mxs = []
        for half in range(2):
            cs = slice(half * tq, (half + 1) * tq)
            mx = None
            for u in range(count):
                start = pl.multiple_of((gidx * count + u) * tk, tk)
                s = jnp.dot(k_ref[0, pl.ds(start, tk), :], rhs_scr[:, cs], preferred_element_type=F32)
                s_ref[u * tk:(u + 1) * tk, cs] = s
                cm = jnp.max(s, axis=0, keepdims=True)
                mx = cm if mx is None else jnp.maximum(mx, cm)
            mxs.append(mx)
        return mxs

    def accumulate(gidx, count, s_ref, mxs):
        for half in range(2):
            cs = slice(half * tq, (half + 1) * tq)
            m_prev = m_scr[:, cs]
            m_new = jnp.maximum(m_prev, mxs[half])
            alpha = jnp.exp2(m_prev - m_new)
            pv = None
            for u in range(count):
                p = jnp.exp2((s_ref[u * tk:(u + 1) * tk, cs] - m_new).astype(BF16))
                d = jnp.dot(vt_ref[0, 0, gidx * count + u], p, preferred_element_type=F32)
                pv = d if pv is None else pv + d
            acc_scr[:, cs] = alpha * acc_scr[:, cs] + pv
            m_scr[:, cs] = m_new

    bufs = (sa_scr, sb_scr, sc_scr)

    def context_keys_only():
        for c in range(ctx_chunks):
            accumulate(c, 1, sa_scr, scores(c, 1, sa_scr))

    def all_keys():
        n_groups = all_chunks // unroll
        mx = [scores(g, unroll, bufs[g % 3]) for g in range(min(2, n_groups))]
        for g in range(n_groups):
            accumulate(g, unroll, bufs[g % 3], mx[g])
            if g + 2 < n_groups:
                mx.append(scores(g + 2, unroll, bufs[(g + 2) % 3]))

    if q_off >= n_ctx_qblk:
        all_keys()
    else:
        pl.when(i < n_ctx_qblk)(context_keys_only)
        pl.when(i >= n_ctx_qblk)(all_keys)

    vdim = 2 * HEAD_DIM
    o = acc_scr[0:vdim, :] / acc_scr[vdim:vdim + 1, :]
    lp = lamp_ref[...]
    lam = (jnp.exp(jnp.sum(lp[0:1] * lp[1:2], axis=1, keepdims=True))
           - jnp.exp(jnp.sum(lp[2:3] * lp[3:4], axis=1, keepdims=True)) + lam_init)
    od = o[:, :tq] - lam * o[:, tq:]
    y = od * lax.rsqrt(jnp.mean(od * od, axis=0, keepdims=True) + NORM_EPS)
    o_ref[0] = (y * g_ref[...] * (1.0 - lam_init)).T.astype(o_ref.dtype)


def _diff_attention(qbt, kb, vbt, lam_params, g_subln, lam_init, lc, with_ctx, tq, unroll):
    b, s, w = kb.shape
    heads = w // (2 * HEAD_DIM)
    vrows, tk = vbt.shape[3], vbt.shape[4]
    assert vrows == 2 * HEAD_DIM + ONES_ROWS
    q_off = 0 if with_ctx else lc // tq
    nq = s // tq - q_off
    assert (s // tk) % unroll == 0 and lc % tk == 0
    kern = functools.partial(_diff_kernel, tq=tq, tk=tk, unroll=unroll, q_off=q_off, n_ctx_qblk=lc // tq,
                             ctx_chunks=lc // tk, all_chunks=s // tk, lam_init=lam_init)
    return pl.pallas_call(
        kern,
        out_shape=jax.ShapeDtypeStruct((b, nq * tq, w), BF16),
        grid=(b, heads, nq),
        in_specs=[pl.BlockSpec((4, HEAD_DIM), lambda bi, h, i: (0, 0)),
                  pl.BlockSpec((1, 2 * HEAD_DIM, tq), lambda bi, h, i: (bi, h, i + q_off)),
                  pl.BlockSpec((1, s, 2 * HEAD_DIM), lambda bi, h, i: (bi, 0, h)),
                  pl.BlockSpec((1, 1, s // tk, vrows, tk), lambda bi, h, i: (bi, h, 0, 0, 0)),
                  pl.BlockSpec((2 * HEAD_DIM, 1), lambda bi, h, i: (0, 0))],
        out_specs=pl.BlockSpec((1, tq, 2 * HEAD_DIM), lambda bi, h, i: (bi, i, h)),
        scratch_shapes=[pltpu.VMEM((1, 2 * tq), F32),
                        pltpu.VMEM((vrows, 2 * tq), F32),
                        pltpu.VMEM((unroll * tk, 2 * tq), F32), pltpu.VMEM((unroll * tk, 2 * tq), F32),
                        pltpu.VMEM((unroll * tk, 2 * tq), F32),
                        pltpu.VMEM((2 * HEAD_DIM, 2 * tq), BF16)],
        compiler_params=_cparams(("arbitrary", "arbitrary", "arbitrary")),
        name="diff_attention",
    )(lam_params, qbt, kb, vbt, g_subln.reshape(2 * HEAD_DIM, 1))


def _swa_kernel(sink_ref, qt_ref, kctx_ref, vtctx_ref, kp_ref, kc_ref, kn_ref, vtp_ref, vtc_ref, vtn_ref, o_ref,
                *, tq, q_off, n_ctx_qblk, n_lat, kv_heads, group):
    i = pl.program_id(1) + q_off
    is_lat = i >= n_ctx_qblk
    q0 = (i - n_ctx_qblk) * tq
    wq = group * tq
    kpos = q0 - tq + lax.broadcasted_iota(jnp.int32, (3 * tq, wq), 0)
    qpos = q0 + jnp.bitwise_and(lax.broadcasted_iota(jnp.int32, (3 * tq, wq), 1), tq - 1)
    valid = jnp.logical_and(jnp.abs(qpos - kpos) <= WINDOW, jnp.logical_and(kpos >= 0, kpos < n_lat))
    valid = jnp.logical_and(valid, is_lat)
    qt = qt_ref[0]
    kloc = jnp.concatenate([kp_ref[0], kc_ref[0], kn_ref[0]], axis=0)
    vtloc = jnp.concatenate([vtp_ref[0], vtc_ref[0], vtn_ref[0]], axis=1)
    kctx = kctx_ref[0]
    vtctx = vtctx_ref[0]
    zeros = jnp.zeros((HEAD_DIM, tq), qt.dtype)
    heads_out = []
    for h in range(kv_heads):
        gs = slice((h // 2) * LANES, (h // 2 + 1) * LANES)
        vs = slice((h // 2) * (LANES + ONES_ROWS), (h // 2 + 1) * (LANES + ONES_ROWS))
        upper = h % 2 == 1
        cols = []
        for gi in range(group):
            hq = h * group + gi
            qh = qt[hq * HEAD_DIM:(hq + 1) * HEAD_DIM, :]
            cols.append(jnp.concatenate([zeros, qh] if upper else [qh, zeros], axis=0))
        rhs = jnp.concatenate(cols, axis=1)
        sink = jnp.concatenate(
            [jnp.full((1, tq), sink_ref[h * group + gi] * LOG2E, F32) for gi in range(group)], axis=1)
        s_ctx = jnp.dot(kctx[:, gs], rhs, preferred_element_type=F32)
        s_loc = jnp.dot(kloc[:, gs], rhs, preferred_element_type=F32)
        s_loc = jnp.where(valid, s_loc, MASK_VALUE)
        m = jnp.maximum(jnp.maximum(jnp.max(s_ctx, axis=0, keepdims=True), jnp.max(s_loc, axis=0, keepdims=True)), sink)
        e_ctx = jnp.exp2((s_ctx - m).astype(BF16))
        e_loc = jnp.exp2((s_loc - m).astype(BF16))
        ot = (jnp.dot(vtctx[vs, :], e_ctx, preferred_element_type=F32)
              + jnp.dot(vtloc[vs, :], e_loc, preferred_element_type=F32))
        denom = ot[LANES:LANES + 1, :] + jnp.exp2(sink - m)
        ot = (ot[HEAD_DIM:LANES, :] if upper else ot[:HEAD_DIM, :]) / denom
        for gi in range(group):
            heads_out.append(ot[:, gi * tq:(gi + 1) * tq])
    for pi in range(len(heads_out) // 2):
        pair = jnp.concatenate([heads_out[2 * pi], heads_out[2 * pi + 1]], axis=0)
        o_ref[0, :, pi * LANES:(pi + 1) * LANES] = pair.T.astype(o_ref.dtype)


def _swa(qct, kc, vct, sink, lc, with_ctx, tq):
    b, s, kvw = kc.shape
    w = qct.shape[1]
    kv_heads = kvw // HEAD_DIM
    group = w // kvw
    assert tq & (tq - 1) == 0 and kv_heads % 2 == 0
    q_off = 0 if with_ctx else lc // tq
    nq = s // tq - q_off
    nblk = s // tq

    def rows(f):
        return pl.BlockSpec((1, tq, kvw), lambda bi, i: (bi, f(i + q_off), 0))

    vrows = vct.shape[1]
    assert vrows == (kvw // LANES) * (LANES + ONES_ROWS)

    def cols(f):
        return pl.BlockSpec((1, vrows, tq), lambda bi, i: (bi, 0, f(i + q_off)))

    cur = lambda i: i
    prev = lambda i: jnp.maximum(i - 1, 0)
    nxt = lambda i: jnp.minimum(i + 1, nblk - 1)
    kern = functools.partial(_swa_kernel, tq=tq, q_off=q_off, n_ctx_qblk=lc // tq, n_lat=s - lc,
                             kv_heads=kv_heads, group=group)
    return pl.pallas_call(
        kern,
        out_shape=jax.ShapeDtypeStruct((b, nq * tq, w), BF16),
        grid=(b, nq),
        in_specs=[pl.BlockSpec(memory_space=pltpu.SMEM),
                  pl.BlockSpec((1, w, tq), lambda bi, i: (bi, 0, i + q_off)),
                  pl.BlockSpec((1, lc, kvw), lambda bi, i: (bi, 0, 0)),
                  pl.BlockSpec((1, vrows, lc), lambda bi, i: (bi, 0, 0)),
                  rows(prev), rows(cur), rows(nxt), cols(prev), cols(cur), cols(nxt)],
        out_specs=pl.BlockSpec((1, tq, w), lambda bi, i: (bi, i, 0)),
        compiler_params=_cparams(("arbitrary", "arbitrary")),
        name="swa",
    )(sink, qct, kc, vct, kc, kc, kc, vct, vct, vct)


def _out_kernel(a_ref, ob_ref, oc_ref, xc_ref, xl_ref, w_ref, mod_ref, g_ref, xo_ref, h_ref, *, n_ctx_tiles):
    na, nb = a_ref.shape[2], ob_ref.shape[2]
    mix = jnp.dot(a_ref[0], w_ref[0:na, :], preferred_element_type=F32)
    mix = mix + jnp.dot(ob_ref[0], w_ref[na:na + nb, :], preferred_element_type=F32)
    mix = mix + jnp.dot(oc_ref[0], w_ref[na + nb:, :], preferred_element_type=F32)
    x = jnp.where(pl.program_id(1) < n_ctx_tiles, xc_ref[0], xl_ref[0])
    xn = x + mod_ref[0, 2:3, :] * mix
    xo_ref[0] = xn
    h_ref[0] = _rms_mod(xn, g_ref[...], mod_ref[0, 3:4, :], mod_ref[0, 4:5, :]).astype(h_ref.dtype)


def _out_proj(a, ob, oc, parts, w_bf16, mod, g2, lc, with_ctx, tm, h_dtype):
    b, d = parts[0].shape[0], parts[0].shape[2]
    s = sum(p.shape[1] for p in parts)
    nctx = lc // tm
    q_off = 0 if with_ctx else nctx
    nt = s // tm - q_off
    n_mod = mod.shape[0]
    out_rows = s - q_off * tm

    def row(bi, j):
        return (bi, j + q_off, 0)

    def modrow(bi, j):
        return (jnp.where(j + q_off < nctx, n_mod - 1, bi), 0, 0)

    def own(bi, j):
        return (bi, j, 0)

    assert ob.shape[1] == out_rows and oc.shape[1] == out_rows
    stream, stream_specs = _stream_specs(parts, lc, tm, q_off)
    return pl.pallas_call(
        functools.partial(_out_kernel, n_ctx_tiles=nctx - q_off),
        out_shape=(jax.ShapeDtypeStruct((b, out_rows, d), F32), jax.ShapeDtypeStruct((b, out_rows, d), h_dtype)),
        grid=(b, nt),
        in_specs=[pl.BlockSpec((1, tm, a.shape[2]), row), pl.BlockSpec((1, tm, ob.shape[2]), own),
                  pl.BlockSpec((1, tm, oc.shape[2]), own)] + stream_specs + [
                  pl.BlockSpec(w_bf16.shape, lambda bi, j: (0, 0)),
                  pl.BlockSpec((1, 6, d), modrow),
                  pl.BlockSpec((1, d), lambda bi, j: (0, 0))],
        out_specs=(pl.BlockSpec((1, tm, d), own), pl.BlockSpec((1, tm, d), own)),
        compiler_params=_cparams(("arbitrary", "arbitrary")),
        name="out_proj",
    )(a, ob, oc, *stream, w_bf16, mod, g2.reshape(1, d))


def _ffn_kernel(h_ref, x_ref, wg_ref, wu_ref, wd_ref, modb_ref, modc_ref, o_ref, *, tm, lc):
    f = pl.program_id(2)

    @pl.when(f == 0)
    def _():
        o_ref[...] = jnp.zeros(o_ref.shape, F32)

    h = h_ref[0]
    g = jnp.dot(h, wg_ref[...], preferred_element_type=F32)
    u = jnp.dot(h, wu_ref[...], preferred_element_type=F32)
    act = (g * _sigmoid(g) * u).astype(BF16)
    o_ref[0] += jnp.dot(act, wd_ref[...], preferred_element_type=F32)

    @pl.when(f == pl.num_programs(2) - 1)
    def _():
        rows = pl.program_id(1) * tm + lax.broadcasted_iota(jnp.int32, (tm, 1), 0)
        gate = jnp.where(rows < lc, modc_ref[0, 5:6, :], modb_ref[0, 5:6, :])
        o_ref[0] = x_ref[0] + gate * o_ref[0]


def _ffn_dense(h, x, wg, wu, wd, mod, lc, tm, tf):
    b, s, d = x.shape
    dff = wg.shape[1]
    n_mod = mod.shape[0]
    row = lambda bi, j, f: (bi, j, 0)
    return pl.pallas_call(
        functools.partial(_ffn_kernel, tm=tm, lc=lc),
        out_shape=jax.ShapeDtypeStruct((b, s, d), F32),
        grid=(b, s // tm, dff // tf),
        in_specs=[pl.BlockSpec((1, tm, d), row), pl.BlockSpec((1, tm, d), row),
                  pl.BlockSpec((d, tf), lambda bi, j, f: (0, f)),
                  pl.BlockSpec((d, tf), lambda bi, j, f: (0, f)),
                  pl.BlockSpec((tf, d), lambda bi, j, f: (f, 0)),
                  pl.BlockSpec((1, 6, d), lambda bi, j, f: (bi, 0, 0)),
                  pl.BlockSpec((1, 6, d), lambda bi, j, f: (n_mod - 1, 0, 0))],
        out_specs=pl.BlockSpec((1, tm, d), row),
        compiler_params=_cparams(("arbitrary", "arbitrary", "arbitrary")),
        name="ffn_dense",
    )(h, x, wg, wu, wd, mod, mod)


def _router_kernel(h_ref, w_ref, idx_ref, wt_ref):
    h = h_ref[...]
    w = w_ref[...]
    h_hi = h.astype(BF16)
    w_hi = w.astype(BF16)
    h_lo = (h - h_hi.astype(F32)).astype(BF16)
    w_lo = (w - w_hi.astype(F32)).astype(BF16)
    logits = (jnp.dot(h_hi, w_hi, preferred_element_type=F32) + jnp.dot(h_hi, w_lo, preferred_element_type=F32)
              + jnp.dot(h_lo, w_hi, preferred_element_type=F32))
    n_e = logits.shape[1]
    lane = lax.broadcasted_iota(jnp.int32, logits.shape, 1)
    m1 = jnp.max(logits, axis=1, keepdims=True)
    i1 = jnp.min(jnp.where(logits == m1, lane, n_e), axis=1, keepdims=True)
    rest = jnp.where(lane == i1, -jnp.inf, logits)
    m2 = jnp.max(rest, axis=1, keepdims=True)
    i2 = jnp.min(jnp.where(rest == m2, lane, n_e), axis=1, keepdims=True)
    e2 = jnp.exp(m2 - m1)
    den = 1.0 + e2
    out_lane = lax.broadcasted_iota(jnp.int32, idx_ref.shape, 1)
    idx_ref[...] = jnp.where(out_lane == 0, i1, jnp.where(out_lane == 1, i2, 0))
    wt_ref[...] = jnp.where(out_lane == 0, 1.0 / den, jnp.where(out_lane == 1, e2 / den, 0.0))


def _router(h_flat, w_router, tm):
    t, d = h_flat.shape
    n_e = w_router.shape[1]
    return pl.pallas_call(
        _router_kernel,
        out_shape=(jax.ShapeDtypeStruct((t, LANES), jnp.int32), jax.ShapeDtypeStruct((t, LANES), F32)),
        grid=(t // tm,),
        in_specs=[pl.BlockSpec((tm, d), lambda i: (i, 0)), pl.BlockSpec((d, n_e), lambda i: (0, 0))],
        out_specs=(pl.BlockSpec((tm, LANES), lambda i: (i, 0)), pl.BlockSpec((tm, LANES), lambda i: (i, 0))),
        compiler_params=_cparams(("arbitrary",)),
        name="router",
    )(h_flat, w_router)


def _row_copy(src_hbm, row, dst, r, sem):
    return pltpu.make_async_copy(src_hbm.at[pl.ds(row, 1)], dst.at[pl.ds(r, 1)], sem)


def _start_rows(src_hbm, idx_ref, dst, sem, n):
    def body(r, carry):
        _row_copy(src_hbm, idx_ref[0, 0, r], dst, r, sem).start()
        return carry

    lax.fori_loop(0, n, body, 0, unroll=8)


def _wait_rows(src_hbm, dst, sem, n):
    def body(r, carry):
        _row_copy(src_hbm, 0, dst, r, sem).wait()
        return carry

    lax.fori_loop(0, n, body, 0, unroll=8)


def _gather_kernel(nact_ref, tok_ref, tok_next_ref, h_hbm, o_ref, buf, sem, *, tm):
    i = pl.program_id(0)
    nact = nact_ref[0]
    slot = lax.rem(i, 2)

    @pl.when(jnp.logical_and(i == 0, nact > 0))
    def _():
        _start_rows(h_hbm, tok_ref, buf.at[0], sem.at[0], tm)

    @pl.when(i + 1 < nact)
    def _():
        _start_rows(h_hbm, tok_next_ref, buf.at[1 - slot], sem.at[1 - slot], tm)

    @pl.when(i < nact)
    def _():
        _wait_rows(h_hbm, buf.at[slot], sem.at[slot], tm)
        o_ref[...] = buf[slot].astype(o_ref.dtype)

    @pl.when(i >= nact)
    def _():
        o_ref[...] = jnp.zeros(o_ref.shape, o_ref.dtype)


def _gather_rows(h_flat, tok, n_active, tm):
    n_tiles = tok.shape[0]
    d = h_flat.shape[1]
    grid_spec = pltpu.PrefetchScalarGridSpec(
        num_scalar_prefetch=1,
        grid=(n_tiles,),
        in_specs=[pl.BlockSpec((1, 1, tm), lambda i, na: (i, 0, 0), memory_space=pltpu.SMEM),
                  pl.BlockSpec((1, 1, tm), lambda i, na: (jnp.minimum(i + 1, n_tiles - 1), 0, 0),
                               memory_space=pltpu.SMEM),
                  pl.BlockSpec(memory_space=pl.ANY)],
        out_specs=pl.BlockSpec((tm, d), lambda i, na: (i, 0)),
        scratch_shapes=[pltpu.VMEM((2, tm, d), h_flat.dtype), pltpu.SemaphoreType.DMA((2,))],
    )
    return pl.pallas_call(
        functools.partial(_gather_kernel, tm=tm),
        out_shape=jax.ShapeDtypeStruct((n_tiles * tm, d), BF16),
        grid_spec=grid_spec,
        compiler_params=_cparams(("arbitrary",)),
        name="moe_gather",
    )(n_active, tok, tok, h_flat)


def _moe_ffn_kernel(te_ref, rows_ref, nact_ref, x_ref, wg_ref, wu_ref, wd_ref, o_ref):
    i = pl.program_id(0)
    f = pl.program_id(1)
    half = o_ref.shape[0] // 2

    @pl.when(f == 0)
    def _():
        o_ref[...] = jnp.zeros(o_ref.shape, o_ref.dtype)

    def swiglu_rows(n_rows):
        x = x_ref[0:n_rows, :]
        g = jnp.dot(x, wg_ref[0].astype(BF16), preferred_element_type=F32)
        u = jnp.dot(x, wu_ref[0].astype(BF16), preferred_element_type=F32)
        act = (g * _sigmoid(g) * u).astype(BF16)
        o_ref[0:n_rows, :] += jnp.dot(act, wd_ref[0].astype(BF16), preferred_element_type=F32)

    rows = rows_ref[i]
    pl.when(rows > half)(functools.partial(swiglu_rows, o_ref.shape[0]))
    pl.when(jnp.logical_and(rows > 0, rows <= half))(functools.partial(swiglu_rows, half))


def _moe_ffn(xs, wg, wu, wd, tile_expert, tile_rows, n_active, tm, tf):
    p, d = xs.shape
    dff = wg.shape[2]
    nf = dff // tf
    n_tiles = p // tm

    def tile(i, f, te, tr, na):
        return (jnp.minimum(i, na[0] - 1), 0)

    def fcol(i, f, na):
        return jnp.where(i < na[0], f, nf - 1)

    grid_spec = pltpu.PrefetchScalarGridSpec(
        num_scalar_prefetch=3,
        grid=(n_tiles, nf),
        in_specs=[pl.BlockSpec((tm, d), tile),
                  pl.BlockSpec((1, d, tf), lambda i, f, te, tr, na: (te[i], 0, fcol(i, f, na))),
                  pl.BlockSpec((1, d, tf), lambda i, f, te, tr, na: (te[i], 0, fcol(i, f, na))),
                  pl.BlockSpec((1, tf, d), lambda i, f, te, tr, na: (te[i], fcol(i, f, na), 0))],
        out_specs=pl.BlockSpec((tm, d), lambda i, f, te, tr, na: (i, 0)),
    )
    return pl.pallas_call(
        _moe_ffn_kernel,
        out_shape=jax.ShapeDtypeStruct((p, d), F32),
        grid_spec=grid_spec,
        compiler_params=_cparams(("arbitrary", "arbitrary")),
        name="moe_ffn",
    )(tile_expert, tile_rows, n_active, xs, wg, wu, wd)


def _combine_kernel(p0_ref, p1_ref, p0n_ref, p1n_ref, y_hbm, x_ref, wt_ref, mod_ref, g_ref, o_ref, buf, sem, *, tm):
    t = pl.program_id(0) * pl.num_programs(1) + pl.program_id(1)
    n_tiles = pl.num_programs(0) * pl.num_programs(1)
    slot = lax.rem(t, 2)

    def start(r0, r1, s):
        _start_rows(y_hbm, r0, buf.at[s, 0], sem.at[s, 0], tm)
        _start_rows(y_hbm, r1, buf.at[s, 1], sem.at[s, 1], tm)

    @pl.when(t == 0)
    def _():
        start(p0_ref, p1_ref, 0)

    @pl.when(t + 1 < n_tiles)
    def _():
        start(p0n_ref, p1n_ref, 1 - slot)

    _wait_rows(y_hbm, buf.at[slot, 0], sem.at[slot, 0], tm)
    _wait_rows(y_hbm, buf.at[slot, 1], sem.at[slot, 1], tm)
    wt = wt_ref[...]
    f = wt[:, 0:1] * buf[slot, 0] + wt[:, 1:2] * buf[slot, 1]
    xn = x_ref[0] + mod_ref[0, 5:6, :] * f
    y = xn * lax.rsqrt(jnp.mean(xn * xn, axis=-1, keepdims=True) + NORM_EPS)
    o_ref[0] = y * g_ref[...]


def _combine(y, pos0, pos1, x, wts, mod, g_final, tm):
    b, n, d = x.shape
    nt = n // tm
    last = b * nt - 1
    idx = lambda f: pl.BlockSpec((1, 1, tm), lambda bi, j: (f(bi * nt + j), 0, 0), memory_space=pltpu.SMEM)
    cur = lambda t: t
    nxt = lambda t: jnp.minimum(t + 1, last)
    return pl.pallas_call(
        functools.partial(_combine_kernel, tm=tm),
        out_shape=jax.ShapeDtypeStruct((b, n, d), F32),
        grid=(b, nt),
        in_specs=[idx(cur), idx(cur), idx(nxt), idx(nxt),
                  pl.BlockSpec(memory_space=pl.ANY),
                  pl.BlockSpec((1, tm, d), lambda bi, j: (bi, j, 0)),
                  pl.BlockSpec((tm, LANES), lambda bi, j: (bi * nt + j, 0)),
                  pl.BlockSpec((1, 6, d), lambda bi, j: (bi, 0, 0)),
                  pl.BlockSpec((1, d), lambda bi, j: (0, 0))],
        out_specs=pl.BlockSpec((1, tm, d), lambda bi, j: (bi, j, 0)),
        scratch_shapes=[pltpu.VMEM((2, 2, tm, d), F32), pltpu.SemaphoreType.DMA((2, 2))],
        compiler_params=_cparams(("arbitrary", "arbitrary")),
        name="moe_combine",
    )(pos0, pos1, pos0, pos1, y, x, wts, mod, g_final.reshape(1, d))


def _dispatch_plan(idx, n_experts, tm):
    t = idx.shape[0]
    e_flat = idx.reshape(-1)
    onehot = (e_flat[:, None] == jnp.arange(n_experts, dtype=jnp.int32)[None, :]).astype(jnp.int32)
    rank = jnp.sum((jnp.cumsum(onehot, axis=0) - 1) * onehot, axis=1)
    counts = jnp.sum(onehot, axis=0)
    padded = ((counts + tm - 1) // tm) * tm
    ends = jnp.cumsum(padded)
    starts = ends - padded
    dest = starts[e_flat] + rank
    n_tiles = (TOP_K * t) // tm + n_experts
    tok = jnp.zeros((n_tiles * tm,), jnp.int32).at[dest].set(jnp.arange(TOP_K * t, dtype=jnp.int32) // TOP_K)
    tile_row0 = jnp.arange(n_tiles, dtype=jnp.int32) * tm
    tile_expert = jnp.sum((ends[None, :] <= tile_row0[:, None]).astype(jnp.int32), axis=1)
    tile_expert = jnp.minimum(tile_expert, n_experts - 1)
    n_active = (ends[-1] // tm).astype(jnp.int32).reshape(1)
    active = jnp.arange(n_tiles) < n_active[0]
    tile_rows = jnp.clip((starts + counts)[tile_expert] - tile_row0, 0, tm)
    tile_rows = jnp.where(active, tile_rows, 0).astype(jnp.int32)
    last = jnp.maximum(n_active[0] - 1, 0)
    tile_expert = jnp.where(active, tile_expert, tile_expert[last])
    pos = dest.reshape(t, TOP_K)
    return tok.reshape(n_tiles, 1, tm), tile_expert, tile_rows, n_active, pos[:, 0], pos[:, 1]


def _rope_tables(lc, n):
    rows = n // GRID_W
    row = jnp.repeat(jnp.arange(rows), GRID_W).astype(F32)
    col = jnp.tile(jnp.arange(GRID_W), rows).astype(F32)
    axis_dim = HEAD_DIM // 2
    inv = ROPE_THETA ** (-jnp.arange(axis_dim // 2, dtype=F32) / (axis_dim // 2))
    ang_r = row[:, None] * inv[None, :]
    ang_c = col[:, None] * inv[None, :]
    ang = jnp.concatenate([ang_r, ang_r, ang_c, ang_c], axis=-1)
    cos, sin = jnp.cos(ang), jnp.sin(ang)
    first_half = (jnp.arange(HEAD_DIM) % (HEAD_DIM // 2)) < (HEAD_DIM // 4)
    sa = jnp.where(first_half[None, :], -sin, 0.0)
    sb = jnp.where(first_half[None, :], 0.0, sin)
    ident = lambda v: jnp.full((lc, HEAD_DIM), v, F32)
    full = lambda ctx_v, t: jnp.tile(jnp.concatenate([ident(ctx_v), t], axis=0), (1, LANES // HEAD_DIM))
    return full(1.0, cos), full(0.0, sa), full(0.0, sb)


def _pick_tile(n, candidates):
    for c in candidates:
        if n % c == 0:
            return c
    raise ValueError(f"no tile size among {candidates} divides {n}")


def kernel(x, c, ctx, c_ctx, w_ada, b_ada, g_norm1, g_norm2, w_in, w_dw, b_dw, ln_g, ln_b, lam_q1, lam_k1, lam_q2, lam_k2, g_subln, sink, w_out, w_ff_gate, w_ff_up, w_ff_down, w_router, w_ex_gate, w_ex_up, w_ex_down, g_final):
    b, n, d = x.shape
    lc = ctx.shape[1]
    s = lc + n
    depth = w_ada.shape[0]
    conv_ch = w_dw.shape[-1]
    n_swa_q = sink.shape[1] * HEAD_DIM
    n_diff_v = d - conv_ch - n_swa_q
    n_kv = n_swa_q // 3
    sizes = (conv_ch, n_diff_v, n_swa_q, n_diff_v, n_diff_v, n_kv, n_kv)
    assert sum(sizes) + conv_ch == w_in.shape[2]
    n_experts = w_router.shape[2]

    tm = 256
    assert lc % tm == 0 and n % tm == 0
    tm_ffn = _pick_tile(s, (768, 512, 256))
    tf_dense = _pick_tile(w_ff_gate.shape[2], (512, 256, 128))
    tf_moe = _pick_tile(w_ex_gate.shape[3], (256, 128))
    tm_moe = 1024
    tm_gather = 512

    stream = (ctx, x)
    cvec = jnp.concatenate([c, c_ctx[None, :]], axis=0)
    mods = _ada(cvec, w_ada, b_ada).reshape(depth, b + 1, 6, d)
    tabs = _rope_tables(lc, n)

    out = None
    for l in range(depth):
        last = l == depth - 1
        lam_init = 0.8 - 0.6 * math.exp(-0.3 * l)
        mod = mods[l]
        lam_params = jnp.stack([lam_q1[l], lam_k1[l], lam_q2[l], lam_k2[l]], axis=0)
        y, qb, qc, kb, vb, kc, vc = _in_proj(stream, g_norm1[l], mod, w_in[l].astype(BF16), tabs, sizes, lc, tm)
        a = _conv_module(y, w_dw[l], b_dw[l], ln_g[l], ln_b[l], lc, tm)
        ob = _diff_attention(qb, kb, vb, lam_params, g_subln[l], lam_init, lc, not last, tq=256, unroll=3)
        oc = _swa(qc, kc, vc, sink[l], lc, not last, tq=WINDOW)
        xs, h2 = _out_proj(a, ob, oc, stream, w_out[l].astype(BF16), mod, g_norm2[l], lc, not last, tm,
                           F32 if l % 2 == 1 else BF16)
        i = l // 2
        if l % 2 == 0:
            assert not last, "a final dense layer would need its own final-norm epilogue"
            xs = _ffn_dense(h2, xs, w_ff_gate[i].astype(BF16), w_ff_up[i].astype(BF16),
                            w_ff_down[i].astype(BF16), mod, lc, tm_ffn, tf_dense)
            stream = (xs,)
        else:
            assert last, "the MoE layer carries the final norm and drops the context rows"
            h_flat = h2.reshape(b * n, d)
            idx, wts = _router(h_flat, w_router[i], tm)
            tok, tile_expert, tile_rows, n_active, pos0, pos1 = _dispatch_plan(idx[:, :TOP_K], n_experts, tm_moe)
            split = tm_moe // tm_gather
            xg = _gather_rows(h_flat, tok.reshape(-1, 1, tm_gather), n_active * split, tm_gather)
            yexp = _moe_ffn(xg, w_ex_gate[i], w_ex_up[i], w_ex_down[i], tile_expert, tile_rows, n_active,
                            tm_moe, tf_moe)
            nt = (b * n) // tm
            out = _combine(yexp, pos0.reshape(nt, 1, tm), pos1.reshape(nt, 1, tm), xs, wts, mod, g_final, tm)
    return out
```

```python
import functools
import math

import jax
import jax.numpy as jnp
from jax import lax
from jax.experimental import pallas as pl
from jax.experimental.pallas import tpu as pltpu

F32 = jnp.float32
BF16 = jnp.bfloat16

HEAD_DIM = 64
GRID_W = 64
ROPE_THETA = 10000.0
NORM_EPS = 1e-6
MASK_VALUE = -1e30
LOG2E = math.log2(math.e)
CONV_K = 31
CONV_HALO = 16
WINDOW = 128
TOP_K = 2
LANES = 128
SUBLANES = 8
ONES_ROWS = 16
VMEM_LIMIT = 56 * 1024 * 1024


def _cparams(sem):
    return pltpu.CompilerParams(dimension_semantics=sem, vmem_limit_bytes=VMEM_LIMIT)


def _rms_mod(x, g, shift, scale):
    y = x * lax.rsqrt(jnp.mean(x * x, axis=-1, keepdims=True) + NORM_EPS)
    return (y * g) * (1.0 + scale) + shift


def _sigmoid(x):
    return 1.0 / (1.0 + jnp.exp(-x))


def _ada_kernel(c_ref, w_ref, b_ref, o_ref):
    c = c_ref[...]
    s = (c * _sigmoid(c)).astype(BF16)
    o_ref[0] = jnp.dot(s, w_ref[0].astype(BF16), preferred_element_type=F32) + b_ref[0]


def _ada(cvec, w_ada, b_ada):
    depth, d, n6 = w_ada.shape
    rows = cvec.shape[0]
    tn = 1024
    return pl.pallas_call(
        _ada_kernel,
        out_shape=jax.ShapeDtypeStruct((depth, rows, n6), F32),
        grid=(depth, n6 // tn),
        in_specs=[pl.BlockSpec((rows, d), lambda l, j: (0, 0)),
                  pl.BlockSpec((1, d, tn), lambda l, j: (l, 0, j)),
                  pl.BlockSpec((1, 1, tn), lambda l, j: (l, 0, j))],
        out_specs=pl.BlockSpec((1, rows, tn), lambda l, j: (l, 0, j)),
        compiler_params=_cparams(("arbitrary", "arbitrary")),
        name="ada",
    )(cvec, w_ada, b_ada.reshape(depth, 1, n6))


def _in_kernel(xc_ref, xl_ref, g_ref, mod_ref, w_ref, cos_ref, sa_ref, sb_ref,
               y_ref, qb_ref, qc_ref, kb_ref, vb_ref, kc_ref, vc_ref, *, sizes, nctx):
    conv_ch, n_bq, n_cq, n_bk, n_bv, n_ck, n_cv = sizes
    x = jnp.where(pl.program_id(1) < nctx, xc_ref[0], xl_ref[0])
    hb = _rms_mod(x, g_ref[...], mod_ref[0, 0:1, :], mod_ref[0, 1:2, :]).astype(BF16)

    def mm(lo, width):
        return jnp.dot(hb, w_ref[:, lo:lo + width], preferred_element_type=F32)

    cos, sa, sb = cos_ref[...], sa_ref[...], sb_ref[...]

    def rope_store(u, out_ref, scale, transposed=False):
        for gi in range(u.shape[1] // LANES):
            ug = u[:, gi * LANES:(gi + 1) * LANES]
            r = (ug * cos + pltpu.roll(ug, LANES - 16, 1) * sa + pltpu.roll(ug, 16, 1) * sb) * scale
            if transposed:
                out_ref[0, gi * LANES:(gi + 1) * LANES, :] = r.T.astype(out_ref.dtype)
            else:
                out_ref[0, :, gi * LANES:(gi + 1) * LANES] = r.astype(out_ref.dtype)

    qscale = HEAD_DIM ** -0.5
    grp = LANES + ONES_ROWS

    def glu_store(u):
        y_ref[0] = u[:, :conv_ch] * _sigmoid(u[:, conv_ch:])

    def vb_store(v):
        for gi in range(n_bv // LANES):
            vb_ref[0, gi, 0, 0:LANES, :] = v[:, gi * LANES:(gi + 1) * LANES].T.astype(BF16)
            vb_ref[0, gi, 0, LANES:LANES + ONES_ROWS, :] = jnp.ones((ONES_ROWS, v.shape[0]), BF16)

    def vc_store(v):
        for gi in range(n_cv // LANES):
            vc_ref[0, gi * grp:gi * grp + LANES, :] = v[:, gi * LANES:(gi + 1) * LANES].T.astype(BF16)
            vc_ref[0, gi * grp + LANES:(gi + 1) * grp, :] = jnp.ones((ONES_ROWS, v.shape[0]), BF16)

    sections = [
        (2 * conv_ch, glu_store),
        (n_bq, functools.partial(rope_store, out_ref=qb_ref, scale=qscale * LOG2E, transposed=True)),
        (n_cq, functools.partial(rope_store, out_ref=qc_ref, scale=qscale * LOG2E, transposed=True)),
        (n_bk, functools.partial(rope_store, out_ref=kb_ref, scale=1.0)),
        (n_bv, vb_store),
        (n_ck, functools.partial(rope_store, out_ref=kc_ref, scale=1.0)),
        (n_cv, vc_store),
    ]
    off = 0
    pending = None
    for width, epilogue in sections:
        u = mm(off, width)
        off += width
        if pending is not None:
            pending[1](pending[0])
        pending = (u, epilogue)
    pending[1](pending[0])


def _stream_specs(parts, lc, tm, q_off=0):
    nctx = lc // tm
    if len(parts) == 1:
        arrs, lat_off = (parts[0], parts[0]), nctx
    else:
        arrs, lat_off = parts, 0
    d = arrs[0].shape[2]
    ctx_spec = pl.BlockSpec((1, tm, d), lambda bi, j: (bi, jnp.minimum(j + q_off, nctx - 1), 0))
    lat_spec = pl.BlockSpec((1, tm, d), lambda bi, j: (bi, jnp.maximum(j + q_off - nctx, 0) + lat_off, 0))
    return arrs, [ctx_spec, lat_spec]


def _in_proj(parts, g, mod, w_bf16, tabs, sizes, lc, tm):
    b, d = parts[0].shape[0], parts[0].shape[2]
    s = sum(p.shape[1] for p in parts)
    conv_ch, n_bq, n_cq, n_bk, n_bv, n_ck, n_cv = sizes
    d_in = w_bf16.shape[1]
    nctx = lc // tm
    n_mod = mod.shape[0]

    def row(bi, j):
        return (bi, j, 0)

    def modrow(bi, j):
        return (jnp.where(j < nctx, n_mod - 1, bi), 0, 0)

    widths = (conv_ch, n_bq, n_cq, n_bk, n_bv, n_ck, n_cv)
    dts = (F32, BF16, BF16, BF16, BF16, BF16, BF16)
    out_shape = [jax.ShapeDtypeStruct((b, s, w), dt) for w, dt in zip(widths, dts)]
    out_specs = [pl.BlockSpec((1, tm, w), row) for w in widths]
    n_vh = n_bv // LANES
    out_shape[1] = jax.ShapeDtypeStruct((b, n_bq, s), BF16)
    out_specs[1] = pl.BlockSpec((1, n_bq, tm), lambda bi, j: (bi, 0, j))
    out_shape[4] = jax.ShapeDtypeStruct((b, n_vh, s // tm, LANES + ONES_ROWS, tm), BF16)
    out_specs[4] = pl.BlockSpec((1, n_vh, 1, LANES + ONES_ROWS, tm), lambda bi, j: (bi, 0, j, 0, 0))
    for idx, wdt in ((2, n_cq), (6, (n_cv // LANES) * (LANES + ONES_ROWS))):
        out_shape[idx] = jax.ShapeDtypeStruct((b, wdt, s), BF16)
        out_specs[idx] = pl.BlockSpec((1, wdt, tm), lambda bi, j: (bi, 0, j))
    tab_spec = pl.BlockSpec((tm, LANES), lambda bi, j: (j, 0))
    stream, stream_specs = _stream_specs(parts, lc, tm)
    return pl.pallas_call(
        functools.partial(_in_kernel, sizes=sizes, nctx=nctx),
        out_shape=out_shape,
        grid=(b, s // tm),
        in_specs=stream_specs + [
                  pl.BlockSpec((1, d), lambda bi, j: (0, 0)),
                  pl.BlockSpec((1, 6, d), modrow),
                  pl.BlockSpec((d, d_in), lambda bi, j: (0, 0)),
                  tab_spec, tab_spec, tab_spec],
        out_specs=out_specs,
        compiler_params=_cparams(("arbitrary", "arbitrary")),
        name="in_proj",
    )(*stream, g.reshape(1, d), mod, w_bf16, *tabs)


def _conv_kernel(yp_ref, yc_ref, yn_ref, w_ref, b_ref, g_ref, bb_ref, o_ref, win_ref, *, tm, seg_tile, sub):
    j = pl.program_id(1)
    nj = pl.num_programs(1)
    has_prev = jnp.logical_and(j > 0, j != seg_tile)
    has_next = jnp.logical_and(j + 1 < nj, j + 1 != seg_tile)
    h = CONV_HALO
    win_ref[0, 0:h, :] = jnp.where(has_prev, yp_ref[0], 0.0)
    win_ref[0, h:h + tm, :] = yc_ref[0]
    win_ref[0, h + tm:h + tm + h, :] = jnp.where(has_next, yn_ref[0], 0.0)
    span = tm + 2 * h - SUBLANES
    for r in range(1, SUBLANES):
        win_ref[r, 0:span, :] = win_ref[0, pl.ds(r, span), :]
    base = h - CONV_K // 2
    for r0 in range(0, tm, sub):
        acc = jnp.zeros((sub, win_ref.shape[2]), F32)
        for t in range(CONV_K):
            o = base + r0 + t
            acc = acc + w_ref[t:t + 1, :] * win_ref[o % SUBLANES, o - o % SUBLANES:o - o % SUBLANES + sub, :]
        y = acc + b_ref[...]
        mu = jnp.mean(y, axis=-1, keepdims=True)
        yc = y - mu
        var = jnp.mean(yc * yc, axis=-1, keepdims=True)
        z = yc * lax.rsqrt(var + NORM_EPS) * g_ref[...] + bb_ref[...]
        o_ref[0, r0:r0 + sub, :] = (z * _sigmoid(z)).astype(o_ref.dtype)


def _conv_module(y, w_dw, b_dw, ln_g, ln_b, lc, tm):
    b, s, ch = y.shape
    h = CONV_HALO
    per = tm // h
    nblk = s // h

    def prev(bi, j):
        return (bi, jnp.maximum(j * per - 1, 0), 0)

    def nxt(bi, j):
        return (bi, jnp.minimum((j + 1) * per, nblk - 1), 0)

    vec = pl.BlockSpec((1, ch), lambda bi, j: (0, 0))
    return pl.pallas_call(
        functools.partial(_conv_kernel, tm=tm, seg_tile=lc // tm, sub=32),
        out_shape=jax.ShapeDtypeStruct((b, s, ch), BF16),
        grid=(b, s // tm),
        in_specs=[pl.BlockSpec((1, h, ch), prev),
                  pl.BlockSpec((1, tm, ch), lambda bi, j: (bi, j, 0)),
                  pl.BlockSpec((1, h, ch), nxt),
                  pl.BlockSpec((CONV_K, ch), lambda bi, j: (0, 0)),
                  vec, vec, vec],
        out_specs=pl.BlockSpec((1, tm, ch), lambda bi, j: (bi, j, 0)),
        scratch_shapes=[pltpu.VMEM((SUBLANES, tm + 2 * h, ch), F32)],
        compiler_params=_cparams(("arbitrary", "arbitrary")),
        name="conv_module",
    )(y, y, y, w_dw.reshape(CONV_K, ch), b_dw.reshape(1, ch), ln_g.reshape(1, ch), ln_b.reshape(1, ch))


def _diff_kernel(lamp_ref, qt_ref, k_ref, vt_ref, g_ref, o_ref, m_scr, acc_scr, sa_scr, sb_scr, sc_scr, rhs_scr,
                 *, tq, tk, unroll, context_only, ctx_chunks, all_chunks, lam_init):
    qt = qt_ref[0]
    sub = lax.broadcasted_iota(jnp.int32, qt.shape, 0)
    zero = jnp.zeros_like(qt)
    rhs_scr[:, 0:tq] = jnp.where(sub < HEAD_DIM, qt, zero)
    rhs_scr[:, tq:2 * tq] = jnp.where(sub >= HEAD_DIM, qt, zero)
    m_scr[...] = jnp.full(m_scr.shape, MASK_VALUE, F32)
    acc_scr[...] = jnp.zeros(acc_scr.shape, F32)

    def scores(gidx, count, s_ref):
        mxs = []
        for half in range(2):
            cs = slice(half * tq, (half + 1) * tq)
            mx = None
            for u in range(count):
                start = pl.multiple_of((gidx * count + u) * tk, tk)
                s = jnp.dot(k_ref[0, pl.ds(start, tk), :], rhs_scr[:, cs], preferred_element_type=F32)
                s_ref[u * tk:(u + 1) * tk, cs] = s
                cm = jnp.max(s, axis=0, keepdims=True)
                mx = cm if mx is None else jnp.maximum(mx, cm)
            mxs.append(mx)
        return mxs

    def accumulate(gidx, count, s_ref, mxs):
        for half in range(2):
            cs = slice(half * tq, (half + 1) * tq)
            m_prev = m_scr[:, cs]
            m_new = jnp.maximum(m_prev, mxs[half])
            alpha = jnp.exp2(m_prev - m_new)
            pv = None
            for u in range(count):
                p = jnp.exp2((s_ref[u * tk:(u + 1) * tk, cs] - m_new).astype(BF16))
                d = jnp.dot(vt_ref[0, 0, gidx * count + u], p, preferred_element_type=F32)
                pv = d if pv is None else pv + d
            acc_scr[:, cs] = alpha * acc_scr[:, cs] + pv
            m_scr[:, cs] = m_new

    bufs = (sa_scr, sb_scr, sc_scr)

    def accumulate_and_score(g_acc, g_new, mxs):
        s_acc, s_new = bufs[g_acc % 3], bufs[g_new % 3]
        new_mxs = []
        for half in range(2):
            cs = slice(half * tq, (half + 1) * tq)
            m_prev = m_scr[:, cs]
            m_new = jnp.maximum(m_prev, mxs[half])
            alpha = jnp.exp2(m_prev - m_new)
            pv = None
            mx = None
            for u in range(unroll):
                p = jnp.exp2((s_acc[u * tk:(u + 1) * tk, cs] - m_new).astype(BF16))
                d = jnp.dot(vt_ref[0, 0, g_acc * unroll + u], p, preferred_element_type=F32)
                pv = d if pv is None else pv + d
                start = pl.multiple_of((g_new * unroll + u) * tk, tk)
                s = jnp.dot(k_ref[0, pl.ds(start, tk), :], rhs_scr[:, cs], preferred_element_type=F32)
                s_new[u * tk:(u + 1) * tk, cs] = s
                cm = jnp.max(s, axis=0, keepdims=True)
                mx = cm if mx is None else jnp.maximum(mx, cm)
            acc_scr[:, cs] = alpha * acc_scr[:, cs] + pv
            m_scr[:, cs] = m_new
            new_mxs.append(mx)
        return new_mxs

    def context_keys_only():
        for c in range(ctx_chunks):
            accumulate(c, 1, sa_scr, scores(c, 1, sa_scr))

    def all_keys():
        n_groups = all_chunks // unroll
        mx = [scores(g, unroll, bufs[g % 3]) for g in range(min(2, n_groups))]
        for g in range(n_groups):
            if g + 2 < n_groups:
                mx.append(accumulate_and_score(g, g + 2, mx[g]))
            else:
                accumulate(g, unroll, bufs[g % 3], mx[g])

    if context_only:
        context_keys_only()
    else:
        all_keys()

    vdim = 2 * HEAD_DIM
    o = acc_scr[0:vdim, :] / acc_scr[vdim:vdim + 1, :]
    lp = lamp_ref[...]
    lam = (jnp.exp(jnp.sum(lp[0:1] * lp[1:2], axis=1, keepdims=True))
           - jnp.exp(jnp.sum(lp[2:3] * lp[3:4], axis=1, keepdims=True)) + lam_init)
    od = o[:, :tq] - lam * o[:, tq:]
    y = od * lax.rsqrt(jnp.mean(od * od, axis=0, keepdims=True) + NORM_EPS)
    o_ref[0] = (y * g_ref[...] * (1.0 - lam_init)).T.astype(o_ref.dtype)


def _diff_attention(qbt, kb, vbt, lam_params, g_subln, lam_init, lc, context_queries, tq, unroll):
    b, s, w = kb.shape
    heads = w // (2 * HEAD_DIM)
    vrows, tk = vbt.shape[3], vbt.shape[4]
    assert vrows == 2 * HEAD_DIM + ONES_ROWS
    q_off = 0 if context_queries else lc // tq
    nq = lc // tq if context_queries else (s - lc) // tq
    n_keys = lc if context_queries else s
    assert (s // tk) % unroll == 0 and lc % tk == 0
    kern = functools.partial(_diff_kernel, tq=tq, tk=tk, unroll=unroll, context_only=context_queries,
                             ctx_chunks=lc // tk, all_chunks=s // tk, lam_init=lam_init)
    return pl.pallas_call(
        kern,
        out_shape=jax.ShapeDtypeStruct((b, nq * tq, w), BF16),
        grid=(b, heads, nq),
        in_specs=[pl.BlockSpec((4, HEAD_DIM), lambda bi, h, i: (0, 0)),
                  pl.BlockSpec((1, 2 * HEAD_DIM, tq), lambda bi, h, i: (bi, h, i + q_off)),
                  pl.BlockSpec((1, n_keys, 2 * HEAD_DIM), lambda bi, h, i: (bi, 0, h)),
                  pl.BlockSpec((1, 1, n_keys // tk, vrows, tk), lambda bi, h, i: (bi, h, 0, 0, 0)),
                  pl.BlockSpec((2 * HEAD_DIM, 1), lambda bi, h, i: (0, 0))],
        out_specs=pl.BlockSpec((1, tq, 2 * HEAD_DIM), lambda bi, h, i: (bi, i, h)),
        scratch_shapes=[pltpu.VMEM((1, 2 * tq), F32),
                        pltpu.VMEM((vrows, 2 * tq), F32),
                        pltpu.VMEM((unroll * tk, 2 * tq), F32), pltpu.VMEM((unroll * tk, 2 * tq), F32),
                        pltpu.VMEM((unroll * tk, 2 * tq), F32),
                        pltpu.VMEM((2 * HEAD_DIM, 2 * tq), BF16)],
        compiler_params=_cparams(("arbitrary", "arbitrary", "arbitrary")),
        name="diff_attention",
    )(lam_params, qbt, kb, vbt, g_subln.reshape(2 * HEAD_DIM, 1))


def _swa_kernel(sink_ref, qt_ref, kctx_ref, vtctx_ref, kp_ref, kc_ref, kn_ref, vtp_ref, vtc_ref, vtn_ref, o_ref,
                *, tq, q_off, n_ctx_qblk, n_lat, kv_heads, group):
    i = pl.program_id(1) + q_off
    is_lat = i >= n_ctx_qblk
    q0 = (i - n_ctx_qblk) * tq
    wq = group * tq
    kpos = q0 - tq + lax.broadcasted_iota(jnp.int32, (3 * tq, wq), 0)
    qpos = q0 + jnp.bitwise_and(lax.broadcasted_iota(jnp.int32, (3 * tq, wq), 1), tq - 1)
    valid = jnp.logical_and(jnp.abs(qpos - kpos) <= WINDOW, jnp.logical_and(kpos >= 0, kpos < n_lat))
    valid = jnp.logical_and(valid, is_lat)
    qt = qt_ref[0]
    kloc = jnp.concatenate([kp_ref[0], kc_ref[0], kn_ref[0]], axis=0)
    vtloc = jnp.concatenate([vtp_ref[0], vtc_ref[0], vtn_ref[0]], axis=1)
    kctx = kctx_ref[0]
    vtctx = vtctx_ref[0]
    zeros = jnp.zeros((HEAD_DIM, tq), qt.dtype)
    heads_out = []

    def head_scores(h):
        gs = slice((h // 2) * LANES, (h // 2 + 1) * LANES)
        upper = h % 2 == 1
        cols = []
        for gi in range(group):
            hq = h * group + gi
            qh = qt[hq * HEAD_DIM:(hq + 1) * HEAD_DIM, :]
            cols.append(jnp.concatenate([zeros, qh] if upper else [qh, zeros], axis=0))
        rhs = jnp.concatenate(cols, axis=1)
        sink = jnp.concatenate(
            [jnp.full((1, tq), sink_ref[h * group + gi] * LOG2E, F32) for gi in range(group)], axis=1)
        s_ctx = jnp.dot(kctx[:, gs], rhs, preferred_element_type=F32)
        s_loc = jnp.dot(kloc[:, gs], rhs, preferred_element_type=F32)
        s_loc = jnp.where(valid, s_loc, MASK_VALUE)
        m = jnp.maximum(jnp.maximum(jnp.max(s_ctx, axis=0, keepdims=True), jnp.max(s_loc, axis=0, keepdims=True)), sink)
        return s_ctx, s_loc, m, sink

    def head_output(h, s_ctx, s_loc, m, sink):
        vs = slice((h // 2) * (LANES + ONES_ROWS), (h // 2 + 1) * (LANES + ONES_ROWS))
        upper = h % 2 == 1
        e_ctx = jnp.exp2((s_ctx - m).astype(BF16))
        e_loc = jnp.exp2((s_loc - m).astype(BF16))
        ot = (jnp.dot(vtctx[vs, :], e_ctx, preferred_element_type=F32)
              + jnp.dot(vtloc[vs, :], e_loc, preferred_element_type=F32))
        denom = ot[LANES:LANES + 1, :] + jnp.exp2(sink - m)
        ot = (ot[HEAD_DIM:LANES, :] if upper else ot[:HEAD_DIM, :]) / denom
        for gi in range(group):
            heads_out.append(ot[:, gi * tq:(gi + 1) * tq])

    cur = head_scores(0)
    for h in range(kv_heads):
        nxt = head_scores(h + 1) if h + 1 < kv_heads else None
        head_output(h, *cur)
        cur = nxt
    for pi in range(len(heads_out) // 2):
        pair = jnp.concatenate([heads_out[2 * pi], heads_out[2 * pi + 1]], axis=0)
        o_ref[0, :, pi * LANES:(pi + 1) * LANES] = pair.T.astype(o_ref.dtype)


def _swa(qct, kc, vct, sink, lc, with_ctx, tq):
    b, s, kvw = kc.shape
    w = qct.shape[1]
    kv_heads = kvw // HEAD_DIM
    group = w // kvw
    assert tq & (tq - 1) == 0 and kv_heads % 2 == 0
    q_off = 0 if with_ctx else lc // tq
    nq = s // tq - q_off
    nblk = s // tq

    def rows(f):
        return pl.BlockSpec((1, tq, kvw), lambda bi, i: (bi, f(i + q_off), 0))

    vrows = vct.shape[1]
    assert vrows == (kvw // LANES) * (LANES + ONES_ROWS)

    def cols(f):
        return pl.BlockSpec((1, vrows, tq), lambda bi, i: (bi, 0, f(i + q_off)))

    cur = lambda i: i
    prev = lambda i: jnp.maximum(i - 1, 0)
    nxt = lambda i: jnp.minimum(i + 1, nblk - 1)
    kern = functools.partial(_swa_kernel, tq=tq, q_off=q_off, n_ctx_qblk=lc // tq, n_lat=s - lc,
                             kv_heads=kv_heads, group=group)
    return pl.pallas_call(
        kern,
        out_shape=jax.ShapeDtypeStruct((b, nq * tq, w), BF16),
        grid=(b, nq),
        in_specs=[pl.BlockSpec(memory_space=pltpu.SMEM),
                  pl.BlockSpec((1, w, tq), lambda bi, i: (bi, 0, i + q_off)),
                  pl.BlockSpec((1, lc, kvw), lambda bi, i: (bi, 0, 0)),
                  pl.BlockSpec((1, vrows, lc), lambda bi, i: (bi, 0, 0)),
                  rows(prev), rows(cur), rows(nxt), cols(prev), cols(cur), cols(nxt)],
        out_specs=pl.BlockSpec((1, tq, w), lambda bi, i: (bi, i, 0)),
        compiler_params=_cparams(("arbitrary", "arbitrary")),
        name="swa",
    )(sink, qct, kc, vct, kc, kc, kc, vct, vct, vct)


def _out_kernel(a_ref, ob_ref, oc_ref, xc_ref, xl_ref, w_ref, mod_ref, g_ref, xo_ref, h_ref, *, n_ctx_tiles):
    na, nb = a_ref.shape[2], ob_ref.shape[2]
    mix = jnp.dot(a_ref[0], w_ref[0:na, :], preferred_element_type=F32)
    mix = mix + jnp.dot(ob_ref[0], w_ref[na:na + nb, :], preferred_element_type=F32)
    mix = mix + jnp.dot(oc_ref[0], w_ref[na + nb:, :], preferred_element_type=F32)
    x = jnp.where(pl.program_id(1) < n_ctx_tiles, xc_ref[0], xl_ref[0])
    xn = x + mod_ref[0, 2:3, :] * mix
    xo_ref[0] = xn
    h_ref[0] = _rms_mod(xn, g_ref[...], mod_ref[0, 3:4, :], mod_ref[0, 4:5, :]).astype(h_ref.dtype)


def _out_proj(a, ob, oc, parts, w_bf16, mod, g2, lc, with_ctx, tm, h_dtype):
    b, d = parts[0].shape[0], parts[0].shape[2]
    s = sum(p.shape[1] for p in parts)
    nctx = lc // tm
    q_off = 0 if with_ctx else nctx
    nt = s // tm - q_off
    n_mod = mod.shape[0]
    out_rows = s - q_off * tm

    def row(bi, j):
        return (bi, j + q_off, 0)

    def modrow(bi, j):
        return (jnp.where(j + q_off < nctx, n_mod - 1, bi), 0, 0)

    def own(bi, j):
        return (bi, j, 0)

    assert ob.shape[1] == out_rows and oc.shape[1] == out_rows
    stream, stream_specs = _stream_specs(parts, lc, tm, q_off)
    return pl.pallas_call(
        functools.partial(_out_kernel, n_ctx_tiles=nctx - q_off),
        out_shape=(jax.ShapeDtypeStruct((b, out_rows, d), F32), jax.ShapeDtypeStruct((b, out_rows, d), h_dtype)),
        grid=(b, nt),
        in_specs=[pl.BlockSpec((1, tm, a.shape[2]), row), pl.BlockSpec((1, tm, ob.shape[2]), own),
                  pl.BlockSpec((1, tm, oc.shape[2]), own)] + stream_specs + [
                  pl.BlockSpec(w_bf16.shape, lambda bi, j: (0, 0)),
                  pl.BlockSpec((1, 6, d), modrow),
                  pl.BlockSpec((1, d), lambda bi, j: (0, 0))],
        out_specs=(pl.BlockSpec((1, tm, d), own), pl.BlockSpec((1, tm, d), own)),
        compiler_params=_cparams(("arbitrary", "arbitrary")),
        name="out_proj",
    )(a, ob, oc, *stream, w_bf16, mod, g2.reshape(1, d))


def _ffn_kernel(h_ref, x_ref, wg_ref, wu_ref, wd_ref, modb_ref, modc_ref, o_ref, *, tm, lc):
    f = pl.program_id(2)

    @pl.when(f == 0)
    def _():
        o_ref[...] = jnp.zeros(o_ref.shape, F32)

    h = h_ref[0]
    g = jnp.dot(h, wg_ref[...], preferred_element_type=F32)
    u = jnp.dot(h, wu_ref[...], preferred_element_type=F32)
    act = (g * _sigmoid(g) * u).astype(BF16)
    o_ref[0] += jnp.dot(act, wd_ref[...], preferred_element_type=F32)

    @pl.when(f == pl.num_programs(2) - 1)
    def _():
        rows = pl.program_id(1) * tm + lax.broadcasted_iota(jnp.int32, (tm, 1), 0)
        gate = jnp.where(rows < lc, modc_ref[0, 5:6, :], modb_ref[0, 5:6, :])
        o_ref[0] = x_ref[0] + gate * o_ref[0]


def _ffn_dense(h, x, wg, wu, wd, mod, lc, tm, tf):
    b, s, d = x.shape
    dff = wg.shape[1]
    n_mod = mod.shape[0]
    row = lambda bi, j, f: (bi, j, 0)
    return pl.pallas_call(
        functools.partial(_ffn_kernel, tm=tm, lc=lc),
        out_shape=jax.ShapeDtypeStruct((b, s, d), F32),
        grid=(b, s // tm, dff // tf),
        in_specs=[pl.BlockSpec((1, tm, d), row), pl.BlockSpec((1, tm, d), row),
                  pl.BlockSpec((d, tf), lambda bi, j, f: (0, f)),
                  pl.BlockSpec((d, tf), lambda bi, j, f: (0, f)),
                  pl.BlockSpec((tf, d), lambda bi, j, f: (f, 0)),
                  pl.BlockSpec((1, 6, d), lambda bi, j, f: (bi, 0, 0)),
                  pl.BlockSpec((1, 6, d), lambda bi, j, f: (n_mod - 1, 0, 0))],
        out_specs=pl.BlockSpec((1, tm, d), row),
        compiler_params=_cparams(("arbitrary", "arbitrary", "arbitrary")),
        name="ffn_dense",
    )(h, x, wg, wu, wd, mod, mod)


def _router_kernel(h_ref, w_ref, idx_ref, wt_ref):
    h = h_ref[...]
    w = w_ref[...]
    h_hi = h.astype(BF16)
    w_hi = w.astype(BF16)
    h_lo = (h - h_hi.astype(F32)).astype(BF16)
    w_lo = (w - w_hi.astype(F32)).astype(BF16)
    logits = (jnp.dot(h_hi, w_hi, preferred_element_type=F32) + jnp.dot(h_hi, w_lo, preferred_element_type=F32)
              + jnp.dot(h_lo, w_hi, preferred_element_type=F32))
    n_e = logits.shape[1]
    lane = lax.broadcasted_iota(jnp.int32, logits.shape, 1)
    m1 = jnp.max(logits, axis=1, keepdims=True)
    i1 = jnp.min(jnp.where(logits == m1, lane, n_e), axis=1, keepdims=True)
    rest = jnp.where(lane == i1, -jnp.inf, logits)
    m2 = jnp.max(rest, axis=1, keepdims=True)
    i2 = jnp.min(jnp.where(rest == m2, lane, n_e), axis=1, keepdims=True)
    e2 = jnp.exp(m2 - m1)
    den = 1.0 + e2
    out_lane = lax.broadcasted_iota(jnp.int32, idx_ref.shape, 1)
    idx_ref[...] = jnp.where(out_lane == 0, i1, jnp.where(out_lane == 1, i2, 0))
    wt_ref[...] = jnp.where(out_lane == 0, 1.0 / den, jnp.where(out_lane == 1, e2 / den, 0.0))


def _router(h_flat, w_router, tm):
    t, d = h_flat.shape
    n_e = w_router.shape[1]
    return pl.pallas_call(
        _router_kernel,
        out_shape=(jax.ShapeDtypeStruct((t, LANES), jnp.int32), jax.ShapeDtypeStruct((t, LANES), F32)),
        grid=(t // tm,),
        in_specs=[pl.BlockSpec((tm, d), lambda i: (i, 0)), pl.BlockSpec((d, n_e), lambda i: (0, 0))],
        out_specs=(pl.BlockSpec((tm, LANES), lambda i: (i, 0)), pl.BlockSpec((tm, LANES), lambda i: (i, 0))),
        compiler_params=_cparams(("arbitrary",)),
        name="router",
    )(h_flat, w_router)


def _row_copy(src_hbm, row, dst, r, sem):
    return pltpu.make_async_copy(src_hbm.at[pl.ds(row, 1)], dst.at[pl.ds(r, 1)], sem)


def _start_rows(src_hbm, idx_ref, dst, sem, n):
    def body(r, carry):
        _row_copy(src_hbm, idx_ref[0, 0, r], dst, r, sem).start()
        return carry

    lax.fori_loop(0, n, body, 0, unroll=8)


def _wait_rows(src_hbm, dst, sem, n):
    def body(r, carry):
        _row_copy(src_hbm, 0, dst, r, sem).wait()
        return carry

    lax.fori_loop(0, n, body, 0, unroll=8)


def _gather_kernel(nact_ref, tok_ref, tok_next_ref, h_hbm, o_ref, buf, sem, *, tm):
    i = pl.program_id(0)
    nact = nact_ref[0]
    slot = lax.rem(i, 2)

    @pl.when(jnp.logical_and(i == 0, nact > 0))
    def _():
        _start_rows(h_hbm, tok_ref, buf.at[0], sem.at[0], tm)

    @pl.when(i + 1 < nact)
    def _():
        _start_rows(h_hbm, tok_next_ref, buf.at[1 - slot], sem.at[1 - slot], tm)

    @pl.when(i < nact)
    def _():
        _wait_rows(h_hbm, buf.at[slot], sem.at[slot], tm)
        o_ref[...] = buf[slot].astype(o_ref.dtype)

    @pl.when(i >= nact)
    def _():
        o_ref[...] = jnp.zeros(o_ref.shape, o_ref.dtype)


def _gather_rows(h_flat, tok, n_active, tm):
    n_tiles = tok.shape[0]
    d = h_flat.shape[1]
    grid_spec = pltpu.PrefetchScalarGridSpec(
        num_scalar_prefetch=1,
        grid=(n_tiles,),
        in_specs=[pl.BlockSpec((1, 1, tm), lambda i, na: (i, 0, 0), memory_space=pltpu.SMEM),
                  pl.BlockSpec((1, 1, tm), lambda i, na: (jnp.minimum(i + 1, n_tiles - 1), 0, 0),
                               memory_space=pltpu.SMEM),
                  pl.BlockSpec(memory_space=pl.ANY)],
        out_specs=pl.BlockSpec((tm, d), lambda i, na: (i, 0)),
        scratch_shapes=[pltpu.VMEM((2, tm, d), h_flat.dtype), pltpu.SemaphoreType.DMA((2,))],
    )
    return pl.pallas_call(
        functools.partial(_gather_kernel, tm=tm),
        out_shape=jax.ShapeDtypeStruct((n_tiles * tm, d), BF16),
        grid_spec=grid_spec,
        compiler_params=_cparams(("arbitrary",)),
        name="moe_gather",
    )(n_active, tok, tok, h_flat)


def _moe_ffn_kernel(te_ref, rows_ref, nact_ref, x_ref, wg_ref, wu_ref, wd_ref, o_ref):
    i = pl.program_id(0)
    f = pl.program_id(1)
    half = o_ref.shape[0] // 2

    @pl.when(f == 0)
    def _():
        o_ref[...] = jnp.zeros(o_ref.shape, o_ref.dtype)

    def swiglu_rows(n_rows):
        x = x_ref[0:n_rows, :]
        g = jnp.dot(x, wg_ref[0].astype(BF16), preferred_element_type=F32)
        u = jnp.dot(x, wu_ref[0].astype(BF16), preferred_element_type=F32)
        act = (g * _sigmoid(g) * u).astype(BF16)
        o_ref[0:n_rows, :] += jnp.dot(act, wd_ref[0].astype(BF16), preferred_element_type=F32)

    rows = rows_ref[i]
    pl.when(rows > half)(functools.partial(swiglu_rows, o_ref.shape[0]))
    pl.when(jnp.logical_and(rows > 0, rows <= half))(functools.partial(swiglu_rows, half))


def _moe_ffn(xs, wg, wu, wd, tile_expert, tile_rows, n_active, tm, tf):
    p, d = xs.shape
    dff = wg.shape[2]
    nf = dff // tf
    n_tiles = p // tm

    def tile(i, f, te, tr, na):
        return (jnp.minimum(i, na[0] - 1), 0)

    def fcol(i, f, na):
        return jnp.where(i < na[0], f, nf - 1)

    grid_spec = pltpu.PrefetchScalarGridSpec(
        num_scalar_prefetch=3,
        grid=(n_tiles, nf),
        in_specs=[pl.BlockSpec((tm, d), tile),
                  pl.BlockSpec((1, d, tf), lambda i, f, te, tr, na: (te[i], 0, fcol(i, f, na))),
                  pl.BlockSpec((1, d, tf), lambda i, f, te, tr, na: (te[i], 0, fcol(i, f, na))),
                  pl.BlockSpec((1, tf, d), lambda i, f, te, tr, na: (te[i], fcol(i, f, na), 0))],
        out_specs=pl.BlockSpec((tm, d), lambda i, f, te, tr, na: (i, 0)),
    )
    return pl.pallas_call(
        _moe_ffn_kernel,
        out_shape=jax.ShapeDtypeStruct((p, d), F32),
        grid_spec=grid_spec,
        compiler_params=_cparams(("arbitrary", "arbitrary")),
        name="moe_ffn",
    )(tile_expert, tile_rows, n_active, xs, wg, wu, wd)


def _combine_kernel(p0_ref, p1_ref, p0n_ref, p1n_ref, y_hbm, x_ref, wt_ref, mod_ref, g_ref, o_ref, buf, sem, *, tm):
    t = pl.program_id(0) * pl.num_programs(1) + pl.program_id(1)
    n_tiles = pl.num_programs(0) * pl.num_programs(1)
    slot = lax.rem(t, 2)

    def start(r0, r1, s):
        _start_rows(y_hbm, r0, buf.at[s, 0], sem.at[s, 0], tm)
        _start_rows(y_hbm, r1, buf.at[s, 1], sem.at[s, 1], tm)

    @pl.when(t == 0)
    def _():
        start(p0_ref, p1_ref, 0)

    @pl.when(t + 1 < n_tiles)
    def _():
        start(p0n_ref, p1n_ref, 1 - slot)

    _wait_rows(y_hbm, buf.at[slot, 0], sem.at[slot, 0], tm)
    _wait_rows(y_hbm, buf.at[slot, 1], sem.at[slot, 1], tm)
    wt = wt_ref[...]
    f = wt[:, 0:1] * buf[slot, 0] + wt[:, 1:2] * buf[slot, 1]
    xn = x_ref[0] + mod_ref[0, 5:6, :] * f
    y = xn * lax.rsqrt(jnp.mean(xn * xn, axis=-1, keepdims=True) + NORM_EPS)
    o_ref[0] = y * g_ref[...]


def _combine(y, pos0, pos1, x, wts, mod, g_final, tm):
    b, n, d = x.shape
    nt = n // tm
    last = b * nt - 1
    idx = lambda f: pl.BlockSpec((1, 1, tm), lambda bi, j: (f(bi * nt + j), 0, 0), memory_space=pltpu.SMEM)
    cur = lambda t: t
    nxt = lambda t: jnp.minimum(t + 1, last)
    return pl.pallas_call(
        functools.partial(_combine_kernel, tm=tm),
        out_shape=jax.ShapeDtypeStruct((b, n, d), F32),
        grid=(b, nt),
        in_specs=[idx(cur), idx(cur), idx(nxt), idx(nxt),
                  pl.BlockSpec(memory_space=pl.ANY),
                  pl.BlockSpec((1, tm, d), lambda bi, j: (bi, j, 0)),
                  pl.BlockSpec((tm, LANES), lambda bi, j: (bi * nt + j, 0)),
                  pl.BlockSpec((1, 6, d), lambda bi, j: (bi, 0, 0)),
                  pl.BlockSpec((1, d), lambda bi, j: (0, 0))],
        out_specs=pl.BlockSpec((1, tm, d), lambda bi, j: (bi, j, 0)),
        scratch_shapes=[pltpu.VMEM((2, 2, tm, d), F32), pltpu.SemaphoreType.DMA((2, 2))],
        compiler_params=_cparams(("arbitrary", "arbitrary")),
        name="moe_combine",
    )(pos0, pos1, pos0, pos1, y, x, wts, mod, g_final.reshape(1, d))


def _dispatch_plan(idx, n_experts, tm):
    t = idx.shape[0]
    e_flat = idx.reshape(-1)
    onehot = (e_flat[:, None] == jnp.arange(n_experts, dtype=jnp.int32)[None, :]).astype(jnp.int32)
    rank = jnp.sum((jnp.cumsum(onehot, axis=0) - 1) * onehot, axis=1)
    counts = jnp.sum(onehot, axis=0)
    padded = ((counts + tm - 1) // tm) * tm
    ends = jnp.cumsum(padded)
    starts = ends - padded
    dest = starts[e_flat] + rank
    n_tiles = (TOP_K * t) // tm + n_experts
    tok = jnp.zeros((n_tiles * tm,), jnp.int32).at[dest].set(jnp.arange(TOP_K * t, dtype=jnp.int32) // TOP_K)
    tile_row0 = jnp.arange(n_tiles, dtype=jnp.int32) * tm
    tile_expert = jnp.sum((ends[None, :] <= tile_row0[:, None]).astype(jnp.int32), axis=1)
    tile_expert = jnp.minimum(tile_expert, n_experts - 1)
    n_active = (ends[-1] // tm).astype(jnp.int32).reshape(1)
    active = jnp.arange(n_tiles) < n_active[0]
    tile_rows = jnp.clip((starts + counts)[tile_expert] - tile_row0, 0, tm)
    tile_rows = jnp.where(active, tile_rows, 0).astype(jnp.int32)
    last = jnp.maximum(n_active[0] - 1, 0)
    tile_expert = jnp.where(active, tile_expert, tile_expert[last])
    pos = dest.reshape(t, TOP_K)
    return tok.reshape(n_tiles, 1, tm), tile_expert, tile_rows, n_active, pos[:, 0], pos[:, 1]


def _rope_tables(lc, n):
    rows = n // GRID_W
    row = jnp.repeat(jnp.arange(rows), GRID_W).astype(F32)
    col = jnp.tile(jnp.arange(GRID_W), rows).astype(F32)
    axis_dim = HEAD_DIM // 2
    inv = ROPE_THETA ** (-jnp.arange(axis_dim // 2, dtype=F32) / (axis_dim // 2))
    ang_r = row[:, None] * inv[None, :]
    ang_c = col[:, None] * inv[None, :]
    ang = jnp.concatenate([ang_r, ang_r, ang_c, ang_c], axis=-1)
    cos, sin = jnp.cos(ang), jnp.sin(ang)
    first_half = (jnp.arange(HEAD_DIM) % (HEAD_DIM // 2)) < (HEAD_DIM // 4)
    sa = jnp.where(first_half[None, :], -sin, 0.0)
    sb = jnp.where(first_half[None, :], 0.0, sin)
    ident = lambda v: jnp.full((lc, HEAD_DIM), v, F32)
    full = lambda ctx_v, t: jnp.tile(jnp.concatenate([ident(ctx_v), t], axis=0), (1, LANES // HEAD_DIM))
    return full(1.0, cos), full(0.0, sa), full(0.0, sb)


def _pick_tile(n, candidates):
    for c in candidates:
        if n % c == 0:
            return c
    raise ValueError(f"no tile size among {candidates} divides {n}")


def kernel(x, c, ctx, c_ctx, w_ada, b_ada, g_norm1, g_norm2, w_in, w_dw, b_dw, ln_g, ln_b, lam_q1, lam_k1, lam_q2, lam_k2, g_subln, sink, w_out, w_ff_gate, w_ff_up, w_ff_down, w_router, w_ex_gate, w_ex_up, w_ex_down, g_final):
    b, n, d = x.shape
    lc = ctx.shape[1]
    s = lc + n
    depth = w_ada.shape[0]
    conv_ch = w_dw.shape[-1]
    n_swa_q = sink.shape[1] * HEAD_DIM
    n_diff_v = d - conv_ch - n_swa_q
    n_kv = n_swa_q // 3
    sizes = (conv_ch, n_diff_v, n_swa_q, n_diff_v, n_diff_v, n_kv, n_kv)
    assert sum(sizes) + conv_ch == w_in.shape[2]
    n_experts = w_router.shape[2]

    tm = 256
    assert lc % tm == 0 and n % tm == 0
    tm_ffn = _pick_tile(s, (768, 512, 256))
    tf_dense = _pick_tile(w_ff_gate.shape[2], (512, 256, 128))
    tf_moe = _pick_tile(w_ex_gate.shape[3], (256, 128))
    tm_moe = 1024
    tm_gather = 512

    stream = (ctx, x)
    cvec = jnp.concatenate([c, c_ctx[None, :]], axis=0)
    mods = _ada(cvec, w_ada, b_ada).reshape(depth, b + 1, 6, d)
    tabs = _rope_tables(lc, n)

    out = None
    for l in range(depth):
        last = l == depth - 1
        lam_init = 0.8 - 0.6 * math.exp(-0.3 * l)
        mod = mods[l]
        lam_params = jnp.stack([lam_q1[l], lam_k1[l], lam_q2[l], lam_k2[l]], axis=0)
        y, qb, qc, kb, vb, kc, vc = _in_proj(stream, g_norm1[l], mod, w_in[l].astype(BF16), tabs, sizes, lc, tm)
        a = _conv_module(y, w_dw[l], b_dw[l], ln_g[l], ln_b[l], lc, tm)
        ob = _diff_attention(qb, kb, vb, lam_params, g_subln[l], lam_init, lc, False, tq=256, unroll=3)
        if not last:
            ob_ctx = _diff_attention(qb, kb, vb, lam_params, g_subln[l], lam_init, lc, True, tq=256, unroll=3)
            ob = jnp.concatenate([ob_ctx, ob], axis=1)
        oc = _swa(qc, kc, vc, sink[l], lc, not last, tq=WINDOW)
        xs, h2 = _out_proj(a, ob, oc, stream, w_out[l].astype(BF16), mod, g_norm2[l], lc, not last, tm,
                           F32 if l % 2 == 1 else BF16)
        i = l // 2
        if l % 2 == 0:
            assert not last, "a final dense layer would need its own final-norm epilogue"
            xs = _ffn_dense(h2, xs, w_ff_gate[i].astype(BF16), w_ff_up[i].astype(BF16),
                            w_ff_down[i].astype(BF16), mod, lc, tm_ffn, tf_dense)
            stream = (xs,)
        else:
            assert last, "the MoE layer carries the final norm and drops the context rows"
            h_flat = h2.reshape(b * n, d)
            idx, wts = _router(h_flat, w_router[i], tm)
            tok, tile_expert, tile_rows, n_active, pos0, pos1 = _dispatch_plan(idx[:, :TOP_K], n_experts, tm_moe)
            split = tm_moe // tm_gather
            xg = _gather_rows(h_flat, tok.reshape(-1, 1, tm_gather), n_active * split, tm_gather)
            yexp = _moe_ffn(xg, w_ex_gate[i], w_ex_up[i], w_ex_down[i], tile_expert, tile_rows, n_active,
                            tm_moe, tf_moe)
            nt = (b * n) // tm
            out = _combine(yexp, pos0.reshape(nt, 1, tm), pos1.reshape(nt, 1, tm), xs, wts, mod, g_final, tm)
    return out
```

```python
import functools
import math

import jax
import jax.numpy as jnp
from jax import lax
from jax.experimental import pallas as pl
from jax.experimental.pallas import tpu as pltpu

F32 = jnp.float32
BF16 = jnp.bfloat16

HEAD_DIM = 64
GRID_W = 64
ROPE_THETA = 10000.0
NORM_EPS = 1e-6
MASK_VALUE = -1e30
LOG2E = math.log2(math.e)
CONV_K = 31
CONV_HALO = 16
WINDOW = 128
TOP_K = 2
LANES = 128
SUBLANES = 8
ONES_ROWS = 16
VMEM_LIMIT = 56 * 1024 * 1024


def _cparams(sem):
    return pltpu.CompilerParams(dimension_semantics=sem, vmem_limit_bytes=VMEM_LIMIT)


def _rms_mod(x, g, shift, scale):
    y = x * lax.rsqrt(jnp.mean(x * x, axis=-1, keepdims=True) + NORM_EPS)
    return (y * g) * (1.0 + scale) + shift


def _sigmoid(x):
    return 1.0 / (1.0 + jnp.exp(-x))


def _ada_kernel(c_ref, w_ref, b_ref, o_ref):
    c = c_ref[...]
    s = (c * _sigmoid(c)).astype(BF16)
    o_ref[0] = jnp.dot(s, w_ref[0].astype(BF16), preferred_element_type=F32) + b_ref[0]


def _ada(cvec, w_ada, b_ada):
    depth, d, n6 = w_ada.shape
    rows = cvec.shape[0]
    tn = 1024
    return pl.pallas_call(
        _ada_kernel,
        out_shape=jax.ShapeDtypeStruct((depth, rows, n6), F32),
        grid=(depth, n6 // tn),
        in_specs=[pl.BlockSpec((rows, d), lambda l, j: (0, 0)),
                  pl.BlockSpec((1, d, tn), lambda l, j: (l, 0, j)),
                  pl.BlockSpec((1, 1, tn), lambda l, j: (l, 0, j))],
        out_specs=pl.BlockSpec((1, rows, tn), lambda l, j: (l, 0, j)),
        compiler_params=_cparams(("arbitrary", "arbitrary")),
        name="ada",
    )(cvec, w_ada, b_ada.reshape(depth, 1, n6))


def _in_kernel(xc_ref, xl_ref, g_ref, mod_ref, w_ref, cos_ref, sa_ref, sb_ref,
               y_ref, qb_ref, qc_ref, kb_ref, vb_ref, kc_ref, vc_ref, *, sizes, nctx):
    conv_ch, n_bq, n_cq, n_bk, n_bv, n_ck, n_cv = sizes
    x = jnp.where(pl.program_id(1) < nctx, xc_ref[0], xl_ref[0])
    hb = _rms_mod(x, g_ref[...], mod_ref[0, 0:1, :], mod_ref[0, 1:2, :]).astype(BF16)

    def mm(lo, width):
        return jnp.dot(hb, w_ref[:, lo:lo + width], preferred_element_type=F32)

    cos, sa, sb = cos_ref[...], sa_ref[...], sb_ref[...]

    def rope_store(u, out_ref, scale, transposed=False):
        for gi in range(u.shape[1] // LANES):
            ug = u[:, gi * LANES:(gi + 1) * LANES]
            r = (ug * cos + pltpu.roll(ug, LANES - 16, 1) * sa + pltpu.roll(ug, 16, 1) * sb) * scale
            if transposed:
                out_ref[0, gi * LANES:(gi + 1) * LANES, :] = r.T.astype(out_ref.dtype)
            else:
                out_ref[0, :, gi * LANES:(gi + 1) * LANES] = r.astype(out_ref.dtype)

    qscale = HEAD_DIM ** -0.5
    grp = LANES + ONES_ROWS

    def glu_store(u):
        y_ref[0] = u[:, :conv_ch] * _sigmoid(u[:, conv_ch:])

    def vb_store(v):
        for gi in range(n_bv // LANES):
            vb_ref[0, gi, 0, 0:LANES, :] = v[:, gi * LANES:(gi + 1) * LANES].T.astype(BF16)
            vb_ref[0, gi, 0, LANES:LANES + ONES_ROWS, :] = jnp.ones((ONES_ROWS, v.shape[0]), BF16)

    def vc_store(v):
        for gi in range(n_cv // LANES):
            vc_ref[0, gi * grp:gi * grp + LANES, :] = v[:, gi * LANES:(gi + 1) * LANES].T.astype(BF16)
            vc_ref[0, gi * grp + LANES:(gi + 1) * grp, :] = jnp.ones((ONES_ROWS, v.shape[0]), BF16)

    sections = [
        (2 * conv_ch, glu_store),
        (n_bq, functools.partial(rope_store, out_ref=qb_ref, scale=qscale * LOG2E, transposed=True)),
        (n_cq, functools.partial(rope_store, out_ref=qc_ref, scale=qscale * LOG2E, transposed=True)),
        (n_bk, functools.partial(rope_store, out_ref=kb_ref, scale=1.0)),
        (n_bv, vb_store),
        (n_ck, functools.partial(rope_store, out_ref=kc_ref, scale=1.0)),
        (n_cv, vc_store),
    ]
    off = 0
    pending = None
    for width, epilogue in sections:
        u = mm(off, width)
        off += width
        if pending is not None:
            pending[1](pending[0])
        pending = (u, epilogue)
    pending[1](pending[0])


def _stream_specs(parts, lc, tm, q_off=0):
    nctx = lc // tm
    if len(parts) == 1:
        arrs, lat_off = (parts[0], parts[0]), nctx
    else:
        arrs, lat_off = parts, 0
    d = arrs[0].shape[2]
    ctx_spec = pl.BlockSpec((1, tm, d), lambda bi, j: (bi, jnp.minimum(j + q_off, nctx - 1), 0))
    lat_spec = pl.BlockSpec((1, tm, d), lambda bi, j: (bi, jnp.maximum(j + q_off - nctx, 0) + lat_off, 0))
    return arrs, [ctx_spec, lat_spec]


def _in_proj(parts, g, mod, w_bf16, tabs, sizes, lc, tm):
    b, d = parts[0].shape[0], parts[0].shape[2]
    s = sum(p.shape[1] for p in parts)
    conv_ch, n_bq, n_cq, n_bk, n_bv, n_ck, n_cv = sizes
    d_in = w_bf16.shape[1]
    nctx = lc // tm
    n_mod = mod.shape[0]

    def row(bi, j):
        return (bi, j, 0)

    def modrow(bi, j):
        return (jnp.where(j < nctx, n_mod - 1, bi), 0, 0)

    widths = (conv_ch, n_bq, n_cq, n_bk, n_bv, n_ck, n_cv)
    dts = (F32, BF16, BF16, BF16, BF16, BF16, BF16)
    out_shape = [jax.ShapeDtypeStruct((b, s, w), dt) for w, dt in zip(widths, dts)]
    out_specs = [pl.BlockSpec((1, tm, w), row) for w in widths]
    n_vh = n_bv // LANES
    out_shape[1] = jax.ShapeDtypeStruct((b, n_bq, s), BF16)
    out_specs[1] = pl.BlockSpec((1, n_bq, tm), lambda bi, j: (bi, 0, j))
    out_shape[4] = jax.ShapeDtypeStruct((b, n_vh, s // tm, LANES + ONES_ROWS, tm), BF16)
    out_specs[4] = pl.BlockSpec((1, n_vh, 1, LANES + ONES_ROWS, tm), lambda bi, j: (bi, 0, j, 0, 0))
    for idx, wdt in ((2, n_cq), (6, (n_cv // LANES) * (LANES + ONES_ROWS))):
        out_shape[idx] = jax.ShapeDtypeStruct((b, wdt, s), BF16)
        out_specs[idx] = pl.BlockSpec((1, wdt, tm), lambda bi, j: (bi, 0, j))
    tab_spec = pl.BlockSpec((tm, LANES), lambda bi, j: (j, 0))
    stream, stream_specs = _stream_specs(parts, lc, tm)
    return pl.pallas_call(
        functools.partial(_in_kernel, sizes=sizes, nctx=nctx),
        out_shape=out_shape,
        grid=(b, s // tm),
        in_specs=stream_specs + [
                  pl.BlockSpec((1, d), lambda bi, j: (0, 0)),
                  pl.BlockSpec((1, 6, d), modrow),
                  pl.BlockSpec((d, d_in), lambda bi, j: (0, 0)),
                  tab_spec, tab_spec, tab_spec],
        out_specs=out_specs,
        compiler_params=_cparams(("arbitrary", "arbitrary")),
        name="in_proj",
    )(*stream, g.reshape(1, d), mod, w_bf16, *tabs)


def _conv_kernel(yp_ref, yc_ref, yn_ref, w_ref, b_ref, g_ref, bb_ref, o_ref, win_ref, *, tm, seg_tile, sub):
    j = pl.program_id(1)
    nj = pl.num_programs(1)
    has_prev = jnp.logical_and(j > 0, j != seg_tile)
    has_next = jnp.logical_and(j + 1 < nj, j + 1 != seg_tile)
    h = CONV_HALO
    win_ref[0, 0:h, :] = jnp.where(has_prev, yp_ref[0], 0.0)
    win_ref[0, h:h + tm, :] = yc_ref[0]
    win_ref[0, h + tm:h + tm + h, :] = jnp.where(has_next, yn_ref[0], 0.0)
    span = tm + 2 * h - SUBLANES
    for r in range(1, SUBLANES):
        win_ref[r, 0:span, :] = win_ref[0, pl.ds(r, span), :]
    base = h - CONV_K // 2
    for r0 in range(0, tm, sub):
        acc = jnp.zeros((sub, win_ref.shape[2]), F32)
        for t in range(CONV_K):
            o = base + r0 + t
            acc = acc + w_ref[t:t + 1, :] * win_ref[o % SUBLANES, o - o % SUBLANES:o - o % SUBLANES + sub, :]
        y = acc + b_ref[...]
        mu = jnp.mean(y, axis=-1, keepdims=True)
        yc = y - mu
        var = jnp.mean(yc * yc, axis=-1, keepdims=True)
        z = yc * lax.rsqrt(var + NORM_EPS) * g_ref[...] + bb_ref[...]
        o_ref[0, r0:r0 + sub, :] = (z * _sigmoid(z)).astype(o_ref.dtype)


def _conv_module(y, w_dw, b_dw, ln_g, ln_b, lc, tm):
    b, s, ch = y.shape
    h = CONV_HALO
    per = tm // h
    nblk = s // h

    def prev(bi, j):
        return (bi, jnp.maximum(j * per - 1, 0), 0)

    def nxt(bi, j):
        return (bi, jnp.minimum((j + 1) * per, nblk - 1), 0)

    vec = pl.BlockSpec((1, ch), lambda bi, j: (0, 0))
    return pl.pallas_call(
        functools.partial(_conv_kernel, tm=tm, seg_tile=lc // tm, sub=32),
        out_shape=jax.ShapeDtypeStruct((b, s, ch), BF16),
        grid=(b, s // tm),
        in_specs=[pl.BlockSpec((1, h, ch), prev),
                  pl.BlockSpec((1, tm, ch), lambda bi, j: (bi, j, 0)),
                  pl.BlockSpec((1, h, ch), nxt),
                  pl.BlockSpec((CONV_K, ch), lambda bi, j: (0, 0)),
                  vec, vec, vec],
        out_specs=pl.BlockSpec((1, tm, ch), lambda bi, j: (bi, j, 0)),
        scratch_shapes=[pltpu.VMEM((SUBLANES, tm + 2 * h, ch), F32)],
        compiler_params=_cparams(("arbitrary", "arbitrary")),
        name="conv_module",
    )(y, y, y, w_dw.reshape(CONV_K, ch), b_dw.reshape(1, ch), ln_g.reshape(1, ch), ln_b.reshape(1, ch))


def _diff_kernel(lamp_ref, qt_ref, k_ref, vt_ref, g_ref, o_ref, m_scr, acc_scr, sa_scr, sb_scr, sc_scr, rhs_scr,
                 *, tq, tk, unroll, context_only, ctx_chunks, all_chunks, lam_init):
    qt = qt_ref[0]
    sub = lax.broadcasted_iota(jnp.int32, qt.shape, 0)
    zero = jnp.zeros_like(qt)
    rhs_scr[:, 0:tq] = jnp.where(sub < HEAD_DIM, qt, zero)
    rhs_scr[:, tq:2 * tq] = jnp.where(sub >= HEAD_DIM, qt, zero)
    m_scr[...] = jnp.full(m_scr.shape, MASK_VALUE, F32)
    acc_scr[...] = jnp.zeros(acc_scr.shape, F32)

    def scores(gidx, count, s_ref):
        mxs = []
        for half in range(2):
            cs = slice(half * tq, (half + 1) * tq)
            mx = None
            for u in range(count):
                start = pl.multiple_of((gidx * count + u) * tk, tk)
                s = jnp.dot(k_ref[0, pl.ds(start, tk), :], rhs_scr[:, cs], preferred_element_type=F32)
                s_ref[u * tk:(u + 1) * tk, cs] = s
                cm = jnp.max(s, axis=0, keepdims=True)
                mx = cm if mx is None else jnp.maximum(mx, cm)
            mxs.append(mx)
        return mxs

    def accumulate(gidx, count, s_ref, mxs):
        for half in range(2):
            cs = slice(half * tq, (half + 1) * tq)
            m_prev = m_scr[:, cs]
            m_new = jnp.maximum(m_prev, mxs[half])
            alpha = jnp.exp2(m_prev - m_new)
            pv = None
            for u in range(count):
                p = jnp.exp2((s_ref[u * tk:(u + 1) * tk, cs] - m_new).astype(BF16))
                d = jnp.dot(vt_ref[0, 0, gidx * count + u], p, preferred_element_type=F32)
                pv = d if pv is None else pv + d
            acc_scr[:, cs] = alpha * acc_scr[:, cs] + pv
            m_scr[:, cs] = m_new

    bufs = (sa_scr, sb_scr, sc_scr)

    def accumulate_and_score(g_acc, g_new, mxs):
        s_acc, s_new = bufs[g_acc % 3], bufs[g_new % 3]
        new_mxs = []
        for half in range(2):
            cs = slice(half * tq, (half + 1) * tq)
            m_prev = m_scr[:, cs]
            m_new = jnp.maximum(m_prev, mxs[half])
            alpha = jnp.exp2(m_prev - m_new)
            pv = None
            mx = None
            for u in range(unroll):
                p = jnp.exp2((s_acc[u * tk:(u + 1) * tk, cs] - m_new).astype(BF16))
                d = jnp.dot(vt_ref[0, 0, g_acc * unroll + u], p, preferred_element_type=F32)
                pv = d if pv is None else pv + d
                start = pl.multiple_of((g_new * unroll + u) * tk, tk)
                s = jnp.dot(k_ref[0, pl.ds(start, tk), :], rhs_scr[:, cs], preferred_element_type=F32)
                s_new[u * tk:(u + 1) * tk, cs] = s
                cm = jnp.max(s, axis=0, keepdims=True)
                mx = cm if mx is None else jnp.maximum(mx, cm)
            acc_scr[:, cs] = alpha * acc_scr[:, cs] + pv
            m_scr[:, cs] = m_new
            new_mxs.append(mx)
        return new_mxs

    def context_keys_only():
        for c in range(ctx_chunks):
            accumulate(c, 1, sa_scr, scores(c, 1, sa_scr))

    def all_keys():
        n_groups = all_chunks // unroll
        mx = [scores(g, unroll, bufs[g % 3]) for g in range(min(2, n_groups))]
        for g in range(n_groups):
            if g + 2 < n_groups:
                mx.append(accumulate_and_score(g, g + 2, mx[g]))
            else:
                accumulate(g, unroll, bufs[g % 3], mx[g])

    if context_only:
        context_keys_only()
    else:
        all_keys()

    vdim = 2 * HEAD_DIM
    o = acc_scr[0:vdim, :] / acc_scr[vdim:vdim + 1, :]
    lp = lamp_ref[...]
    lam = (jnp.exp(jnp.sum(lp[0:1] * lp[1:2], axis=1, keepdims=True))
           - jnp.exp(jnp.sum(lp[2:3] * lp[3:4], axis=1, keepdims=True)) + lam_init)
    od = o[:, :tq] - lam * o[:, tq:]
    y = od * lax.rsqrt(jnp.mean(od * od, axis=0, keepdims=True) + NORM_EPS)
    o_ref[0] = (y * g_ref[...] * (1.0 - lam_init)).T.astype(o_ref.dtype)


def _diff_attention(qbt, kb, vbt, lam_params, g_subln, lam_init, lc, context_queries, tq, unroll):
    b, s, w = kb.shape
    heads = w // (2 * HEAD_DIM)
    vrows, tk = vbt.shape[3], vbt.shape[4]
    assert vrows == 2 * HEAD_DIM + ONES_ROWS
    q_off = 0 if context_queries else lc // tq
    nq = lc // tq if context_queries else (s - lc) // tq
    n_keys = lc if context_queries else s
    assert (s // tk) % unroll == 0 and lc % tk == 0
    kern = functools.partial(_diff_kernel, tq=tq, tk=tk, unroll=unroll, context_only=context_queries,
                             ctx_chunks=lc // tk, all_chunks=s // tk, lam_init=lam_init)
    return pl.pallas_call(
        kern,
        out_shape=jax.ShapeDtypeStruct((b, nq * tq, w), BF16),
        grid=(b, heads, nq),
        in_specs=[pl.BlockSpec((4, HEAD_DIM), lambda bi, h, i: (0, 0)),
                  pl.BlockSpec((1, 2 * HEAD_DIM, tq), lambda bi, h, i: (bi, h, i + q_off)),
                  pl.BlockSpec((1, n_keys, 2 * HEAD_DIM), lambda bi, h, i: (bi, 0, h)),
                  pl.BlockSpec((1, 1, n_keys // tk, vrows, tk), lambda bi, h, i: (bi, h, 0, 0, 0)),
                  pl.BlockSpec((2 * HEAD_DIM, 1), lambda bi, h, i: (0, 0))],
        out_specs=pl.BlockSpec((1, tq, 2 * HEAD_DIM), lambda bi, h, i: (bi, i, h)),
        scratch_shapes=[pltpu.VMEM((1, 2 * tq), F32),
                        pltpu.VMEM((vrows, 2 * tq), F32),
                        pltpu.VMEM((unroll * tk, 2 * tq), F32), pltpu.VMEM((unroll * tk, 2 * tq), F32),
                        pltpu.VMEM((unroll * tk, 2 * tq), F32),
                        pltpu.VMEM((2 * HEAD_DIM, 2 * tq), BF16)],
        compiler_params=_cparams(("arbitrary", "arbitrary", "arbitrary")),
        name="diff_attention",
    )(lam_params, qbt, kb, vbt, g_subln.reshape(2 * HEAD_DIM, 1))


def _swa_kernel(sink_ref, qt_ref, kctx_ref, vtctx_ref, kp_ref, kc_ref, kn_ref, vtp_ref, vtc_ref, vtn_ref, o_ref,
                *, tq, q_off, n_ctx_qblk, n_lat, kv_heads, group):
    i = pl.program_id(1) + q_off
    is_lat = i >= n_ctx_qblk
    q0 = (i - n_ctx_qblk) * tq
    wq = group * tq
    kpos = q0 - tq + lax.broadcasted_iota(jnp.int32, (3 * tq, wq), 0)
    qpos = q0 + jnp.bitwise_and(lax.broadcasted_iota(jnp.int32, (3 * tq, wq), 1), tq - 1)
    valid = jnp.logical_and(jnp.abs(qpos - kpos) <= WINDOW, jnp.logical_and(kpos >= 0, kpos < n_lat))
    valid = jnp.logical_and(valid, is_lat)
    qt = qt_ref[0]
    kloc = jnp.concatenate([kp_ref[0], kc_ref[0], kn_ref[0]], axis=0)
    vtloc = jnp.concatenate([vtp_ref[0], vtc_ref[0], vtn_ref[0]], axis=1)
    kctx = kctx_ref[0]
    vtctx = vtctx_ref[0]
    zeros = jnp.zeros((HEAD_DIM, tq), qt.dtype)
    heads_out = []

    def head_scores(h):
        gs = slice((h // 2) * LANES, (h // 2 + 1) * LANES)
        upper = h % 2 == 1
        cols = []
        for gi in range(group):
            hq = h * group + gi
            qh = qt[hq * HEAD_DIM:(hq + 1) * HEAD_DIM, :]
            cols.append(jnp.concatenate([zeros, qh] if upper else [qh, zeros], axis=0))
        rhs = jnp.concatenate(cols, axis=1)
        sink = jnp.concatenate(
            [jnp.full((1, tq), sink_ref[h * group + gi] * LOG2E, F32) for gi in range(group)], axis=1)
        s_ctx = jnp.dot(kctx[:, gs], rhs, preferred_element_type=F32)
        s_loc = jnp.dot(kloc[:, gs], rhs, preferred_element_type=F32)
        s_loc = jnp.where(valid, s_loc, MASK_VALUE)
        m = jnp.maximum(jnp.maximum(jnp.max(s_ctx, axis=0, keepdims=True), jnp.max(s_loc, axis=0, keepdims=True)), sink)
        return s_ctx, s_loc, m, sink

    def head_output(h, s_ctx, s_loc, m, sink):
        vs = slice((h // 2) * (LANES + ONES_ROWS), (h // 2 + 1) * (LANES + ONES_ROWS))
        upper = h % 2 == 1
        e_ctx = jnp.exp2((s_ctx - m).astype(BF16))
        e_loc = jnp.exp2((s_loc - m).astype(BF16))
        ot = (jnp.dot(vtctx[vs, :], e_ctx, preferred_element_type=F32)
              + jnp.dot(vtloc[vs, :], e_loc, preferred_element_type=F32))
        denom = ot[LANES:LANES + 1, :] + jnp.exp2(sink - m)
        ot = (ot[HEAD_DIM:LANES, :] if upper else ot[:HEAD_DIM, :]) / denom
        for gi in range(group):
            heads_out.append(ot[:, gi * tq:(gi + 1) * tq])

    cur = head_scores(0)
    for h in range(kv_heads):
        nxt = head_scores(h + 1) if h + 1 < kv_heads else None
        head_output(h, *cur)
        cur = nxt
    for pi in range(len(heads_out) // 2):
        pair = jnp.concatenate([heads_out[2 * pi], heads_out[2 * pi + 1]], axis=0)
        o_ref[0, :, pi * LANES:(pi + 1) * LANES] = pair.T.astype(o_ref.dtype)


def _swa(qct, kc, vct, sink, lc, with_ctx, tq):
    b, s, kvw = kc.shape
    w = qct.shape[1]
    kv_heads = kvw // HEAD_DIM
    group = w // kvw
    assert tq & (tq - 1) == 0 and kv_heads % 2 == 0
    q_off = 0 if with_ctx else lc // tq
    nq = s // tq - q_off
    nblk = s // tq

    def rows(f):
        return pl.BlockSpec((1, tq, kvw), lambda bi, i: (bi, f(i + q_off), 0))

    vrows = vct.shape[1]
    assert vrows == (kvw // LANES) * (LANES + ONES_ROWS)

    def cols(f):
        return pl.BlockSpec((1, vrows, tq), lambda bi, i: (bi, 0, f(i + q_off)))

    cur = lambda i: i
    prev = lambda i: jnp.maximum(i - 1, 0)
    nxt = lambda i: jnp.minimum(i + 1, nblk - 1)
    kern = functools.partial(_swa_kernel, tq=tq, q_off=q_off, n_ctx_qblk=lc // tq, n_lat=s - lc,
                             kv_heads=kv_heads, group=group)
    return pl.pallas_call(
        kern,
        out_shape=jax.ShapeDtypeStruct((b, nq * tq, w), BF16),
        grid=(b, nq),
        in_specs=[pl.BlockSpec(memory_space=pltpu.SMEM),
                  pl.BlockSpec((1, w, tq), lambda bi, i: (bi, 0, i + q_off)),
                  pl.BlockSpec((1, lc, kvw), lambda bi, i: (bi, 0, 0)),
                  pl.BlockSpec((1, vrows, lc), lambda bi, i: (bi, 0, 0)),
                  rows(prev), rows(cur), rows(nxt), cols(prev), cols(cur), cols(nxt)],
        out_specs=pl.BlockSpec((1, tq, w), lambda bi, i: (bi, i, 0)),
        compiler_params=_cparams(("arbitrary", "arbitrary")),
        name="swa",
    )(sink, qct, kc, vct, kc, kc, kc, vct, vct, vct)


def _out_kernel(a_ref, ob_ref, oc_ref, xc_ref, xl_ref, w_ref, mod_ref, g_ref, xo_ref, h_ref, *, n_ctx_tiles):
    na, nb = a_ref.shape[2], ob_ref.shape[2]
    mix = jnp.dot(a_ref[0], w_ref[0:na, :], preferred_element_type=F32)
    mix = mix + jnp.dot(ob_ref[0], w_ref[na:na + nb, :], preferred_element_type=F32)
    mix = mix + jnp.dot(oc_ref[0], w_ref[na + nb:, :], preferred_element_type=F32)
    x = jnp.where(pl.program_id(1) < n_ctx_tiles, xc_ref[0], xl_ref[0])
    xn = x + mod_ref[0, 2:3, :] * mix
    xo_ref[0] = xn
    h_ref[0] = _rms_mod(xn, g_ref[...], mod_ref[0, 3:4, :], mod_ref[0, 4:5, :]).astype(h_ref.dtype)


def _out_proj(a, ob, oc, parts, w_bf16, mod, g2, lc, with_ctx, tm, h_dtype):
    b, d = parts[0].shape[0], parts[0].shape[2]
    s = sum(p.shape[1] for p in parts)
    nctx = lc // tm
    q_off = 0 if with_ctx else nctx
    nt = s // tm - q_off
    n_mod = mod.shape[0]
    out_rows = s - q_off * tm

    def row(bi, j):
        return (bi, j + q_off, 0)

    def modrow(bi, j):
        return (jnp.where(j + q_off < nctx, n_mod - 1, bi), 0, 0)

    def own(bi, j):
        return (bi, j, 0)

    assert ob.shape[1] == out_rows and oc.shape[1] == out_rows
    stream, stream_specs = _stream_specs(parts, lc, tm, q_off)
    return pl.pallas_call(
        functools.partial(_out_kernel, n_ctx_tiles=nctx - q_off),
        out_shape=(jax.ShapeDtypeStruct((b, out_rows, d), F32), jax.ShapeDtypeStruct((b, out_rows, d), h_dtype)),
        grid=(b, nt),
        in_specs=[pl.BlockSpec((1, tm, a.shape[2]), row), pl.BlockSpec((1, tm, ob.shape[2]), own),
                  pl.BlockSpec((1, tm, oc.shape[2]), own)] + stream_specs + [
                  pl.BlockSpec(w_bf16.shape, lambda bi, j: (0, 0)),
                  pl.BlockSpec((1, 6, d), modrow),
                  pl.BlockSpec((1, d), lambda bi, j: (0, 0))],
        out_specs=(pl.BlockSpec((1, tm, d), own), pl.BlockSpec((1, tm, d), own)),
        compiler_params=_cparams(("arbitrary", "arbitrary")),
        name="out_proj",
    )(a, ob, oc, *stream, w_bf16, mod, g2.reshape(1, d))


def _ffn_kernel(h_ref, x_ref, wg_ref, wu_ref, wd_ref, modb_ref, modc_ref, o_ref, *, tm, lc):
    f = pl.program_id(2)

    @pl.when(f == 0)
    def _():
        o_ref[...] = jnp.zeros(o_ref.shape, F32)

    h = h_ref[0]
    g = jnp.dot(h, wg_ref[...], preferred_element_type=F32)
    u = jnp.dot(h, wu_ref[...], preferred_element_type=F32)
    act = (g * _sigmoid(g) * u).astype(BF16)
    o_ref[0] += jnp.dot(act, wd_ref[...], preferred_element_type=F32)

    @pl.when(f == pl.num_programs(2) - 1)
    def _():
        rows = pl.program_id(1) * tm + lax.broadcasted_iota(jnp.int32, (tm, 1), 0)
        gate = jnp.where(rows < lc, modc_ref[0, 5:6, :], modb_ref[0, 5:6, :])
        o_ref[0] = x_ref[0] + gate * o_ref[0]


def _ffn_dense(h, x, wg, wu, wd, mod, lc, tm, tf):
    b, s, d = x.shape
    dff = wg.shape[1]
    n_mod = mod.shape[0]
    row = lambda bi, j, f: (bi, j, 0)
    return pl.pallas_call(
        functools.partial(_ffn_kernel, tm=tm, lc=lc),
        out_shape=jax.ShapeDtypeStruct((b, s, d), F32),
        grid=(b, s // tm, dff // tf),
        in_specs=[pl.BlockSpec((1, tm, d), row), pl.BlockSpec((1, tm, d), row),
                  pl.BlockSpec((d, tf), lambda bi, j, f: (0, f)),
                  pl.BlockSpec((d, tf), lambda bi, j, f: (0, f)),
                  pl.BlockSpec((tf, d), lambda bi, j, f: (f, 0)),
                  pl.BlockSpec((1, 6, d), lambda bi, j, f: (bi, 0, 0)),
                  pl.BlockSpec((1, 6, d), lambda bi, j, f: (n_mod - 1, 0, 0))],
        out_specs=pl.BlockSpec((1, tm, d), row),
        compiler_params=_cparams(("arbitrary", "arbitrary", "arbitrary")),
        name="ffn_dense",
    )(h, x, wg, wu, wd, mod, mod)


def _router_kernel(h_ref, w_ref, idx_ref, wt_ref):
    h = h_ref[...]
    w = w_ref[...]
    h_hi = h.astype(BF16)
    w_hi = w.astype(BF16)
    h_lo = (h - h_hi.astype(F32)).astype(BF16)
    w_lo = (w - w_hi.astype(F32)).astype(BF16)
    logits = (jnp.dot(h_hi, w_hi, preferred_element_type=F32) + jnp.dot(h_hi, w_lo, preferred_element_type=F32)
              + jnp.dot(h_lo, w_hi, preferred_element_type=F32))
    n_e = logits.shape[1]
    lane = lax.broadcasted_iota(jnp.int32, logits.shape, 1)
    m1 = jnp.max(logits, axis=1, keepdims=True)
    i1 = jnp.min(jnp.where(logits == m1, lane, n_e), axis=1, keepdims=True)
    rest = jnp.where(lane == i1, -jnp.inf, logits)
    m2 = jnp.max(rest, axis=1, keepdims=True)
    i2 = jnp.min(jnp.where(rest == m2, lane, n_e), axis=1, keepdims=True)
    e2 = jnp.exp(m2 - m1)
    den = 1.0 + e2
    out_lane = lax.broadcasted_iota(jnp.int32, idx_ref.shape, 1)
    idx_ref[...] = jnp.where(out_lane == 0, i1, jnp.where(out_lane == 1, i2, 0))
    wt_ref[...] = jnp.where(out_lane == 0, 1.0 / den, jnp.where(out_lane == 1, e2 / den, 0.0))


def _router(h_flat, w_router, tm):
    t, d = h_flat.shape
    n_e = w_router.shape[1]
    return pl.pallas_call(
        _router_kernel,
        out_shape=(jax.ShapeDtypeStruct((t, LANES), jnp.int32), jax.ShapeDtypeStruct((t, LANES), F32)),
        grid=(t // tm,),
        in_specs=[pl.BlockSpec((tm, d), lambda i: (i, 0)), pl.BlockSpec((d, n_e), lambda i: (0, 0))],
        out_specs=(pl.BlockSpec((tm, LANES), lambda i: (i, 0)), pl.BlockSpec((tm, LANES), lambda i: (i, 0))),
        compiler_params=_cparams(("arbitrary",)),
        name="router",
    )(h_flat, w_router)


def _row_copy(src_hbm, row, dst, r, sem):
    return pltpu.make_async_copy(src_hbm.at[pl.ds(row, 1)], dst.at[pl.ds(r, 1)], sem)


def _start_rows(src_hbm, idx_ref, dst, sem, n):
    def body(r, carry):
        _row_copy(src_hbm, idx_ref[0, 0, r], dst, r, sem).start()
        return carry

    lax.fori_loop(0, n, body, 0, unroll=8)


def _wait_rows(src_hbm, dst, sem, n):
    def body(r, carry):
        _row_copy(src_hbm, 0, dst, r, sem).wait()
        return carry

    lax.fori_loop(0, n, body, 0, unroll=8)


def _gather_kernel(nact_ref, tok_ref, tok_next_ref, h_hbm, o_ref, buf, sem, *, tm):
    i = pl.program_id(0)
    nact = nact_ref[0]
    slot = lax.rem(i, 2)

    @pl.when(jnp.logical_and(i == 0, nact > 0))
    def _():
        _start_rows(h_hbm, tok_ref, buf.at[0], sem.at[0], tm)

    @pl.when(i + 1 < nact)
    def _():
        _start_rows(h_hbm, tok_next_ref, buf.at[1 - slot], sem.at[1 - slot], tm)

    @pl.when(i < nact)
    def _():
        _wait_rows(h_hbm, buf.at[slot], sem.at[slot], tm)
        o_ref[...] = buf[slot].astype(o_ref.dtype)

    @pl.when(i >= nact)
    def _():
        o_ref[...] = jnp.zeros(o_ref.shape, o_ref.dtype)


def _gather_rows(h_flat, tok, n_active, tm):
    n_tiles = tok.shape[0]
    d = h_flat.shape[1]
    grid_spec = pltpu.PrefetchScalarGridSpec(
        num_scalar_prefetch=1,
        grid=(n_tiles,),
        in_specs=[pl.BlockSpec((1, 1, tm), lambda i, na: (i, 0, 0), memory_space=pltpu.SMEM),
                  pl.BlockSpec((1, 1, tm), lambda i, na: (jnp.minimum(i + 1, n_tiles - 1), 0, 0),
                               memory_space=pltpu.SMEM),
                  pl.BlockSpec(memory_space=pl.ANY)],
        out_specs=pl.BlockSpec((tm, d), lambda i, na: (i, 0)),
        scratch_shapes=[pltpu.VMEM((2, tm, d), h_flat.dtype), pltpu.SemaphoreType.DMA((2,))],
    )
    return pl.pallas_call(
        functools.partial(_gather_kernel, tm=tm),
        out_shape=jax.ShapeDtypeStruct((n_tiles * tm, d), BF16),
        grid_spec=grid_spec,
        compiler_params=_cparams(("arbitrary",)),
        name="moe_gather",
    )(n_active, tok, tok, h_flat)


def _moe_ffn_kernel(te_ref, rows_ref, nact_ref, x_ref, wg_ref, wu_ref, wd_ref, o_ref):
    i = pl.program_id(0)
    f = pl.program_id(1)
    half = o_ref.shape[0] // 2

    @pl.when(f == 0)
    def _():
        o_ref[...] = jnp.zeros(o_ref.shape, o_ref.dtype)

    def swiglu_rows(n_rows):
        x = x_ref[0:n_rows, :]
        g = jnp.dot(x, wg_ref[0].astype(BF16), preferred_element_type=F32)
        u = jnp.dot(x, wu_ref[0].astype(BF16), preferred_element_type=F32)
        act = (g * _sigmoid(g) * u).astype(BF16)
        o_ref[0:n_rows, :] += jnp.dot(act, wd_ref[0].astype(BF16), preferred_element_type=F32)

    rows = rows_ref[i]
    pl.when(rows > half)(functools.partial(swiglu_rows, o_ref.shape[0]))
    pl.when(jnp.logical_and(rows > 0, rows <= half))(functools.partial(swiglu_rows, half))


def _moe_ffn(xs, wg, wu, wd, tile_expert, tile_rows, n_active, tm, tf):
    p, d = xs.shape
    dff = wg.shape[2]
    nf = dff // tf
    n_tiles = p // tm

    def tile(i, f, te, tr, na):
        return (jnp.minimum(i, na[0] - 1), 0)

    def fcol(i, f, na):
        return jnp.where(i < na[0], f, nf - 1)

    grid_spec = pltpu.PrefetchScalarGridSpec(
        num_scalar_prefetch=3,
        grid=(n_tiles, nf),
        in_specs=[pl.BlockSpec((tm, d), tile, pipeline_mode=pl.Buffered(1)),
                  pl.BlockSpec((1, d, tf), lambda i, f, te, tr, na: (te[i], 0, fcol(i, f, na))),
                  pl.BlockSpec((1, d, tf), lambda i, f, te, tr, na: (te[i], 0, fcol(i, f, na))),
                  pl.BlockSpec((1, tf, d), lambda i, f, te, tr, na: (te[i], fcol(i, f, na), 0))],
        out_specs=pl.BlockSpec((tm, d), lambda i, f, te, tr, na: (i, 0), pipeline_mode=pl.Buffered(1)),
    )
    return pl.pallas_call(
        _moe_ffn_kernel,
        out_shape=jax.ShapeDtypeStruct((p, d), F32),
        grid_spec=grid_spec,
        compiler_params=_cparams(("arbitrary", "arbitrary")),
        name="moe_ffn",
    )(tile_expert, tile_rows, n_active, xs, wg, wu, wd)


def _combine_kernel(p0_ref, p1_ref, p0n_ref, p1n_ref, y_hbm, x_ref, wt_ref, mod_ref, g_ref, o_ref, buf, sem, *, tm):
    t = pl.program_id(0) * pl.num_programs(1) + pl.program_id(1)
    n_tiles = pl.num_programs(0) * pl.num_programs(1)
    slot = lax.rem(t, 2)

    def start(r0, r1, s):
        _start_rows(y_hbm, r0, buf.at[s, 0], sem.at[s, 0], tm)
        _start_rows(y_hbm, r1, buf.at[s, 1], sem.at[s, 1], tm)

    @pl.when(t == 0)
    def _():
        start(p0_ref, p1_ref, 0)

    @pl.when(t + 1 < n_tiles)
    def _():
        start(p0n_ref, p1n_ref, 1 - slot)

    _wait_rows(y_hbm, buf.at[slot, 0], sem.at[slot, 0], tm)
    _wait_rows(y_hbm, buf.at[slot, 1], sem.at[slot, 1], tm)
    wt = wt_ref[...]
    f = wt[:, 0:1] * buf[slot, 0] + wt[:, 1:2] * buf[slot, 1]
    xn = x_ref[0] + mod_ref[0, 5:6, :] * f
    y = xn * lax.rsqrt(jnp.mean(xn * xn, axis=-1, keepdims=True) + NORM_EPS)
    o_ref[0] = y * g_ref[...]


def _combine(y, pos0, pos1, x, wts, mod, g_final, tm):
    b, n, d = x.shape
    nt = n // tm
    last = b * nt - 1
    idx = lambda f: pl.BlockSpec((1, 1, tm), lambda bi, j: (f(bi * nt + j), 0, 0), memory_space=pltpu.SMEM)
    cur = lambda t: t
    nxt = lambda t: jnp.minimum(t + 1, last)
    return pl.pallas_call(
        functools.partial(_combine_kernel, tm=tm),
        out_shape=jax.ShapeDtypeStruct((b, n, d), F32),
        grid=(b, nt),
        in_specs=[idx(cur), idx(cur), idx(nxt), idx(nxt),
                  pl.BlockSpec(memory_space=pl.ANY),
                  pl.BlockSpec((1, tm, d), lambda bi, j: (bi, j, 0)),
                  pl.BlockSpec((tm, LANES), lambda bi, j: (bi * nt + j, 0)),
                  pl.BlockSpec((1, 6, d), lambda bi, j: (bi, 0, 0)),
                  pl.BlockSpec((1, d), lambda bi, j: (0, 0))],
        out_specs=pl.BlockSpec((1, tm, d), lambda bi, j: (bi, j, 0)),
        scratch_shapes=[pltpu.VMEM((2, 2, tm, d), F32), pltpu.SemaphoreType.DMA((2, 2))],
        compiler_params=_cparams(("arbitrary", "arbitrary")),
        name="moe_combine",
    )(pos0, pos1, pos0, pos1, y, x, wts, mod, g_final.reshape(1, d))


def _dispatch_plan(idx, n_experts, tm):
    t = idx.shape[0]
    e_flat = idx.reshape(-1)
    onehot = (e_flat[:, None] == jnp.arange(n_experts, dtype=jnp.int32)[None, :]).astype(jnp.int32)
    rank = jnp.sum((jnp.cumsum(onehot, axis=0) - 1) * onehot, axis=1)
    counts = jnp.sum(onehot, axis=0)
    padded = ((counts + tm - 1) // tm) * tm
    ends = jnp.cumsum(padded)
    starts = ends - padded
    dest = starts[e_flat] + rank
    n_tiles = (TOP_K * t) // tm + n_experts
    tok = jnp.zeros((n_tiles * tm,), jnp.int32).at[dest].set(jnp.arange(TOP_K * t, dtype=jnp.int32) // TOP_K)
    tile_row0 = jnp.arange(n_tiles, dtype=jnp.int32) * tm
    tile_expert = jnp.sum((ends[None, :] <= tile_row0[:, None]).astype(jnp.int32), axis=1)
    tile_expert = jnp.minimum(tile_expert, n_experts - 1)
    n_active = (ends[-1] // tm).astype(jnp.int32).reshape(1)
    active = jnp.arange(n_tiles) < n_active[0]
    tile_rows = jnp.clip((starts + counts)[tile_expert] - tile_row0, 0, tm)
    tile_rows = jnp.where(active, tile_rows, 0).astype(jnp.int32)
    last = jnp.maximum(n_active[0] - 1, 0)
    tile_expert = jnp.where(active, tile_expert, tile_expert[last])
    pos = dest.reshape(t, TOP_K)
    return tok.reshape(n_tiles, 1, tm), tile_expert, tile_rows, n_active, pos[:, 0], pos[:, 1]


def _rope_tables(lc, n):
    rows = n // GRID_W
    row = jnp.repeat(jnp.arange(rows), GRID_W).astype(F32)
    col = jnp.tile(jnp.arange(GRID_W), rows).astype(F32)
    axis_dim = HEAD_DIM // 2
    inv = ROPE_THETA ** (-jnp.arange(axis_dim // 2, dtype=F32) / (axis_dim // 2))
    ang_r = row[:, None] * inv[None, :]
    ang_c = col[:, None] * inv[None, :]
    ang = jnp.concatenate([ang_r, ang_r, ang_c, ang_c], axis=-1)
    cos, sin = jnp.cos(ang), jnp.sin(ang)
    first_half = (jnp.arange(HEAD_DIM) % (HEAD_DIM // 2)) < (HEAD_DIM // 4)
    sa = jnp.where(first_half[None, :], -sin, 0.0)
    sb = jnp.where(first_half[None, :], 0.0, sin)
    ident = lambda v: jnp.full((lc, HEAD_DIM), v, F32)
    full = lambda ctx_v, t: jnp.tile(jnp.concatenate([ident(ctx_v), t], axis=0), (1, LANES // HEAD_DIM))
    return full(1.0, cos), full(0.0, sa), full(0.0, sb)


def _pick_tile(n, candidates):
    for c in candidates:
        if n % c == 0:
            return c
    raise ValueError(f"no tile size among {candidates} divides {n}")


def kernel(x, c, ctx, c_ctx, w_ada, b_ada, g_norm1, g_norm2, w_in, w_dw, b_dw, ln_g, ln_b, lam_q1, lam_k1, lam_q2, lam_k2, g_subln, sink, w_out, w_ff_gate, w_ff_up, w_ff_down, w_router, w_ex_gate, w_ex_up, w_ex_down, g_final):
    b, n, d = x.shape
    lc = ctx.shape[1]
    s = lc + n
    depth = w_ada.shape[0]
    conv_ch = w_dw.shape[-1]
    n_swa_q = sink.shape[1] * HEAD_DIM
    n_diff_v = d - conv_ch - n_swa_q
    n_kv = n_swa_q // 3
    sizes = (conv_ch, n_diff_v, n_swa_q, n_diff_v, n_diff_v, n_kv, n_kv)
    assert sum(sizes) + conv_ch == w_in.shape[2]
    n_experts = w_router.shape[2]

    tm = 256
    assert lc % tm == 0 and n % tm == 0
    tm_ffn = _pick_tile(s, (768, 512, 256))
    tf_dense = _pick_tile(w_ff_gate.shape[2], (512, 256, 128))
    tf_moe = _pick_tile(w_ex_gate.shape[3], (512, 256, 128))
    tm_moe = 1024
    tm_gather = 512

    stream = (ctx, x)
    cvec = jnp.concatenate([c, c_ctx[None, :]], axis=0)
    mods = _ada(cvec, w_ada, b_ada).reshape(depth, b + 1, 6, d)
    tabs = _rope_tables(lc, n)

    out = None
    for l in range(depth):
        last = l == depth - 1
        lam_init = 0.8 - 0.6 * math.exp(-0.3 * l)
        mod = mods[l]
        lam_params = jnp.stack([lam_q1[l], lam_k1[l], lam_q2[l], lam_k2[l]], axis=0)
        y, qb, qc, kb, vb, kc, vc = _in_proj(stream, g_norm1[l], mod, w_in[l].astype(BF16), tabs, sizes, lc, tm)
        a = _conv_module(y, w_dw[l], b_dw[l], ln_g[l], ln_b[l], lc, tm)
        ob = _diff_attention(qb, kb, vb, lam_params, g_subln[l], lam_init, lc, False, tq=256, unroll=3)
        if not last:
            ob_ctx = _diff_attention(qb, kb, vb, lam_params, g_subln[l], lam_init, lc, True, tq=256, unroll=3)
            ob = jnp.concatenate([ob_ctx, ob], axis=1)
        oc = _swa(qc, kc, vc, sink[l], lc, not last, tq=WINDOW)
        xs, h2 = _out_proj(a, ob, oc, stream, w_out[l].astype(BF16), mod, g_norm2[l], lc, not last, tm,
                           F32 if l % 2 == 1 else BF16)
        i = l // 2
        if l % 2 == 0:
            assert not last, "a final dense layer would need its own final-norm epilogue"
            xs = _ffn_dense(h2, xs, w_ff_gate[i].astype(BF16), w_ff_up[i].astype(BF16),
                            w_ff_down[i].astype(BF16), mod, lc, tm_ffn, tf_dense)
            stream = (xs,)
        else:
            assert last, "the MoE layer carries the final norm and drops the context rows"
            h_flat = h2.reshape(b * n, d)
            idx, wts = _router(h_flat, w_router[i], tm)
            tok, tile_expert, tile_rows, n_active, pos0, pos1 = _dispatch_plan(idx[:, :TOP_K], n_experts, tm_moe)
            split = tm_moe // tm_gather
            xg = _gather_rows(h_flat, tok.reshape(-1, 1, tm_gather), n_active * split, tm_gather)
            yexp = _moe_ffn(xg, w_ex_gate[i], w_ex_up[i], w_ex_down[i], tile_expert, tile_rows, n_active,
                            tm_moe, tf_moe)
            nt = (b * n) // tm
            out = _combine(yexp, pos0.reshape(nt, 1, tm), pos1.reshape(nt, 1, tm), xs, wts, mod, g_final, tm)
    return out
```

```python
import functools
import math

import jax
import jax.numpy as jnp
from jax import lax
from jax.experimental import pallas as pl
from jax.experimental.pallas import tpu as pltpu

F32 = jnp.float32
BF16 = jnp.bfloat16

HEAD_DIM = 64
GRID_W = 64
ROPE_THETA = 10000.0
NORM_EPS = 1e-6
MASK_VALUE = -1e30
LOG2E = math.log2(math.e)
CONV_K = 31
CONV_HALO = 16
WINDOW = 128
TOP_K = 2
LANES = 128
SUBLANES = 8
ONES_ROWS = 16
VMEM_LIMIT = 56 * 1024 * 1024


def _cparams(sem):
    return pltpu.CompilerParams(dimension_semantics=sem, vmem_limit_bytes=VMEM_LIMIT)


def _rms_mod(x, g, shift, scale):
    y = x * lax.rsqrt(jnp.mean(x * x, axis=-1, keepdims=True) + NORM_EPS)
    return (y * g) * (1.0 + scale) + shift


def _sigmoid(x):
    return 1.0 / (1.0 + jnp.exp(-x))


def _ada_kernel(c_ref, w_ref, b_ref, o_ref):
    c = c_ref[...]
    s = (c * _sigmoid(c)).astype(BF16)
    o_ref[0] = jnp.dot(s, w_ref[0].astype(BF16), preferred_element_type=F32) + b_ref[0]


def _ada(cvec, w_ada, b_ada):
    depth, d, n6 = w_ada.shape
    rows = cvec.shape[0]
    tn = 1024
    return pl.pallas_call(
        _ada_kernel,
        out_shape=jax.ShapeDtypeStruct((depth, rows, n6), F32),
        grid=(depth, n6 // tn),
        in_specs=[pl.BlockSpec((rows, d), lambda l, j: (0, 0)),
                  pl.BlockSpec((1, d, tn), lambda l, j: (l, 0, j)),
                  pl.BlockSpec((1, 1, tn), lambda l, j: (l, 0, j))],
        out_specs=pl.BlockSpec((1, rows, tn), lambda l, j: (l, 0, j)),
        compiler_params=_cparams(("arbitrary", "arbitrary")),
        name="ada",
    )(cvec, w_ada, b_ada.reshape(depth, 1, n6))


def _in_kernel(xc_ref, xl_ref, g_ref, mod_ref, w_ref, cos_ref, sa_ref, sb_ref,
               y_ref, qb_ref, qc_ref, kb_ref, vb_ref, kc_ref, vc_ref, *, sizes, nctx):
    conv_ch, n_bq, n_cq, n_bk, n_bv, n_ck, n_cv = sizes
    x = jnp.where(pl.program_id(1) < nctx, xc_ref[0], xl_ref[0])
    hb = _rms_mod(x, g_ref[...], mod_ref[0, 0:1, :], mod_ref[0, 1:2, :]).astype(BF16)

    def mm(lo, width):
        return jnp.dot(hb, w_ref[:, lo:lo + width], preferred_element_type=F32)

    cos, sa, sb = cos_ref[...], sa_ref[...], sb_ref[...]

    def rope_store(u, out_ref, scale, transposed=False):
        for gi in range(u.shape[1] // LANES):
            ug = u[:, gi * LANES:(gi + 1) * LANES]
            r = (ug * cos + pltpu.roll(ug, LANES - 16, 1) * sa + pltpu.roll(ug, 16, 1) * sb) * scale
            if transposed:
                out_ref[0, gi * LANES:(gi + 1) * LANES, :] = r.T.astype(out_ref.dtype)
            else:
                out_ref[0, :, gi * LANES:(gi + 1) * LANES] = r.astype(out_ref.dtype)

    qscale = HEAD_DIM ** -0.5
    grp = LANES + ONES_ROWS

    def glu_store(u):
        y_ref[0] = u[:, :conv_ch] * _sigmoid(u[:, conv_ch:])

    def vb_store(v):
        for gi in range(n_bv // LANES):
            vb_ref[0, gi, 0, 0:LANES, :] = v[:, gi * LANES:(gi + 1) * LANES].T.astype(BF16)
            vb_ref[0, gi, 0, LANES:LANES + ONES_ROWS, :] = jnp.ones((ONES_ROWS, v.shape[0]), BF16)

    def vc_store(v):
        for gi in range(n_cv // LANES):
            vc_ref[0, gi * grp:gi * grp + LANES, :] = v[:, gi * LANES:(gi + 1) * LANES].T.astype(BF16)
            vc_ref[0, gi * grp + LANES:(gi + 1) * grp, :] = jnp.ones((ONES_ROWS, v.shape[0]), BF16)

    sections = [
        (2 * conv_ch, glu_store),
        (n_bq, functools.partial(rope_store, out_ref=qb_ref, scale=qscale * LOG2E, transposed=True)),
        (n_cq, functools.partial(rope_store, out_ref=qc_ref, scale=qscale * LOG2E, transposed=True)),
        (n_bk, functools.partial(rope_store, out_ref=kb_ref, scale=1.0)),
        (n_bv, vb_store),
        (n_ck, functools.partial(rope_store, out_ref=kc_ref, scale=1.0)),
        (n_cv, vc_store),
    ]
    off = 0
    pending = None
    for width, epilogue in sections:
        u = mm(off, width)
        off += width
        if pending is not None:
            pending[1](pending[0])
        pending = (u, epilogue)
    pending[1](pending[0])


def _stream_specs(parts, lc, tm, q_off=0):
    nctx = lc // tm
    if len(parts) == 1:
        arrs, lat_off = (parts[0], parts[0]), nctx
    else:
        arrs, lat_off = parts, 0
    d = arrs[0].shape[2]
    ctx_spec = pl.BlockSpec((1, tm, d), lambda bi, j: (bi, jnp.minimum(j + q_off, nctx - 1), 0))
    lat_spec = pl.BlockSpec((1, tm, d), lambda bi, j: (bi, jnp.maximum(j + q_off - nctx, 0) + lat_off, 0))
    return arrs, [ctx_spec, lat_spec]


def _in_proj(parts, g, mod, w_bf16, tabs, sizes, lc, tm):
    b, d = parts[0].shape[0], parts[0].shape[2]
    s = sum(p.shape[1] for p in parts)
    conv_ch, n_bq, n_cq, n_bk, n_bv, n_ck, n_cv = sizes
    d_in = w_bf16.shape[1]
    nctx = lc // tm
    n_mod = mod.shape[0]

    def row(bi, j):
        return (bi, j, 0)

    def modrow(bi, j):
        return (jnp.where(j < nctx, n_mod - 1, bi), 0, 0)

    widths = (conv_ch, n_bq, n_cq, n_bk, n_bv, n_ck, n_cv)
    dts = (F32, BF16, BF16, BF16, BF16, BF16, BF16)
    out_shape = [jax.ShapeDtypeStruct((b, s, w), dt) for w, dt in zip(widths, dts)]
    out_specs = [pl.BlockSpec((1, tm, w), row) for w in widths]
    n_vh = n_bv // LANES
    out_shape[1] = jax.ShapeDtypeStruct((b, n_bq, s), BF16)
    out_specs[1] = pl.BlockSpec((1, n_bq, tm), lambda bi, j: (bi, 0, j))
    out_shape[4] = jax.ShapeDtypeStruct((b, n_vh, s // tm, LANES + ONES_ROWS, tm), BF16)
    out_specs[4] = pl.BlockSpec((1, n_vh, 1, LANES + ONES_ROWS, tm), lambda bi, j: (bi, 0, j, 0, 0))
    for idx, wdt in ((2, n_cq), (6, (n_cv // LANES) * (LANES + ONES_ROWS))):
        out_shape[idx] = jax.ShapeDtypeStruct((b, wdt, s), BF16)
        out_specs[idx] = pl.BlockSpec((1, wdt, tm), lambda bi, j: (bi, 0, j))
    tab_spec = pl.BlockSpec((tm, LANES), lambda bi, j: (j, 0))
    stream, stream_specs = _stream_specs(parts, lc, tm)
    return pl.pallas_call(
        functools.partial(_in_kernel, sizes=sizes, nctx=nctx),
        out_shape=out_shape,
        grid=(b, s // tm),
        in_specs=stream_specs + [
                  pl.BlockSpec((1, d), lambda bi, j: (0, 0)),
                  pl.BlockSpec((1, 6, d), modrow),
                  pl.BlockSpec((d, d_in), lambda bi, j: (0, 0)),
                  tab_spec, tab_spec, tab_spec],
        out_specs=out_specs,
        compiler_params=_cparams(("arbitrary", "arbitrary")),
        name="in_proj",
    )(*stream, g.reshape(1, d), mod, w_bf16, *tabs)


def _conv_kernel(yp_ref, yc_ref, yn_ref, w_ref, b_ref, g_ref, bb_ref, o_ref, win_ref, *, tm, seg_tile, sub):
    j = pl.program_id(1)
    nj = pl.num_programs(1)
    has_prev = jnp.logical_and(j > 0, j != seg_tile)
    has_next = jnp.logical_and(j + 1 < nj, j + 1 != seg_tile)
    h = CONV_HALO
    win_ref[0, 0:h, :] = jnp.where(has_prev, yp_ref[0], 0.0)
    win_ref[0, h:h + tm, :] = yc_ref[0]
    win_ref[0, h + tm:h + tm + h, :] = jnp.where(has_next, yn_ref[0], 0.0)
    span = tm + 2 * h - SUBLANES
    for r in range(1, SUBLANES):
        win_ref[r, 0:span, :] = win_ref[0, pl.ds(r, span), :]
    base = h - CONV_K // 2
    for r0 in range(0, tm, sub):
        acc = jnp.zeros((sub, win_ref.shape[2]), F32)
        for t in range(CONV_K):
            o = base + r0 + t
            acc = acc + w_ref[t:t + 1, :] * win_ref[o % SUBLANES, o - o % SUBLANES:o - o % SUBLANES + sub, :]
        y = acc + b_ref[...]
        mu = jnp.mean(y, axis=-1, keepdims=True)
        yc = y - mu
        var = jnp.mean(yc * yc, axis=-1, keepdims=True)
        z = yc * lax.rsqrt(var + NORM_EPS) * g_ref[...] + bb_ref[...]
        o_ref[0, r0:r0 + sub, :] = (z * _sigmoid(z)).astype(o_ref.dtype)


def _conv_module(y, w_dw, b_dw, ln_g, ln_b, lc, tm):
    b, s, ch = y.shape
    h = CONV_HALO
    per = tm // h
    nblk = s // h

    def prev(bi, j):
        return (bi, jnp.maximum(j * per - 1, 0), 0)

    def nxt(bi, j):
        return (bi, jnp.minimum((j + 1) * per, nblk - 1), 0)

    vec = pl.BlockSpec((1, ch), lambda bi, j: (0, 0))
    return pl.pallas_call(
        functools.partial(_conv_kernel, tm=tm, seg_tile=lc // tm, sub=32),
        out_shape=jax.ShapeDtypeStruct((b, s, ch), BF16),
        grid=(b, s // tm),
        in_specs=[pl.BlockSpec((1, h, ch), prev),
                  pl.BlockSpec((1, tm, ch), lambda bi, j: (bi, j, 0)),
                  pl.BlockSpec((1, h, ch), nxt),
                  pl.BlockSpec((CONV_K, ch), lambda bi, j: (0, 0)),
                  vec, vec, vec],
        out_specs=pl.BlockSpec((1, tm, ch), lambda bi, j: (bi, j, 0)),
        scratch_shapes=[pltpu.VMEM((SUBLANES, tm + 2 * h, ch), F32)],
        compiler_params=_cparams(("arbitrary", "arbitrary")),
        name="conv_module",
    )(y, y, y, w_dw.reshape(CONV_K, ch), b_dw.reshape(1, ch), ln_g.reshape(1, ch), ln_b.reshape(1, ch))


def _diff_kernel(lamp_ref, qt_ref, k_ref, vt_ref, g_ref, o_ref, m_scr, acc_scr, sa_scr, sb_scr, sc_scr, rhs_scr,
                 *, tq, tk, unroll, context_only, ctx_chunks, all_chunks, lam_init):
    qt = qt_ref[0]
    sub = lax.broadcasted_iota(jnp.int32, qt.shape, 0)
    zero = jnp.zeros_like(qt)
    rhs_scr[:, 0:tq] = jnp.where(sub < HEAD_DIM, qt, zero)
    rhs_scr[:, tq:2 * tq] = jnp.where(sub >= HEAD_DIM, qt, zero)
    m_scr[...] = jnp.full(m_scr.shape, MASK_VALUE, F32)
    acc_scr[...] = jnp.zeros(acc_scr.shape, F32)

    def scores(gidx, count, s_ref):
        mxs = []
        for half in range(2):
            cs = slice(half * tq, (half + 1) * tq)
            mx = None
            for u in range(count):
                start = pl.multiple_of((gidx * count + u) * tk, tk)
                s = jnp.dot(k_ref[0, pl.ds(start, tk), :], rhs_scr[:, cs], preferred_element_type=F32)
                s_ref[u * tk:(u + 1) * tk, cs] = s
                cm = jnp.max(s, axis=0, keepdims=True)
                mx = cm if mx is None else jnp.maximum(mx, cm)
            mxs.append(mx)
        return mxs

    def accumulate(gidx, count, s_ref, mxs):
        for half in range(2):
            cs = slice(half * tq, (half + 1) * tq)
            m_prev = m_scr[:, cs]
            m_new = jnp.maximum(m_prev, mxs[half])
            alpha = jnp.exp2(m_prev - m_new)
            pv = None
            for u in range(count):
                p = jnp.exp2((s_ref[u * tk:(u + 1) * tk, cs] - m_new).astype(BF16))
                d = jnp.dot(vt_ref[0, 0, gidx * count + u], p, preferred_element_type=F32)
                pv = d if pv is None else pv + d
            acc_scr[:, cs] = alpha * acc_scr[:, cs] + pv
            m_scr[:, cs] = m_new

    bufs = (sa_scr, sb_scr, sc_scr)

    def accumulate_and_score(g_acc, g_new, mxs):
        s_acc, s_new = bufs[g_acc % 3], bufs[g_new % 3]
        new_mxs = []
        for half in range(2):
            cs = slice(half * tq, (half + 1) * tq)
            m_prev = m_scr[:, cs]
            m_new = jnp.maximum(m_prev, mxs[half])
            alpha = jnp.exp2(m_prev - m_new)
            pv = None
            mx = None
            for u in range(unroll):
                p = jnp.exp2((s_acc[u * tk:(u + 1) * tk, cs] - m_new).astype(BF16))
                d = jnp.dot(vt_ref[0, 0, g_acc * unroll + u], p, preferred_element_type=F32)
                pv = d if pv is None else pv + d
                start = pl.multiple_of((g_new * unroll + u) * tk, tk)
                s = jnp.dot(k_ref[0, pl.ds(start, tk), :], rhs_scr[:, cs], preferred_element_type=F32)
                s_new[u * tk:(u + 1) * tk, cs] = s
                cm = jnp.max(s, axis=0, keepdims=True)
                mx = cm if mx is None else jnp.maximum(mx, cm)
            acc_scr[:, cs] = alpha * acc_scr[:, cs] + pv
            m_scr[:, cs] = m_new
            new_mxs.append(mx)
        return new_mxs

    def context_keys_only():
        for c in range(ctx_chunks):
            accumulate(c, 1, sa_scr, scores(c, 1, sa_scr))

    def all_keys():
        n_groups = all_chunks // unroll
        mx = [scores(g, unroll, bufs[g % 3]) for g in range(min(2, n_groups))]
        for g in range(n_groups):
            if g + 2 < n_groups:
                mx.append(accumulate_and_score(g, g + 2, mx[g]))
            else:
                accumulate(g, unroll, bufs[g % 3], mx[g])

    if context_only:
        context_keys_only()
    else:
        all_keys()

    vdim = 2 * HEAD_DIM
    o = acc_scr[0:vdim, :] / acc_scr[vdim:vdim + 1, :]
    lp = lamp_ref[...]
    lam = (jnp.exp(jnp.sum(lp[0:1] * lp[1:2], axis=1, keepdims=True))
           - jnp.exp(jnp.sum(lp[2:3] * lp[3:4], axis=1, keepdims=True)) + lam_init)
    od = o[:, :tq] - lam * o[:, tq:]
    y = od * lax.rsqrt(jnp.mean(od * od, axis=0, keepdims=True) + NORM_EPS)
    o_ref[0] = (y * g_ref[...] * (1.0 - lam_init)).T.astype(o_ref.dtype)


def _diff_attention(qbt, kb, vbt, lam_params, g_subln, lam_init, lc, context_queries, tq, unroll):
    b, s, w = kb.shape
    heads = w // (2 * HEAD_DIM)
    vrows, tk = vbt.shape[3], vbt.shape[4]
    assert vrows == 2 * HEAD_DIM + ONES_ROWS
    q_off = 0 if context_queries else lc // tq
    nq = lc // tq if context_queries else (s - lc) // tq
    n_keys = lc if context_queries else s
    assert (s // tk) % unroll == 0 and lc % tk == 0
    kern = functools.partial(_diff_kernel, tq=tq, tk=tk, unroll=unroll, context_only=context_queries,
                             ctx_chunks=lc // tk, all_chunks=s // tk, lam_init=lam_init)
    return pl.pallas_call(
        kern,
        out_shape=jax.ShapeDtypeStruct((b, nq * tq, w), BF16),
        grid=(b, heads, nq),
        in_specs=[pl.BlockSpec((4, HEAD_DIM), lambda bi, h, i: (0, 0)),
                  pl.BlockSpec((1, 2 * HEAD_DIM, tq), lambda bi, h, i: (bi, h, i + q_off)),
                  pl.BlockSpec((1, n_keys, 2 * HEAD_DIM), lambda bi, h, i: (bi, 0, h)),
                  pl.BlockSpec((1, 1, n_keys // tk, vrows, tk), lambda bi, h, i: (bi, h, 0, 0, 0)),
                  pl.BlockSpec((2 * HEAD_DIM, 1), lambda bi, h, i: (0, 0))],
        out_specs=pl.BlockSpec((1, tq, 2 * HEAD_DIM), lambda bi, h, i: (bi, i, h)),
        scratch_shapes=[pltpu.VMEM((1, 2 * tq), F32),
                        pltpu.VMEM((vrows, 2 * tq), F32),
                        pltpu.VMEM((unroll * tk, 2 * tq), F32), pltpu.VMEM((unroll * tk, 2 * tq), F32),
                        pltpu.VMEM((unroll * tk, 2 * tq), F32),
                        pltpu.VMEM((2 * HEAD_DIM, 2 * tq), BF16)],
        compiler_params=_cparams(("arbitrary", "arbitrary", "arbitrary")),
        name="diff_attention",
    )(lam_params, qbt, kb, vbt, g_subln.reshape(2 * HEAD_DIM, 1))


def _swa_kernel(sink_ref, qt_ref, kctx_ref, vtctx_ref, kp_ref, kc_ref, kn_ref, vtp_ref, vtc_ref, vtn_ref, o_ref,
                *, tq, q_off, n_ctx_qblk, n_lat, kv_heads, group):
    i = pl.program_id(1) + q_off
    is_lat = i >= n_ctx_qblk
    q0 = (i - n_ctx_qblk) * tq
    wq = group * tq
    kpos = q0 - tq + lax.broadcasted_iota(jnp.int32, (3 * tq, wq), 0)
    qpos = q0 + jnp.bitwise_and(lax.broadcasted_iota(jnp.int32, (3 * tq, wq), 1), tq - 1)
    valid = jnp.logical_and(jnp.abs(qpos - kpos) <= WINDOW, jnp.logical_and(kpos >= 0, kpos < n_lat))
    valid = jnp.logical_and(valid, is_lat)
    qt = qt_ref[0]
    kloc = jnp.concatenate([kp_ref[0], kc_ref[0], kn_ref[0]], axis=0)
    vtloc = jnp.concatenate([vtp_ref[0], vtc_ref[0], vtn_ref[0]], axis=1)
    kctx = kctx_ref[0]
    vtctx = vtctx_ref[0]
    zeros = jnp.zeros((HEAD_DIM, tq), qt.dtype)
    heads_out = []

    def head_scores(h):
        gs = slice((h // 2) * LANES, (h // 2 + 1) * LANES)
        upper = h % 2 == 1
        cols = []
        for gi in range(group):
            hq = h * group + gi
            qh = qt[hq * HEAD_DIM:(hq + 1) * HEAD_DIM, :]
            cols.append(jnp.concatenate([zeros, qh] if upper else [qh, zeros], axis=0))
        rhs = jnp.concatenate(cols, axis=1)
        sink = jnp.concatenate(
            [jnp.full((1, tq), sink_ref[h * group + gi] * LOG2E, F32) for gi in range(group)], axis=1)
        s_ctx = jnp.dot(kctx[:, gs], rhs, preferred_element_type=F32)
        s_loc = jnp.dot(kloc[:, gs], rhs, preferred_element_type=F32)
        s_loc = jnp.where(valid, s_loc, MASK_VALUE)
        m = jnp.maximum(jnp.maximum(jnp.max(s_ctx, axis=0, keepdims=True), jnp.max(s_loc, axis=0, keepdims=True)), sink)
        return s_ctx, s_loc, m, sink

    def head_output(h, s_ctx, s_loc, m, sink):
        vs = slice((h // 2) * (LANES + ONES_ROWS), (h // 2 + 1) * (LANES + ONES_ROWS))
        upper = h % 2 == 1
        e_ctx = jnp.exp2((s_ctx - m).astype(BF16))
        e_loc = jnp.exp2((s_loc - m).astype(BF16))
        ot = (jnp.dot(vtctx[vs, :], e_ctx, preferred_element_type=F32)
              + jnp.dot(vtloc[vs, :], e_loc, preferred_element_type=F32))
        denom = ot[LANES:LANES + 1, :] + jnp.exp2(sink - m)
        ot = (ot[HEAD_DIM:LANES, :] if upper else ot[:HEAD_DIM, :]) / denom
        for gi in range(group):
            heads_out.append(ot[:, gi * tq:(gi + 1) * tq])

    cur = head_scores(0)
    for h in range(kv_heads):
        nxt = head_scores(h + 1) if h + 1 < kv_heads else None
        head_output(h, *cur)
        cur = nxt
    for pi in range(len(heads_out) // 2):
        pair = jnp.concatenate([heads_out[2 * pi], heads_out[2 * pi + 1]], axis=0)
        o_ref[0, :, pi * LANES:(pi + 1) * LANES] = pair.T.astype(o_ref.dtype)


def _swa(qct, kc, vct, sink, lc, with_ctx, tq):
    b, s, kvw = kc.shape
    w = qct.shape[1]
    kv_heads = kvw // HEAD_DIM
    group = w // kvw
    assert tq & (tq - 1) == 0 and kv_heads % 2 == 0
    q_off = 0 if with_ctx else lc // tq
    nq = s // tq - q_off
    nblk = s // tq

    def rows(f):
        return pl.BlockSpec((1, tq, kvw), lambda bi, i: (bi, f(i + q_off), 0))

    vrows = vct.shape[1]
    assert vrows == (kvw // LANES) * (LANES + ONES_ROWS)

    def cols(f):
        return pl.BlockSpec((1, vrows, tq), lambda bi, i: (bi, 0, f(i + q_off)))

    cur = lambda i: i
    prev = lambda i: jnp.maximum(i - 1, 0)
    nxt = lambda i: jnp.minimum(i + 1, nblk - 1)
    kern = functools.partial(_swa_kernel, tq=tq, q_off=q_off, n_ctx_qblk=lc // tq, n_lat=s - lc,
                             kv_heads=kv_heads, group=group)
    return pl.pallas_call(
        kern,
        out_shape=jax.ShapeDtypeStruct((b, nq * tq, w), BF16),
        grid=(b, nq),
        in_specs=[pl.BlockSpec(memory_space=pltpu.SMEM),
                  pl.BlockSpec((1, w, tq), lambda bi, i: (bi, 0, i + q_off)),
                  pl.BlockSpec((1, lc, kvw), lambda bi, i: (bi, 0, 0)),
                  pl.BlockSpec((1, vrows, lc), lambda bi, i: (bi, 0, 0)),
                  rows(prev), rows(cur), rows(nxt), cols(prev), cols(cur), cols(nxt)],
        out_specs=pl.BlockSpec((1, tq, w), lambda bi, i: (bi, i, 0)),
        compiler_params=_cparams(("arbitrary", "arbitrary")),
        name="swa",
    )(sink, qct, kc, vct, kc, kc, kc, vct, vct, vct)


def _out_kernel(a_ref, ob_ref, oc_ref, xc_ref, xl_ref, w_ref, mod_ref, g_ref, xo_ref, h_ref, *, n_ctx_tiles):
    na, nb = a_ref.shape[2], ob_ref.shape[2]
    mix = jnp.dot(a_ref[0], w_ref[0:na, :], preferred_element_type=F32)
    mix = mix + jnp.dot(ob_ref[0], w_ref[na:na + nb, :], preferred_element_type=F32)
    mix = mix + jnp.dot(oc_ref[0], w_ref[na + nb:, :], preferred_element_type=F32)
    x = jnp.where(pl.program_id(1) < n_ctx_tiles, xc_ref[0], xl_ref[0])
    xn = x + mod_ref[0, 2:3, :] * mix
    xo_ref[0] = xn
    h_ref[0] = _rms_mod(xn, g_ref[...], mod_ref[0, 3:4, :], mod_ref[0, 4:5, :]).astype(h_ref.dtype)


def _out_proj(a, ob, oc, parts, w_bf16, mod, g2, lc, with_ctx, tm, h_dtype):
    b, d = parts[0].shape[0], parts[0].shape[2]
    s = sum(p.shape[1] for p in parts)
    nctx = lc // tm
    q_off = 0 if with_ctx else nctx
    nt = s // tm - q_off
    n_mod = mod.shape[0]
    out_rows = s - q_off * tm

    def row(bi, j):
        return (bi, j + q_off, 0)

    def modrow(bi, j):
        return (jnp.where(j + q_off < nctx, n_mod - 1, bi), 0, 0)

    def own(bi, j):
        return (bi, j, 0)

    assert ob.shape[1] == out_rows and oc.shape[1] == out_rows
    stream, stream_specs = _stream_specs(parts, lc, tm, q_off)
    return pl.pallas_call(
        functools.partial(_out_kernel, n_ctx_tiles=nctx - q_off),
        out_shape=(jax.ShapeDtypeStruct((b, out_rows, d), F32), jax.ShapeDtypeStruct((b, out_rows, d), h_dtype)),
        grid=(b, nt),
        in_specs=[pl.BlockSpec((1, tm, a.shape[2]), row), pl.BlockSpec((1, tm, ob.shape[2]), own),
                  pl.BlockSpec((1, tm, oc.shape[2]), own)] + stream_specs + [
                  pl.BlockSpec(w_bf16.shape, lambda bi, j: (0, 0)),
                  pl.BlockSpec((1, 6, d), modrow),
                  pl.BlockSpec((1, d), lambda bi, j: (0, 0))],
        out_specs=(pl.BlockSpec((1, tm, d), own), pl.BlockSpec((1, tm, d), own)),
        compiler_params=_cparams(("arbitrary", "arbitrary")),
        name="out_proj",
    )(a, ob, oc, *stream, w_bf16, mod, g2.reshape(1, d))


def _ffn_kernel(h_ref, x_ref, wg_ref, wu_ref, wd_ref, modb_ref, modc_ref, o_ref, *, tm, lc):
    f = pl.program_id(2)

    @pl.when(f == 0)
    def _():
        o_ref[...] = jnp.zeros(o_ref.shape, F32)

    h = h_ref[0]
    g = jnp.dot(h, wg_ref[...], preferred_element_type=F32)
    u = jnp.dot(h, wu_ref[...], preferred_element_type=F32)
    act = (g * _sigmoid(g) * u).astype(BF16)
    o_ref[0] += jnp.dot(act, wd_ref[...], preferred_element_type=F32)

    @pl.when(f == pl.num_programs(2) - 1)
    def _():
        rows = pl.program_id(1) * tm + lax.broadcasted_iota(jnp.int32, (tm, 1), 0)
        gate = jnp.where(rows < lc, modc_ref[0, 5:6, :], modb_ref[0, 5:6, :])
        o_ref[0] = x_ref[0] + gate * o_ref[0]


def _ffn_dense(h, x, wg, wu, wd, mod, lc, tm, tf):
    b, s, d = x.shape
    dff = wg.shape[1]
    n_mod = mod.shape[0]
    row = lambda bi, j, f: (bi, j, 0)
    return pl.pallas_call(
        functools.partial(_ffn_kernel, tm=tm, lc=lc),
        out_shape=jax.ShapeDtypeStruct((b, s, d), F32),
        grid=(b, s // tm, dff // tf),
        in_specs=[pl.BlockSpec((1, tm, d), row), pl.BlockSpec((1, tm, d), row),
                  pl.BlockSpec((d, tf), lambda bi, j, f: (0, f)),
                  pl.BlockSpec((d, tf), lambda bi, j, f: (0, f)),
                  pl.BlockSpec((tf, d), lambda bi, j, f: (f, 0)),
                  pl.BlockSpec((1, 6, d), lambda bi, j, f: (bi, 0, 0)),
                  pl.BlockSpec((1, 6, d), lambda bi, j, f: (n_mod - 1, 0, 0))],
        out_specs=pl.BlockSpec((1, tm, d), row),
        compiler_params=_cparams(("arbitrary", "arbitrary", "arbitrary")),
        name="ffn_dense",
    )(h, x, wg, wu, wd, mod, mod)


def _router_kernel(h_ref, w_ref, idx_ref, wt_ref):
    h = h_ref[...]
    w = w_ref[...]
    h_hi = h.astype(BF16)
    w_hi = w.astype(BF16)
    h_lo = (h - h_hi.astype(F32)).astype(BF16)
    w_lo = (w - w_hi.astype(F32)).astype(BF16)
    logits = (jnp.dot(h_hi, w_hi, preferred_element_type=F32) + jnp.dot(h_hi, w_lo, preferred_element_type=F32)
              + jnp.dot(h_lo, w_hi, preferred_element_type=F32))
    n_e = logits.shape[1]
    lane = lax.broadcasted_iota(jnp.int32, logits.shape, 1)
    m1 = jnp.max(logits, axis=1, keepdims=True)
    i1 = jnp.min(jnp.where(logits == m1, lane, n_e), axis=1, keepdims=True)
    rest = jnp.where(lane == i1, -jnp.inf, logits)
    m2 = jnp.max(rest, axis=1, keepdims=True)
    i2 = jnp.min(jnp.where(rest == m2, lane, n_e), axis=1, keepdims=True)
    e2 = jnp.exp(m2 - m1)
    den = 1.0 + e2
    out_lane = lax.broadcasted_iota(jnp.int32, idx_ref.shape, 1)
    idx_ref[...] = jnp.where(out_lane == 0, i1, jnp.where(out_lane == 1, i2, 0))
    wt_ref[...] = jnp.where(out_lane == 0, 1.0 / den, jnp.where(out_lane == 1, e2 / den, 0.0))


def _router(h_flat, w_router, tm):
    t, d = h_flat.shape
    n_e = w_router.shape[1]
    return pl.pallas_call(
        _router_kernel,
        out_shape=(jax.ShapeDtypeStruct((t, LANES), jnp.int32), jax.ShapeDtypeStruct((t, LANES), F32)),
        grid=(t // tm,),
        in_specs=[pl.BlockSpec((tm, d), lambda i: (i, 0)), pl.BlockSpec((d, n_e), lambda i: (0, 0))],
        out_specs=(pl.BlockSpec((tm, LANES), lambda i: (i, 0)), pl.BlockSpec((tm, LANES), lambda i: (i, 0))),
        compiler_params=_cparams(("arbitrary",)),
        name="router",
    )(h_flat, w_router)


def _row_copy(src_hbm, row, dst, r, sem):
    return pltpu.make_async_copy(src_hbm.at[pl.ds(row, 1)], dst.at[pl.ds(r, 1)], sem)


def _start_rows(src_hbm, idx_ref, dst, sem, n):
    group = 8
    assert n % group == 0

    def body(gi, carry):
        for j in range(group):
            r = gi * group + j
            _row_copy(src_hbm, idx_ref[0, 0, r], dst, r, sem).start(priority=j % 2)
        return carry

    lax.fori_loop(0, n // group, body, 0)


def _wait_rows(src_hbm, dst, sem, n):
    def body(r, carry):
        _row_copy(src_hbm, 0, dst, r, sem).wait()
        return carry

    lax.fori_loop(0, n, body, 0, unroll=8)


def _gather_kernel(nact_ref, tok_ref, tok_next_ref, h_hbm, o_ref, buf, sem, *, tm):
    i = pl.program_id(0)
    nact = nact_ref[0]
    slot = lax.rem(i, 2)

    @pl.when(jnp.logical_and(i == 0, nact > 0))
    def _():
        _start_rows(h_hbm, tok_ref, buf.at[0], sem.at[0], tm)

    @pl.when(i + 1 < nact)
    def _():
        _start_rows(h_hbm, tok_next_ref, buf.at[1 - slot], sem.at[1 - slot], tm)

    @pl.when(i < nact)
    def _():
        _wait_rows(h_hbm, buf.at[slot], sem.at[slot], tm)
        o_ref[...] = buf[slot].astype(o_ref.dtype)

    @pl.when(i >= nact)
    def _():
        o_ref[...] = jnp.zeros(o_ref.shape, o_ref.dtype)


def _gather_rows(h_flat, tok, n_active, tm):
    n_tiles = tok.shape[0]
    d = h_flat.shape[1]
    grid_spec = pltpu.PrefetchScalarGridSpec(
        num_scalar_prefetch=1,
        grid=(n_tiles,),
        in_specs=[pl.BlockSpec((1, 1, tm), lambda i, na: (i, 0, 0), memory_space=pltpu.SMEM),
                  pl.BlockSpec((1, 1, tm), lambda i, na: (jnp.minimum(i + 1, n_tiles - 1), 0, 0),
                               memory_space=pltpu.SMEM),
                  pl.BlockSpec(memory_space=pl.ANY)],
        out_specs=pl.BlockSpec((tm, d), lambda i, na: (i, 0)),
        scratch_shapes=[pltpu.VMEM((2, tm, d), h_flat.dtype), pltpu.SemaphoreType.DMA((2,))],
    )
    return pl.pallas_call(
        functools.partial(_gather_kernel, tm=tm),
        out_shape=jax.ShapeDtypeStruct((n_tiles * tm, d), BF16),
        grid_spec=grid_spec,
        compiler_params=_cparams(("arbitrary",)),
        name="moe_gather",
    )(n_active, tok, tok, h_flat)


def _moe_ffn_kernel(te_ref, rows_ref, nact_ref, x_ref, wg_ref, wu_ref, wd_ref, o_ref):
    i = pl.program_id(0)
    f = pl.program_id(1)
    half = o_ref.shape[0] // 2

    @pl.when(f == 0)
    def _():
        o_ref[...] = jnp.zeros(o_ref.shape, o_ref.dtype)

    def swiglu_rows(n_rows):
        x = x_ref[0:n_rows, :]
        g = jnp.dot(x, wg_ref[0].astype(BF16), preferred_element_type=F32)
        u = jnp.dot(x, wu_ref[0].astype(BF16), preferred_element_type=F32)
        act = (g * _sigmoid(g) * u).astype(BF16)
        o_ref[0:n_rows, :] += jnp.dot(act, wd_ref[0].astype(BF16), preferred_element_type=F32)

    rows = rows_ref[i]
    pl.when(rows > half)(functools.partial(swiglu_rows, o_ref.shape[0]))
    pl.when(jnp.logical_and(rows > 0, rows <= half))(functools.partial(swiglu_rows, half))


def _moe_ffn(xs, wg, wu, wd, tile_expert, tile_rows, n_active, tm, tf):
    p, d = xs.shape
    dff = wg.shape[2]
    nf = dff // tf
    n_tiles = p // tm

    def tile(i, f, te, tr, na):
        return (jnp.minimum(i, na[0] - 1), 0)

    def fcol(i, f, na):
        return jnp.where(i < na[0], f, nf - 1)

    grid_spec = pltpu.PrefetchScalarGridSpec(
        num_scalar_prefetch=3,
        grid=(n_tiles, nf),
        in_specs=[pl.BlockSpec((tm, d), tile),
                  pl.BlockSpec((1, d, tf), lambda i, f, te, tr, na: (te[i], 0, fcol(i, f, na))),
                  pl.BlockSpec((1, d, tf), lambda i, f, te, tr, na: (te[i], 0, fcol(i, f, na))),
                  pl.BlockSpec((1, tf, d), lambda i, f, te, tr, na: (te[i], fcol(i, f, na), 0))],
        out_specs=pl.BlockSpec((tm, d), lambda i, f, te, tr, na: (i, 0)),
    )
    return pl.pallas_call(
        _moe_ffn_kernel,
        out_shape=jax.ShapeDtypeStruct((p, d), F32),
        grid_spec=grid_spec,
        compiler_params=_cparams(("arbitrary", "arbitrary")),
        name="moe_ffn",
    )(tile_expert, tile_rows, n_active, xs, wg, wu, wd)


def _combine_kernel(p0_ref, p1_ref, p0n_ref, p1n_ref, y_hbm, x_ref, wt_ref, mod_ref, g_ref, o_ref, buf, sem, *, tm):
    t = pl.program_id(0) * pl.num_programs(1) + pl.program_id(1)
    n_tiles = pl.num_programs(0) * pl.num_programs(1)
    slot = lax.rem(t, 2)

    def start(r0, r1, s):
        _start_rows(y_hbm, r0, buf.at[s, 0], sem.at[s, 0], tm)
        _start_rows(y_hbm, r1, buf.at[s, 1], sem.at[s, 1], tm)

    @pl.when(t == 0)
    def _():
        start(p0_ref, p1_ref, 0)

    @pl.when(t + 1 < n_tiles)
    def _():
        start(p0n_ref, p1n_ref, 1 - slot)

    _wait_rows(y_hbm, buf.at[slot, 0], sem.at[slot, 0], tm)
    _wait_rows(y_hbm, buf.at[slot, 1], sem.at[slot, 1], tm)
    wt = wt_ref[...]
    f = wt[:, 0:1] * buf[slot, 0] + wt[:, 1:2] * buf[slot, 1]
    xn = x_ref[0] + mod_ref[0, 5:6, :] * f
    y = xn * lax.rsqrt(jnp.mean(xn * xn, axis=-1, keepdims=True) + NORM_EPS)
    o_ref[0] = y * g_ref[...]


def _combine(y, pos0, pos1, x, wts, mod, g_final, tm):
    b, n, d = x.shape
    nt = n // tm
    last = b * nt - 1
    idx = lambda f: pl.BlockSpec((1, 1, tm), lambda bi, j: (f(bi * nt + j), 0, 0), memory_space=pltpu.SMEM)
    cur = lambda t: t
    nxt = lambda t: jnp.minimum(t + 1, last)
    return pl.pallas_call(
        functools.partial(_combine_kernel, tm=tm),
        out_shape=jax.ShapeDtypeStruct((b, n, d), F32),
        grid=(b, nt),
        in_specs=[idx(cur), idx(cur), idx(nxt), idx(nxt),
                  pl.BlockSpec(memory_space=pl.ANY),
                  pl.BlockSpec((1, tm, d), lambda bi, j: (bi, j, 0)),
                  pl.BlockSpec((tm, LANES), lambda bi, j: (bi * nt + j, 0)),
                  pl.BlockSpec((1, 6, d), lambda bi, j: (bi, 0, 0)),
                  pl.BlockSpec((1, d), lambda bi, j: (0, 0))],
        out_specs=pl.BlockSpec((1, tm, d), lambda bi, j: (bi, j, 0)),
        scratch_shapes=[pltpu.VMEM((2, 2, tm, d), F32), pltpu.SemaphoreType.DMA((2, 2))],
        compiler_params=_cparams(("arbitrary", "arbitrary")),
        name="moe_combine",
    )(pos0, pos1, pos0, pos1, y, x, wts, mod, g_final.reshape(1, d))


def _dispatch_plan(idx, n_experts, tm):
    t = idx.shape[0]
    e_flat = idx.reshape(-1)
    onehot = (e_flat[:, None] == jnp.arange(n_experts, dtype=jnp.int32)[None, :]).astype(jnp.int32)
    rank = jnp.sum((jnp.cumsum(onehot, axis=0) - 1) * onehot, axis=1)
    counts = jnp.sum(onehot, axis=0)
    padded = ((counts + tm - 1) // tm) * tm
    ends = jnp.cumsum(padded)
    starts = ends - padded
    dest = starts[e_flat] + rank
    n_tiles = (TOP_K * t) // tm + n_experts
    tok = jnp.zeros((n_tiles * tm,), jnp.int32).at[dest].set(jnp.arange(TOP_K * t, dtype=jnp.int32) // TOP_K)
    tile_row0 = jnp.arange(n_tiles, dtype=jnp.int32) * tm
    tile_expert = jnp.sum((ends[None, :] <= tile_row0[:, None]).astype(jnp.int32), axis=1)
    tile_expert = jnp.minimum(tile_expert, n_experts - 1)
    n_active = (ends[-1] // tm).astype(jnp.int32).reshape(1)
    active = jnp.arange(n_tiles) < n_active[0]
    tile_rows = jnp.clip((starts + counts)[tile_expert] - tile_row0, 0, tm)
    tile_rows = jnp.where(active, tile_rows, 0).astype(jnp.int32)
    last = jnp.maximum(n_active[0] - 1, 0)
    tile_expert = jnp.where(active, tile_expert, tile_expert[last])
    pos = dest.reshape(t, TOP_K)
    return tok.reshape(n_tiles, 1, tm), tile_expert, tile_rows, n_active, pos[:, 0], pos[:, 1]


def _rope_tables(lc, n):
    rows = n // GRID_W
    row = jnp.repeat(jnp.arange(rows), GRID_W).astype(F32)
    col = jnp.tile(jnp.arange(GRID_W), rows).astype(F32)
    axis_dim = HEAD_DIM // 2
    inv = ROPE_THETA ** (-jnp.arange(axis_dim // 2, dtype=F32) / (axis_dim // 2))
    ang_r = row[:, None] * inv[None, :]
    ang_c = col[:, None] * inv[None, :]
    ang = jnp.concatenate([ang_r, ang_r, ang_c, ang_c], axis=-1)
    cos, sin = jnp.cos(ang), jnp.sin(ang)
    first_half = (jnp.arange(HEAD_DIM) % (HEAD_DIM // 2)) < (HEAD_DIM // 4)
    sa = jnp.where(first_half[None, :], -sin, 0.0)
    sb = jnp.where(first_half[None, :], 0.0, sin)
    ident = lambda v: jnp.full((lc, HEAD_DIM), v, F32)
    full = lambda ctx_v, t: jnp.tile(jnp.concatenate([ident(ctx_v), t], axis=0), (1, LANES // HEAD_DIM))
    return full(1.0, cos), full(0.0, sa), full(0.0, sb)


def _pick_tile(n, candidates):
    for c in candidates:
        if n % c == 0:
            return c
    raise ValueError(f"no tile size among {candidates} divides {n}")


def kernel(x, c, ctx, c_ctx, w_ada, b_ada, g_norm1, g_norm2, w_in, w_dw, b_dw, ln_g, ln_b, lam_q1, lam_k1, lam_q2, lam_k2, g_subln, sink, w_out, w_ff_gate, w_ff_up, w_ff_down, w_router, w_ex_gate, w_ex_up, w_ex_down, g_final):
    b, n, d = x.shape
    lc = ctx.shape[1]
    s = lc + n
    depth = w_ada.shape[0]
    conv_ch = w_dw.shape[-1]
    n_swa_q = sink.shape[1] * HEAD_DIM
    n_diff_v = d - conv_ch - n_swa_q
    n_kv = n_swa_q // 3
    sizes = (conv_ch, n_diff_v, n_swa_q, n_diff_v, n_diff_v, n_kv, n_kv)
    assert sum(sizes) + conv_ch == w_in.shape[2]
    n_experts = w_router.shape[2]

    tm = 256
    assert lc % tm == 0 and n % tm == 0
    tm_ffn = _pick_tile(s, (768, 512, 256))
    tf_dense = _pick_tile(w_ff_gate.shape[2], (512, 256, 128))
    tf_moe = _pick_tile(w_ex_gate.shape[3], (256, 128))
    tm_moe = 1024
    tm_gather = 512

    stream = (ctx, x)
    cvec = jnp.concatenate([c, c_ctx[None, :]], axis=0)
    mods = _ada(cvec, w_ada, b_ada).reshape(depth, b + 1, 6, d)
    tabs = _rope_tables(lc, n)

    out = None
    for l in range(depth):
        last = l == depth - 1
        lam_init = 0.8 - 0.6 * math.exp(-0.3 * l)
        mod = mods[l]
        lam_params = jnp.stack([lam_q1[l], lam_k1[l], lam_q2[l], lam_k2[l]], axis=0)
        y, qb, qc, kb, vb, kc, vc = _in_proj(stream, g_norm1[l], mod, w_in[l].astype(BF16), tabs, sizes, lc, tm)
        a = _conv_module(y, w_dw[l], b_dw[l], ln_g[l], ln_b[l], lc, tm)
        ob = _diff_attention(qb, kb, vb, lam_params, g_subln[l], lam_init, lc, False, tq=256, unroll=3)
        if not last:
            ob_ctx = _diff_attention(qb, kb, vb, lam_params, g_subln[l], lam_init, lc, True, tq=256, unroll=3)
            ob = jnp.concatenate([ob_ctx, ob], axis=1)
        oc = _swa(qc, kc, vc, sink[l], lc, not last, tq=WINDOW)
        xs, h2 = _out_proj(a, ob, oc, stream, w_out[l].astype(BF16), mod, g_norm2[l], lc, not last, tm,
                           F32 if l % 2 == 1 else BF16)
        i = l // 2
        if l % 2 == 0:
            assert not last, "a final dense layer would need its own final-norm epilogue"
            xs = _ffn_dense(h2, xs, w_ff_gate[i].astype(BF16), w_ff_up[i].astype(BF16),
                            w_ff_down[i].astype(BF16), mod, lc, tm_ffn, tf_dense)
            stream = (xs,)
        else:
            assert last, "the MoE layer carries the final norm and drops the context rows"
            h_flat = h2.reshape(b * n, d)
            idx, wts = _router(h_flat, w_router[i], tm)
            tok, tile_expert, tile_rows, n_active, pos0, pos1 = _dispatch_plan(idx[:, :TOP_K], n_experts, tm_moe)
            split = tm_moe // tm_gather
            xg = _gather_rows(h_flat, tok.reshape(-1, 1, tm_gather), n_active * split, tm_gather)
            yexp = _moe_ffn(xg, w_ex_gate[i], w_ex_up[i], w_ex_down[i], tile_expert, tile_rows, n_active,
                            tm_moe, tf_moe)
            nt = (b * n) // tm
            out = _combine(yexp, pos0.reshape(nt, 1, tm), pos1.reshape(nt, 1, tm), xs, wts, mod, g_final, tm)
    return out
```
